```python
import functools
import jax
import jax.numpy as jnp
from jax import lax
import numpy as np

D_MODEL = 2048
BATCH = 4
SEQ = 4096
DEPTH = 1
DEC_BATCH = 32
DEC_SEQ = 1
PAST_LEN = 16384
PAGE_SIZE = 128

HEAD_DIM = 128
N_HEADS_A = D_MODEL // (2 * HEAD_DIM)
N_KV_A = 1
N_IDX_HEADS = 16
IDX_DIM = 64
IDX_W_SCALE = (N_IDX_HEADS * IDX_DIM) ** -0.5
TOPK_MAX = 256
N_HEADS_B = D_MODEL // (2 * HEAD_DIM)
N_KV_B = 4
A_WIDTH = N_HEADS_A * HEAD_DIM
B_WIDTH = N_HEADS_B * HEAD_DIM
Q_BLOCK = 128
N_GROUPS = 8
EXPERTS_PER_GROUP = 8
N_EXPERTS = N_GROUPS * EXPERTS_PER_GROUP
TOP_K = 2
D_FF_EXPERT = D_MODEL // 4
MOE_BLOCK_ROWS = 128
RMS_EPS = 1e-6
SPLIT_SIZES = (A_WIDTH, N_KV_A * HEAD_DIM, N_KV_A * HEAD_DIM,
               N_IDX_HEADS * IDX_DIM, IDX_DIM, N_IDX_HEADS,
               B_WIDTH, N_KV_B * HEAD_DIM, N_KV_B * HEAD_DIM,
               D_MODEL, D_MODEL)
IN_WIDTH = sum(SPLIT_SIZES)

kernel_name = 'hybrid_dsa_stickbreak_hmoe_step'


def _rms(x, g):
    xf = x.astype(jnp.float32)
    y = xf * lax.rsqrt(jnp.mean(xf * xf, axis=-1, keepdims=True) + RMS_EPS)
    return y.astype(x.dtype) * g


def _adaln(c, w_ada, b_ada):
    m = jax.nn.silu(c) @ w_ada + b_ada
    return jnp.split(m[:, None, :], 6, axis=-1)


def _alibi_slopes(n_heads):
    return jnp.asarray(2.0 ** (-8.0 * np.arange(1, n_heads + 1) / n_heads), dtype=jnp.float32)


def _to_blocks(a):
    b, t = a.shape[:2]
    return jnp.swapaxes(a.reshape((b, t // Q_BLOCK, Q_BLOCK) + a.shape[2:]), 0, 1)


def _from_blocks(o):
    o = jnp.swapaxes(o, 0, 1)
    return o.reshape(o.shape[0], -1, o.shape[-1])


def _as_pages(a):
    return a.reshape((a.shape[0], a.shape[1] // PAGE_SIZE, PAGE_SIZE) + a.shape[2:])


def _split_in(u):
    b, t = u.shape[:2]
    cuts = np.cumsum(SPLIT_SIZES)[:-1].tolist()
    qa, ka, va, qi, ki, wi, qb, kb, vb, ga, gb = jnp.split(u, cuts, axis=-1)
    heads = lambda a, n: a.reshape(b, t, n, -1)
    return (heads(qa, N_HEADS_A), heads(ka, N_KV_A), heads(va, N_KV_A),
            heads(qi, N_IDX_HEADS), ki, wi * IDX_W_SCALE,
            heads(qb, N_HEADS_B), heads(kb, N_KV_B), heads(vb, N_KV_B), ga, gb)


def _take_rows(rows, idx):
    return jax.vmap(lambda r, i: r[i])(rows, idx)


def _index_scores(qi, wi, ki):
    dots = jnp.einsum('bqhd,bsd->bqhs', qi.astype(jnp.float32), ki.astype(jnp.float32))
    return jnp.einsum('bqh,bqhs->bqs', wi.astype(jnp.float32), jax.nn.relu(dots))


def _select(scores, q_pos, k_pos, n_sel):
    admissible = k_pos[None, None, :] <= q_pos[None, :, None]
    _, idx = lax.top_k(jnp.where(admissible, scores, -jnp.inf), n_sel)
    return idx, idx <= q_pos[None, :, None]


def _sparse_attend(q, k_sel, v_sel, sel_pos, valid, q_pos, slopes):
    b, nq, h, dh = q.shape
    g = k_sel.shape[3]
    qg = q.reshape(b, nq, g, h // g, dh)
    logits = jnp.einsum('bqgrd,bqkgd->bqgrk', qg, k_sel).astype(jnp.float32) * dh ** -0.5
    dist = (q_pos[None, :, None] - sel_pos).astype(jnp.float32)
    logits = logits - slopes.reshape(g, h // g)[None, None, :, :, None] * dist[:, :, None, None, :]
    logits = jnp.where(valid[:, :, None, None, :], logits, -jnp.inf)
    p = jax.nn.softmax(logits, axis=-1).astype(v_sel.dtype)
    return jnp.einsum('bqgrk,bqkgd->bqgrd', p, v_sel).reshape(b, nq, h * dh)


def _dsa_prompt(q, k, v, qi, wi, ki, slopes):
    t = q.shape[1]
    n_sel = min(TOPK_MAX, t // 4)
    k_pos = jnp.arange(t)

    def block(args):
        qb, qib, wib, pos = args
        idx, valid = _select(_index_scores(qib, wib, ki), pos, k_pos, n_sel)
        return _sparse_attend(qb, _take_rows(k, idx), _take_rows(v, idx), idx, valid, pos, slopes)

    out = lax.map(block, (_to_blocks(q), _to_blocks(qi), _to_blocks(wi), k_pos.reshape(-1, Q_BLOCK)))
    return _from_blocks(out)


def _dsa_sample(q, k_new, v_new, qi, wi, ki_new, cache_k, cache_v, cache_ki, layer, page_table, slopes):
    db, nq = q.shape[:2]
    past = page_table.shape[1] * PAGE_SIZE
    n_sel = min(TOPK_MAX, (past + nq) // 4)
    q_pos = past + jnp.arange(nq)
    ki_past = cache_ki[layer, page_table].reshape(db, past, IDX_DIM)
    scores = _index_scores(qi, wi, jnp.concatenate([ki_past, ki_new.astype(ki_past.dtype)], axis=1))
    idx, valid = _select(scores, q_pos, jnp.arange(past + nq), n_sel)
    is_past = (idx < past)[..., None, None]
    ip = jnp.minimum(idx, past - 1)
    page = jnp.take_along_axis(page_table, (ip // PAGE_SIZE).reshape(db, -1), axis=1).reshape(idx.shape)
    off = ip % PAGE_SIZE
    inew = jnp.clip(idx - past, 0, nq - 1)
    k_sel = jnp.where(is_past, cache_k[layer, page, off], _take_rows(k_new, inew))
    v_sel = jnp.where(is_past, cache_v[layer, page, off], _take_rows(v_new, inew))
    return _sparse_attend(q, k_sel, v_sel, idx, valid, q_pos, slopes)


def _stick_breaking(q, k, v, q_pos, k_pos):
    b, nq, h, dh = q.shape
    g = k.shape[2]
    qg = q.reshape(b, nq, g, h // g, dh)
    z = jnp.einsum('bqgrd,bsgd->bgrqs', qg, k).astype(jnp.float32) * dh ** -0.5
    before = k_pos[None, :] < q_pos[:, None]
    log_keep = jnp.where(before, jax.nn.log_sigmoid(-z), 0.0)
    tail = lax.cumsum(log_keep, axis=4, reverse=True) - log_keep
    w = jnp.where(before, jnp.exp(jax.nn.log_sigmoid(z) + tail), 0.0).astype(v.dtype)
    return jnp.einsum('bgrqs,bsgd->bqgrd', w, v).reshape(b, nq, h * dh)


def _sb_prompt(q, k, v):
    pos = jnp.arange(q.shape[1])
    out = lax.map(lambda a: _stick_breaking(a[0], k, v, a[1], pos),
                  (_to_blocks(q), pos.reshape(-1, Q_BLOCK)))
    return _from_blocks(out)


def _sb_sample(q, k_new, v_new, cache_k, cache_v, layer, page_table):
    db, nq = q.shape[:2]
    past = page_table.shape[1] * PAGE_SIZE

    def full(pool, new):
        rows = pool[layer, page_table].reshape((db, past) + pool.shape[3:])
        return jnp.concatenate([rows, new.astype(rows.dtype)], axis=1)

    return _stick_breaking(q, full(cache_k, k_new), full(cache_v, v_new),
                           past + jnp.arange(nq), jnp.arange(past + nq))


def _expert_dispatch(h, expert, weight, w_gate_e, w_up_e, w_down_e):
    n, d = h.shape
    a = expert.shape[0]
    e = w_gate_e.shape[0]
    rows = int(min(MOE_BLOCK_ROWS, max(8, a // e)))
    n_blocks = -(-(a + e * (rows - 1)) // rows)
    token = jnp.arange(a) // TOP_K
    order = jnp.argsort(expert)
    e_sorted = expert[order]
    counts = jnp.zeros((e,), jnp.int32).at[expert].add(1)
    padded = (counts + rows - 1) // rows * rows
    start = jnp.cumsum(counts) - counts
    pstart = jnp.cumsum(padded) - padded
    dest = pstart[e_sorted] + jnp.arange(a) - start[e_sorted]
    slot_token = jnp.full((n_blocks * rows,), n, jnp.int32).at[dest].set(token[order])
    slot_weight = jnp.zeros((n_blocks * rows,), weight.dtype).at[dest].set(weight[order])
    block_expert = jnp.minimum(
        jnp.searchsorted(jnp.cumsum(padded), jnp.arange(n_blocks) * rows, side='right'), e - 1)
    xs = jnp.concatenate([h, jnp.zeros((1, d), h.dtype)], axis=0)[slot_token].reshape(n_blocks, rows, d)

    def run(args):
        xb, ex = args
        return (jax.nn.silu(xb @ w_gate_e[ex]) * (xb @ w_up_e[ex])) @ w_down_e[ex]

    ys = lax.map(run, (xs, block_expert)).reshape(-1, d)
    ys = ys * slot_weight[:, None].astype(ys.dtype)
    return jax.ops.segment_sum(ys, slot_token, num_segments=n + 1)[:n]


def _hier_moe(h, w_group, b_group, w_router, b_router, w_gate_e, w_up_e, w_down_e):
    shp = h.shape
    hf = h.reshape(-1, shp[-1])
    g_logits = (hf @ w_group).astype(jnp.float32) + b_group
    g_idx = jnp.argmax(g_logits, axis=-1)
    p_g = jnp.take_along_axis(jax.nn.softmax(g_logits, axis=-1), g_idx[:, None], axis=1)
    e_all = jnp.einsum('nd,gde->nge', hf, w_router).astype(jnp.float32) + b_router
    e_logits = jnp.take_along_axis(e_all, g_idx[:, None, None], axis=1)[:, 0]
    e_val, e_idx = lax.top_k(e_logits, TOP_K)
    weight = p_g * jax.nn.softmax(e_val, axis=-1)
    expert = g_idx[:, None] * EXPERTS_PER_GROUP + e_idx
    out = _expert_dispatch(hf, expert.reshape(-1), weight.reshape(-1), w_gate_e, w_up_e, w_down_e)
    return out.reshape(shp)


def _mix_prompt(slopes, qa, ka, va, qi, ki, wi, qb, kb, vb):
    return _dsa_prompt(qa, ka, va, qi, wi, ki, slopes), _sb_prompt(qb, kb, vb)


def _mix_sample(caches, layer, page_table, slopes, qa, ka, va, qi, ki, wi, qb, kb, vb):
    c_ak, c_av, c_ik, c_bk, c_bv = caches
    o_a = _dsa_sample(qa, ka, va, qi, wi, ki, c_ak, c_av, c_ik, layer, page_table, slopes)
    o_b = _sb_sample(qb, kb, vb, c_bk, c_bv, layer, page_table)
    return o_a, o_b


def _layer(x, c, mix, params):
    (w_ada, b_ada, g_mix, g_ffn, w_in, w_up_a, w_up_b, w_out,
     w_group, b_group, w_router, b_router, w_gate_e, w_up_e, w_down_e) = params
    sh1, sc1, gt1, sh2, sc2, gt2 = _adaln(c, w_ada, b_ada)
    h = _rms(x, g_mix) * (1.0 + sc1) + sh1
    qa, ka, va, qi, ki, wi, qb, kb, vb, ga, gb = _split_in(h @ w_in)
    o_a, o_b = mix(qa, ka, va, qi, ki, wi, qb, kb, vb)
    merged = jax.nn.sigmoid(ga) * (o_a @ w_up_a) + jax.nn.sigmoid(gb) * (o_b @ w_up_b)
    x = x + gt1 * (merged @ w_out)
    h2 = _rms(x, g_ffn) * (1.0 + sc2) + sh2
    x = x + gt2 * _hier_moe(h2, w_group, b_group, w_router, b_router, w_gate_e, w_up_e, w_down_e)
    return x, (ka, va, ki, kb, vb)


def setup_inputs(seed: int = 0) -> dict:
    key = jax.random.key(seed)
    ks = iter(jax.random.split(key, 32))

    def nrm(shape, scale=1.0):
        return jax.random.normal(next(ks), shape, jnp.float32) * scale

    d = D_MODEL
    nl = DEPTH
    n_pages = PAST_LEN // PAGE_SIZE
    n_used = DEC_BATCH * n_pages
    n_pool = n_used + (n_used + 3) // 4
    page_table = jax.random.permutation(next(ks), n_pool)[:n_used].reshape(DEC_BATCH, n_pages).astype(jnp.int32)
    return {
        'x_prompt': nrm((BATCH, SEQ, d)),
        'x_sample': nrm((DEC_BATCH, DEC_SEQ, d)),
        'cache_a_k': nrm((nl, n_pool, PAGE_SIZE, N_KV_A, HEAD_DIM)),
        'cache_a_v': nrm((nl, n_pool, PAGE_SIZE, N_KV_A, HEAD_DIM)),
        'cache_idx_k': nrm((nl, n_pool, PAGE_SIZE, IDX_DIM)),
        'cache_b_k': nrm((nl, n_pool, PAGE_SIZE, N_KV_B, HEAD_DIM)),
        'cache_b_v': nrm((nl, n_pool, PAGE_SIZE, N_KV_B, HEAD_DIM)),
        'page_table': page_table,
        'c_prompt': nrm((BATCH, d)),
        'c_sample': nrm((DEC_BATCH, d)),
        'w_ada': nrm((nl, d, 6 * d), 0.5 * d ** -0.5),
        'b_ada': nrm((nl, 6 * d), 0.02),
        'g_mix': 1.0 + nrm((nl, d), 0.05),
        'g_ffn': 1.0 + nrm((nl, d), 0.05),
        'g_final': 1.0 + nrm((d,), 0.05),
        'w_in': nrm((nl, d, IN_WIDTH), d ** -0.5),
        'w_up_a': nrm((nl, A_WIDTH, d), A_WIDTH ** -0.5),
        'w_up_b': nrm((nl, B_WIDTH, d), B_WIDTH ** -0.5),
        'w_out': nrm((nl, d, d), d ** -0.5),
        'w_group': nrm((nl, d, N_GROUPS), d ** -0.5),
        'b_group': nrm((nl, N_GROUPS), 0.01),
        'w_router': nrm((nl, N_GROUPS, d, EXPERTS_PER_GROUP), d ** -0.5),
        'b_router': nrm((nl, N_GROUPS, EXPERTS_PER_GROUP), 0.01),
        'w_gate_e': nrm((nl, N_EXPERTS, d, D_FF_EXPERT), d ** -0.5),
        'w_up_e': nrm((nl, N_EXPERTS, d, D_FF_EXPERT), d ** -0.5),
        'w_down_e': nrm((nl, N_EXPERTS, D_FF_EXPERT, d), D_FF_EXPERT ** -0.5),
    }


def reference(x_prompt, x_sample, cache_a_k, cache_a_v, cache_idx_k, cache_b_k, cache_b_v,
              page_table, c_prompt, c_sample, w_ada, b_ada, g_mix, g_ffn, g_final, w_in,
              w_up_a, w_up_b, w_out, w_group, b_group, w_router, b_router,
              w_gate_e, w_up_e, w_down_e):
    slopes = _alibi_slopes(N_HEADS_A)
    stacked = (w_ada, b_ada, g_mix, g_ffn, w_in, w_up_a, w_up_b, w_out,
               w_group, b_group, w_router, b_router, w_gate_e, w_up_e, w_down_e)
    caches = (cache_a_k, cache_a_v, cache_idx_k, cache_b_k, cache_b_v)
    xp, xs = x_prompt, x_sample
    rows_p, rows_s = [], []
    for layer in range(DEPTH):
        params = tuple(w[layer] for w in stacked)
        xp, new_p = _layer(xp, c_prompt, functools.partial(_mix_prompt, slopes), params)
        xs, new_s = _layer(xs, c_sample, functools.partial(_mix_sample, caches, layer, page_table, slopes), params)
        rows_p.append(tuple(_as_pages(r) for r in new_p))
        rows_s.append(new_s)
    y_prompt = _rms(xp, g_final)
    y_sample = _rms(xs, g_final)
    a_k_p, a_v_p, i_k_p, b_k_p, b_v_p = (jnp.stack(r) for r in zip(*rows_p))
    a_k_s, a_v_s, i_k_s, b_k_s, b_v_s = (jnp.stack(r) for r in zip(*rows_s))
    return (y_prompt, y_sample, a_k_p, a_v_p, i_k_p, b_k_p, b_v_p, a_k_s, a_v_s, i_k_s, b_k_s, b_v_s)
```

```python
import functools
from typing import NamedTuple

import numpy as np
import jax
import jax.numpy as jnp
from jax import lax
from jax.experimental import pallas as pl
from jax.experimental.pallas import tpu as pltpu

F32 = jnp.float32
BF16 = jnp.bfloat16
I32 = jnp.int32

LANES = 128
INT_MIN = -(2 ** 31)
RMS_EPS = 1e-6
NEG_BIG = -1e30
VMEM_LIMIT = 48 * 1024 * 1024


class Cfg(NamedTuple):
    d_model: int = 2048
    head_dim: int = 128
    n_heads_a: int = 8
    n_idx_heads: int = 16
    idx_dim: int = 64
    n_heads_b: int = 8
    n_kv_b: int = 4
    topk_max: int = 256
    n_groups: int = 8
    epg: int = 8
    top_k: int = 2
    d_ff: int = 512
    page_size: int = 128
    tn_in: int = 512
    tm_in: int = 1024
    tq_a: int = 128
    tk_a: int = 256
    t_b: int = 256
    tm_merge: int = 256
    moe_rows: int = 128
    tm_comb: int = 128
    pages_per_step: int = 8


def _cparams(sem):
    return pltpu.CompilerParams(dimension_semantics=sem, vmem_limit_bytes=VMEM_LIMIT)


def _round_up(a, b):
    return -(-a // b) * b


def _layout(cfg):
    d, hd = cfg.d_model, cfg.head_dim
    segs = [("ga", d), ("gb", d), ("qa", cfg.n_heads_a * hd), ("qi", cfg.n_idx_heads * cfg.idx_dim),
            ("qb", cfg.n_heads_b * hd), ("kb", cfg.n_kv_b * hd), ("vb", cfg.n_kv_b * hd),
            ("ka", hd), ("va", hd), ("kiwi", LANES)]
    off, o = {}, 0
    for name, w in segs:
        off[name] = (o, w)
        o += w
    return off, _round_up(o, cfg.tn_in)


def _permute_w_in(w_in, cfg):
    hd = cfg.head_dim
    sizes = [cfg.n_heads_a * hd, hd, hd, cfg.n_idx_heads * cfg.idx_dim, cfg.idx_dim, cfg.n_idx_heads,
             cfg.n_heads_b * hd, cfg.n_kv_b * hd, cfg.n_kv_b * hd, cfg.d_model, cfg.d_model]
    names = ["qa", "ka", "va", "qi", "ki", "wi", "qb", "kb", "vb", "ga", "gb"]
    cuts = np.concatenate([[0], np.cumsum(sizes)])
    src = {n: w_in[:, int(cuts[i]):int(cuts[i + 1])] for i, n in enumerate(names)}
    off, width = _layout(cfg)
    assert cfg.idx_dim + cfg.n_idx_heads <= LANES
    kiwi = jnp.concatenate(
        [src["ki"], src["wi"], jnp.zeros((w_in.shape[0], LANES - cfg.idx_dim - cfg.n_idx_heads), w_in.dtype)], axis=1)
    parts = [src["ga"], src["gb"], src["qa"], src["qi"], src["qb"], src["kb"], src["vb"], src["ka"], src["va"], kiwi]
    used = sum(p.shape[1] for p in parts)
    if width > used:
        parts.append(jnp.zeros((w_in.shape[0], width - used), w_in.dtype))
    return jnp.concatenate(parts, axis=1)


def _adaln_kernel(c_ref, w_ref, b_ref, o_ref):
    c = c_ref[...]
    s = (c * (1.0 / (1.0 + jnp.exp(-c)))).astype(BF16)
    o_ref[...] = jnp.dot(s, w_ref[...].astype(BF16), preferred_element_type=F32) + b_ref[...]


def _adaln(c_all, w_ada, b_ada):
    mc, d = c_all.shape
    n = w_ada.shape[1]
    tn = 1024 if n % 1024 == 0 else n
    return pl.pallas_call(
        _adaln_kernel,
        out_shape=jax.ShapeDtypeStruct((mc, n), F32),
        grid=(n // tn,),
        in_specs=[pl.BlockSpec((mc, d), lambda j: (0, 0)),
                  pl.BlockSpec((d, tn), lambda j: (0, j)),
                  pl.BlockSpec((1, tn), lambda j: (0, j))],
        out_specs=pl.BlockSpec((mc, tn), lambda j: (0, j)),
        compiler_params=_cparams(("arbitrary",)),
        name="adaln",
    )(c_all, w_ada, b_ada.reshape(1, n))


def _rms_mod_kernel(x_ref, g_ref, m_ref, o_ref):
    x = x_ref[0]
    y = x * lax.rsqrt(jnp.mean(x * x, axis=-1, keepdims=True) + RMS_EPS) * g_ref[...]
    o_ref[0] = (y * (1.0 + m_ref[0, 1]) + m_ref[0, 0]).astype(o_ref.dtype)


def _rms_mod(x3, g, mod, ts):
    bx, t, d = x3.shape
    r = mod.shape[2]
    return pl.pallas_call(
        _rms_mod_kernel,
        out_shape=jax.ShapeDtypeStruct((bx, t, d), BF16),
        grid=(bx, t // ts),
        in_specs=[pl.BlockSpec((1, ts, d), lambda b, i: (b, i, 0)),
                  pl.BlockSpec((1, d), lambda b, i: (0, 0)),
                  pl.BlockSpec((1, 6, r, d), lambda b, i: (b, 0, 0, 0))],
        out_specs=pl.BlockSpec((1, ts, d), lambda b, i: (b, i, 0)),
        compiler_params=_cparams(("arbitrary", "arbitrary")),
        name="rms_mod",
    )(x3, g.reshape(1, d), mod)


def _mm_kernel(a_ref, w_ref, o_ref, wbf_ref):
    @pl.when(pl.program_id(1) == 0)
    def _():
        wbf_ref[...] = w_ref[...].astype(BF16)

    o_ref[...] = jnp.dot(a_ref[...], wbf_ref[...], preferred_element_type=F32).astype(o_ref.dtype)


def _matmul(a, w, tm, tn, out_dtype=F32):
    m, k = a.shape
    n = w.shape[1]
    return pl.pallas_call(
        _mm_kernel,
        out_shape=jax.ShapeDtypeStruct((m, n), out_dtype),
        grid=(n // tn, m // tm),
        in_specs=[pl.BlockSpec((tm, k), lambda j, i: (i, 0)),
                  pl.BlockSpec((k, tn), lambda j, i: (0, j))],
        out_specs=pl.BlockSpec((tm, tn), lambda j, i: (i, j)),
        scratch_shapes=[pltpu.VMEM((k, tn), BF16)],
        compiler_params=_cparams(("arbitrary", "arbitrary")),
        name="in_proj",
    )(a, w)


def _sortable_key(s):
    bits = pltpu.bitcast(s, I32)
    return jnp.where(bits < 0, bits ^ jnp.int32(0x7FFFFFFF), bits)


def _kth_largest_key(count_ge, rows, k):
    def body(bi, res):
        bit = lax.shift_left(jnp.int32(1), jnp.int32(31) - bi)
        cand = res | bit
        cnt = count_ge(cand ^ jnp.int32(INT_MIN))
        return jnp.where(cnt >= k, cand, res)

    res = lax.fori_loop(0, 32, body, jnp.zeros((rows, 1), I32))
    return res ^ jnp.int32(INT_MIN)


def _tie_cutoff(count_tie_lt, rows, need, nbits):
    def body(bi, x):
        cand = x | lax.shift_left(jnp.int32(1), jnp.int32(nbits - 1) - bi)
        return jnp.where(count_tie_lt(cand) < need, cand, x)

    return lax.fori_loop(0, nbits, body, jnp.zeros((rows, 1), I32))


def _fold_lanes(x):
    acc = x[:, 0:LANES]
    for j in range(1, x.shape[1] // LANES):
        acc = acc + x[:, j * LANES:(j + 1) * LANES]
    return acc


def _dot_nt(a, b):
    return lax.dot_general(a, b, (((1,), (1,)), ((), ())), preferred_element_type=F32)


def _dsa_prompt_kernel(qa_ref, qi_ref, kwq_ref, kw_ref, ka_ref, va_ref, o_ref,
                       ka_s, kb_s, kbf_s, vbf_s, keys_s, bias_s, xcut_s, *, cfg, tq, tk, n_sel):
    hd, idim, ni, ha = cfg.head_dim, cfg.idx_dim, cfg.n_idx_heads, cfg.n_heads_a
    s_total = keys_s.shape[1]
    i = pl.program_id(1)
    t0 = i * tq
    nk = (t0 + tq + tk - 1) // tk

    @pl.when(i == 0)
    def _stage():
        kw = kw_ref[0]
        lane = lax.broadcasted_iota(I32, kw.shape, 1)
        k_lo = jnp.where(lane < idim, kw, 0.0)
        ka_s[...] = k_lo.astype(BF16)
        kb_s[...] = pltpu.roll(k_lo, idim, 1).astype(BF16)
        kbf_s[...] = ka_ref[0].astype(BF16)
        vbf_s[...] = va_ref[0].astype(BF16)

    w_q = kwq_ref[0] * (float(ni * idim) ** -0.5)
    row = t0 + lax.broadcasted_iota(I32, (tq, tk), 0)
    col0 = lax.broadcasted_iota(I32, (tq, tk), 1)

    def score_chunk(c, carry):
        ks = pl.multiple_of(c * tk, tk)
        k_even = ka_s[pl.ds(ks, tk), :]
        k_odd = kb_s[pl.ds(ks, tk), :]
        acc = jnp.zeros((tq, tk), F32)
        for p in range(ni // 2):
            qp = qi_ref[0, :, p * LANES:(p + 1) * LANES].astype(BF16)
            for par, kk in ((0, k_even), (1, k_odd)):
                h = 2 * p + par
                acc = acc + w_q[:, idim + h:idim + h + 1] * jnp.maximum(_dot_nt(qp, kk), 0.0)
        s = jnp.where(ks + col0 <= row, acc, -jnp.inf)
        keys_s[:, pl.ds(ks, tk)] = _sortable_key(s)
        return carry

    lax.fori_loop(0, nk, score_chunk, 0)

    def count_ge(thr):
        def body(c, cnt):
            ks = pl.multiple_of(c * tk, tk)
            return cnt + _fold_lanes((keys_s[:, pl.ds(ks, tk)] >= thr).astype(I32))

        cnt = lax.fori_loop(0, nk, body, jnp.zeros((tq, LANES), I32))
        return jnp.sum(cnt, axis=-1, keepdims=True)

    thr = _kth_largest_key(count_ge, tq, n_sel)

    xcut_s[...] = jnp.full(xcut_s.shape, s_total, I32)

    @pl.when(jnp.max(count_ge(thr)) > n_sel)
    def _ties():
        need = n_sel - count_ge(thr + 1)

        def count_tie_lt(cand):
            def body(c, cnt):
                ks = pl.multiple_of(c * tk, tk)
                hit = (keys_s[:, pl.ds(ks, tk)] == thr) & (ks + col0 < cand)
                return cnt + _fold_lanes(hit.astype(I32))

            cnt = lax.fori_loop(0, nk, body, jnp.zeros((tq, LANES), I32))
            return jnp.sum(cnt, axis=-1, keepdims=True)

        x = _tie_cutoff(count_tie_lt, tq, need, max(1, (s_total - 1).bit_length()))
        xcut_s[...] = jnp.broadcast_to(x, xcut_s.shape)

    xcut = xcut_s[:, 0:1]

    def mask_chunk(c, carry):
        ks = pl.multiple_of(c * tk, tk)
        col = ks + col0
        key = keys_s[:, pl.ds(ks, tk)]
        sel = ((key > thr) | ((key == thr) & (col <= xcut))) & (col <= row)
        bias_s[:, pl.ds(ks, tk)] = jnp.where(sel, 0.0, NEG_BIG)
        return carry

    lax.fori_loop(0, nk, mask_chunk, 0)
    scale = float(hd) ** -0.5

    for h in range(ha):
        qh = qa_ref[0, :, h * hd:(h + 1) * hd].astype(BF16)
        slope = float(2.0 ** (-8.0 * (h + 1) / ha))

        def att_chunk(c, carry, qh=qh, slope=slope):
            m, l, acc = carry
            ks = pl.multiple_of(c * tk, tk)
            col = ks + col0
            bias = bias_s[:, pl.ds(ks, tk)]
            sel = bias == 0.0
            lg = _dot_nt(qh, kbf_s[pl.ds(ks, tk), :]) * scale - slope * (row - col).astype(F32) + bias
            m_new = jnp.maximum(m, jnp.max(lg, axis=-1, keepdims=True))
            p = jnp.where(sel, jnp.exp(lg - m_new), 0.0)
            alpha = jnp.exp(m - m_new)
            l = alpha * l + jnp.sum(p, axis=-1, keepdims=True)
            acc = alpha * acc + jnp.dot(p.astype(BF16), vbf_s[pl.ds(ks, tk), :], preferred_element_type=F32)
            return m_new, l, acc

        init = (jnp.full((tq, 1), NEG_BIG, F32), jnp.zeros((tq, 1), F32), jnp.zeros((tq, hd), F32))
        _, l, acc = lax.fori_loop(0, nk, att_chunk, init)
        o_ref[0, :, h * hd:(h + 1) * hd] = (acc / l).astype(o_ref.dtype)


def _dsa_prompt(u3, cfg):
    b, s, _ = u3.shape
    off, _ = _layout(cfg)
    tq, tk = min(cfg.tq_a, s), min(cfg.tk_a, s)
    n_sel = min(cfg.topk_max, s // 4)
    aw = cfg.n_heads_a * cfg.head_dim
    iw = cfg.n_idx_heads * cfg.idx_dim
    assert 2 * cfg.idx_dim == LANES and s % tk == 0 and s % tq == 0 and tk >= n_sel
    cb = lambda name, w: off[name][0] // w
    kern = functools.partial(_dsa_prompt_kernel, cfg=cfg, tq=tq, tk=tk, n_sel=n_sel)
    return pl.pallas_call(
        kern,
        out_shape=jax.ShapeDtypeStruct((b, s, aw), BF16),
        grid=(b, s // tq),
        in_specs=[pl.BlockSpec((1, tq, aw), lambda bi, i: (bi, i, cb("qa", aw))),
                  pl.BlockSpec((1, tq, iw), lambda bi, i: (bi, i, cb("qi", iw))),
                  pl.BlockSpec((1, tq, LANES), lambda bi, i: (bi, i, cb("kiwi", LANES))),
                  pl.BlockSpec((1, s, LANES), lambda bi, i: (bi, 0, cb("kiwi", LANES))),
                  pl.BlockSpec((1, s, LANES), lambda bi, i: (bi, 0, cb("ka", LANES))),
                  pl.BlockSpec((1, s, LANES), lambda bi, i: (bi, 0, cb("va", LANES)))],
        out_specs=pl.BlockSpec((1, tq, aw), lambda bi, i: (bi, i, 0)),
        scratch_shapes=[pltpu.VMEM((s, LANES), BF16), pltpu.VMEM((s, LANES), BF16),
                        pltpu.VMEM((s, LANES), BF16), pltpu.VMEM((s, LANES), BF16),
                        pltpu.VMEM((tq, s), I32), pltpu.VMEM((tq, s), F32), pltpu.VMEM((tq, LANES), I32)],
        compiler_params=_cparams(("arbitrary", "arbitrary")),
        name="dsa_prompt",
    )(u3, u3, u3, u3, u3, u3)


def _sb_terms(q, kc, scale):
    z = _dot_nt(q, kc) * scale
    lk = -(jnp.maximum(z, 0.0) + jnp.log(1.0 + jnp.exp(-jnp.abs(z))))
    return z, lk


def _suffix_in_chunk(lk, upper):
    hi = lk.astype(BF16)
    lo = (lk - hi.astype(F32)).astype(BF16)
    return (jnp.dot(hi, upper, preferred_element_type=F32) + jnp.dot(lo, upper, preferred_element_type=F32))


def _strict_lower_ones(n):
    return (lax.broadcasted_iota(I32, (n, n), 0) > lax.broadcasted_iota(I32, (n, n), 1)).astype(BF16)


def _sb_prompt_kernel(q_ref, k_ref, v_ref, o_ref, kbf_s, vbf_s, *, t, hd):
    i = pl.program_id(2)

    @pl.when(i == 0)
    def _stage():
        kbf_s[...] = k_ref[0].astype(BF16)
        vbf_s[...] = v_ref[0].astype(BF16)

    q = q_ref[0].astype(BF16)
    scale = float(hd) ** -0.5
    upper = _strict_lower_ones(t)
    before = lax.broadcasted_iota(I32, (t, t), 1) < lax.broadcasted_iota(I32, (t, t), 0)

    ks = pl.multiple_of(i * t, t)
    z, lk = _sb_terms(q, kbf_s[pl.ds(ks, t), :], scale)
    lkm = jnp.where(before, lk, 0.0)
    tail = _suffix_in_chunk(lkm, upper)
    w = jnp.where(before, jnp.exp(z + lk + tail), 0.0)
    acc = jnp.dot(w.astype(BF16), vbf_s[pl.ds(ks, t), :], preferred_element_type=F32)
    carry = jnp.sum(lkm, axis=-1, keepdims=True)

    def body(j, st):
        carry, acc = st
        ks = pl.multiple_of((i - 1 - j) * t, t)
        z, lk = _sb_terms(q, kbf_s[pl.ds(ks, t), :], scale)
        tail = carry + _suffix_in_chunk(lk, upper)
        w = jnp.exp(z + lk + tail)
        acc = acc + jnp.dot(w.astype(BF16), vbf_s[pl.ds(ks, t), :], preferred_element_type=F32)
        return carry + jnp.sum(lk, axis=-1, keepdims=True), acc

    _, acc = lax.fori_loop(0, i, body, (carry, acc))
    o_ref[0] = acc.astype(o_ref.dtype)


def _sb_prompt(u3, cfg):
    b, s, _ = u3.shape
    off, _ = _layout(cfg)
    hd, hb, g = cfg.head_dim, cfg.n_heads_b, cfg.n_kv_b
    rep = hb // g
    t = min(cfg.t_b, s)
    assert s % t == 0
    qb0, kb0, vb0 = off["qb"][0] // hd, off["kb"][0] // hd, off["vb"][0] // hd
    kern = functools.partial(_sb_prompt_kernel, t=t, hd=hd)
    return pl.pallas_call(
        kern,
        out_shape=jax.ShapeDtypeStruct((b, s, hb * hd), BF16),
        grid=(b, hb, s // t),
        in_specs=[pl.BlockSpec((1, t, hd), lambda bi, h, i: (bi, i, qb0 + h)),
                  pl.BlockSpec((1, s, hd), lambda bi, h, i: (bi, 0, kb0 + h // rep)),
                  pl.BlockSpec((1, s, hd), lambda bi, h, i: (bi, 0, vb0 + h // rep))],
        out_specs=pl.BlockSpec((1, t, hd), lambda bi, h, i: (bi, i, h)),
        scratch_shapes=[pltpu.VMEM((s, hd), BF16), pltpu.VMEM((s, hd), BF16)],
        compiler_params=_cparams(("arbitrary", "arbitrary", "arbitrary")),
        name="sb_prompt",
    )(u3, u3, u3)


def _page_specs(n, block, page_of):
    zeros = (0,) * (len(block) - 1)
    return [pl.BlockSpec((None,) + tuple(block[1:]),
                         functools.partial(lambda b, j, pt, r: (page_of(b, j, r, pt),) + zeros, r=r))
            for r in range(n)]


def _dsa_sample_score_kernel(pt_ref, q_ref, w_ref, *rest, pg):
    k_refs, o_ref = rest[:pg], rest[pg]
    q = q_ref[0].astype(BF16)
    w = w_ref[0]
    for r in range(pg):
        d = _dot_nt(q, k_refs[r][...].astype(BF16))
        o_ref[0, 0, r:r + 1, :] = jnp.sum(w * jnp.maximum(d, 0.0), axis=0, keepdims=True)


def _dsa_sample_scores(qi3, wi3, cache_ik, page_table, cfg):
    db, ni, idim = qi3.shape
    pool, ps, _ = cache_ik.shape
    npg = page_table.shape[1]
    pg = min(cfg.pages_per_step, npg)
    assert npg % pg == 0 and ps == LANES
    page_of = lambda b, j, r, pt: pt[b * npg + j * pg + r]
    grid_spec = pltpu.PrefetchScalarGridSpec(
        num_scalar_prefetch=1,
        grid=(db, npg // pg),
        in_specs=[pl.BlockSpec((1, ni, idim), lambda b, j, pt: (b, 0, 0)),
                  pl.BlockSpec((1, ni, 1), lambda b, j, pt: (b, 0, 0))]
                 + _page_specs(pg, (1, ps, idim), page_of),
        out_specs=pl.BlockSpec((1, 1, pg, ps), lambda b, j, pt: (b, j, 0, 0)),
    )
    out = pl.pallas_call(
        functools.partial(_dsa_sample_score_kernel, pg=pg),
        out_shape=jax.ShapeDtypeStruct((db, npg // pg, pg, ps), F32),
        grid_spec=grid_spec,
        compiler_params=_cparams(("arbitrary", "arbitrary")),
        name="dsa_sample_scores",
    )(page_table.reshape(-1), qi3, wi3, *([cache_ik] * pg))
    return out


def _dsa_sample_select_kernel(sc_ref, qi_ref, kw_ref, bias_ref, bnew_ref, keys_s, *, cfg, n_sel, chunk):
    idim, ni = cfg.idx_dim, cfg.n_idx_heads
    db, p = sc_ref.shape
    nchunk = p // chunk
    kw = kw_ref[...]
    ki = kw[:, 0:idim].astype(BF16).astype(F32)
    s_new = jnp.zeros((db, 1), F32)
    for h in range(ni):
        qh = qi_ref[:, h * idim:(h + 1) * idim].astype(BF16).astype(F32)
        d = jnp.sum(qh * ki, axis=-1, keepdims=True)
        s_new = s_new + (kw[:, idim + h:idim + h + 1] * (float(ni * idim) ** -0.5)) * jnp.maximum(d, 0.0)
    kn1 = _sortable_key(jnp.broadcast_to(s_new, (db, LANES)))[:, 0:1]
    for c in range(nchunk):
        keys_s[:, c * chunk:(c + 1) * chunk] = _sortable_key(sc_ref[:, c * chunk:(c + 1) * chunk])
    idx0 = lax.broadcasted_iota(I32, (db, chunk), 1)

    def count(pred, new_hit):
        cnt = jnp.zeros((db, LANES), I32)
        for c in range(nchunk):
            cnt = cnt + _fold_lanes(pred(keys_s[:, c * chunk:(c + 1) * chunk], c * chunk + idx0).astype(I32))
        return jnp.sum(cnt, axis=-1, keepdims=True) + new_hit.astype(I32)

    thr = _kth_largest_key(lambda t: count(lambda k, i: k >= t, kn1 >= t), db, n_sel)
    need = n_sel - count(lambda k, i: k > thr, kn1 > thr)
    xcut = _tie_cutoff(lambda x: count(lambda k, i: (k == thr) & (i < x), (kn1 == thr) & (p < x)),
                       db, need, max(1, p.bit_length()))
    for c in range(nchunk):
        k = keys_s[:, c * chunk:(c + 1) * chunk]
        sel = (k > thr) | ((k == thr) & (c * chunk + idx0 <= xcut))
        bias_ref[:, c * chunk:(c + 1) * chunk] = jnp.where(sel, 0.0, NEG_BIG)
    sel_new = (kn1 > thr) | ((kn1 == thr) & (p <= xcut))
    bnew_ref[...] = jnp.broadcast_to(jnp.where(sel_new, 0.0, NEG_BIG), (db, LANES))


def _dsa_sample_select(scores, u_s, cfg):
    db, p = scores.shape
    off, _ = _layout(cfg)
    iw = cfg.n_idx_heads * cfg.idx_dim
    n_sel = min(cfg.topk_max, (p + 1) // 4)
    chunk = min(2048, p)
    assert p % chunk == 0 and p + 1 >= n_sel
    kern = functools.partial(_dsa_sample_select_kernel, cfg=cfg, n_sel=n_sel, chunk=chunk)
    return pl.pallas_call(
        kern,
        out_shape=(jax.ShapeDtypeStruct((db, p), F32), jax.ShapeDtypeStruct((db, LANES), F32)),
        grid=(1,),
        in_specs=[pl.BlockSpec((db, p), lambda i: (0, 0)),
                  pl.BlockSpec((db, iw), lambda i: (0, off["qi"][0] // iw)),
                  pl.BlockSpec((db, LANES), lambda i: (0, off["kiwi"][0] // LANES))],
        out_specs=(pl.BlockSpec((db, p), lambda i: (0, 0)), pl.BlockSpec((db, LANES), lambda i: (0, 0))),
        scratch_shapes=[pltpu.VMEM((db, p), I32)],
        compiler_params=_cparams(("arbitrary",)),
        name="dsa_sample_select",
    )(scores, u_s, u_s)


def _dsa_sample_attend_kernel(pt_ref, q_ref, bias_ref, bnew_ref, knew_ref, vnew_ref, *rest,
                              pg, cfg, past):
    k_refs, v_refs = rest[:pg], rest[pg:2 * pg]
    o_ref, m_s, l_s, acc_s = rest[2 * pg:]
    hd, ha = cfg.head_dim, cfg.n_heads_a
    j = pl.program_id(1)
    scale = float(hd) ** -0.5

    @pl.when(j == 0)
    def _init():
        m_s[...] = jnp.full(m_s.shape, NEG_BIG, F32)
        l_s[...] = jnp.zeros(l_s.shape, F32)
        acc_s[...] = jnp.zeros(acc_s.shape, F32)

    q = q_ref[0].astype(BF16)
    hidx = lax.broadcasted_iota(I32, (ha, 1), 0).astype(F32)
    slopes = jnp.exp((-8.0 * float(np.log(2.0)) / ha) * (hidx + 1.0))
    lane = lax.broadcasted_iota(I32, (1, LANES), 1)

    def update(lg, bias, v):
        sel = jnp.broadcast_to(bias == 0.0, lg.shape)
        lg = lg + bias
        m = m_s[...]
        m_new = jnp.maximum(m, jnp.max(lg, axis=-1, keepdims=True))
        p = jnp.where(sel, jnp.exp(lg - m_new), 0.0)
        alpha = jnp.exp(m - m_new)
        l_s[...] = alpha * l_s[...] + jnp.sum(p, axis=-1, keepdims=True)
        acc_s[...] = alpha * acc_s[...] + jnp.dot(p.astype(BF16), v, preferred_element_type=F32)
        m_s[...] = m_new

    for r in range(pg):
        pos = (j * pg + r) * LANES + lane
        dist = (past - pos).astype(F32)
        lg = _dot_nt(q, k_refs[r][...].astype(BF16)) * scale - slopes * dist
        update(lg, bias_ref[0, 0, r:r + 1, :], v_refs[r][...].astype(BF16))

    @pl.when(j == pl.num_programs(1) - 1)
    def _fin():
        kn = jnp.broadcast_to(knew_ref[0], (16, hd)).astype(BF16)
        vn = jnp.broadcast_to(vnew_ref[0], (16, hd)).astype(BF16)
        lg = _dot_nt(q, kn) * scale
        first = lax.broadcasted_iota(I32, (1, 16), 1) == 0
        update(lg, jnp.where(first, bnew_ref[0][:, 0:16], NEG_BIG), vn)
        o_ref[0] = (acc_s[...] / l_s[...]).astype(o_ref.dtype)


def _dsa_sample_attend(qa3, bias4, bias_new, u_s3, cache_k, cache_v, page_table, cfg):
    db, ha, hd = qa3.shape
    npg = page_table.shape[1]
    pg = bias4.shape[2]
    ps = cache_k.shape[1]
    off, _ = _layout(cfg)
    page_of = lambda b, j, r, pt: pt[b * npg + j * pg + r]
    kern = functools.partial(_dsa_sample_attend_kernel, pg=pg, cfg=cfg, past=npg * ps)
    grid_spec = pltpu.PrefetchScalarGridSpec(
        num_scalar_prefetch=1,
        grid=(db, npg // pg),
        in_specs=[pl.BlockSpec((1, ha, hd), lambda b, j, pt: (b, 0, 0)),
                  pl.BlockSpec((1, 1, pg, ps), lambda b, j, pt: (b, j, 0, 0)),
                  pl.BlockSpec((1, 1, LANES), lambda b, j, pt: (b, 0, 0)),
                  pl.BlockSpec((1, 1, hd), lambda b, j, pt: (b, 0, off["ka"][0] // hd)),
                  pl.BlockSpec((1, 1, hd), lambda b, j, pt: (b, 0, off["va"][0] // hd))]
                 + _page_specs(pg, (1, ps, hd), page_of) + _page_specs(pg, (1, ps, hd), page_of),
        out_specs=pl.BlockSpec((1, ha, hd), lambda b, j, pt: (b, 0, 0)),
        scratch_shapes=[pltpu.VMEM((ha, 1), F32), pltpu.VMEM((ha, 1), F32), pltpu.VMEM((ha, hd), F32)],
    )
    return pl.pallas_call(
        kern,
        out_shape=jax.ShapeDtypeStruct((db, ha, hd), BF16),
        grid_spec=grid_spec,
        compiler_params=_cparams(("arbitrary", "arbitrary")),
        name="dsa_sample_attend",
    )(page_table.reshape(-1), qa3, bias4, bias_new.reshape(db, 1, LANES),
      u_s3, u_s3, *([cache_k] * pg), *([cache_v] * pg))


def _sb_sample_kernel(pt_ref, q_ref, *rest, pg, cfg):
    k_refs, v_refs = rest[:pg], rest[pg:2 * pg]
    o_ref, carry_s, acc_s = rest[2 * pg:]
    hd, hb, g = cfg.head_dim, cfg.n_heads_b, cfg.n_kv_b
    rep = hb // g
    j = pl.program_id(1)
    scale = float(hd) ** -0.5

    @pl.when(j == 0)
    def _init():
        carry_s[...] = jnp.zeros(carry_s.shape, F32)
        acc_s[...] = jnp.zeros(acc_s.shape, F32)

    q = q_ref[0].astype(BF16)
    upper = _strict_lower_ones(LANES)
    for r in range(pg):
        z, lk = _sb_terms(q, k_refs[r][...].astype(BF16), scale)
        tail = carry_s[...] + _suffix_in_chunk(lk, upper)
        w = jnp.exp(z + lk + tail)
        acc_s[...] += jnp.dot(w.astype(BF16), v_refs[r][...].astype(BF16), preferred_element_type=F32)
        carry_s[...] += jnp.sum(lk, axis=-1, keepdims=True)

    @pl.when(j == pl.num_programs(1) - 1)
    def _fin():
        acc = acc_s[...]
        head_group = lax.broadcasted_iota(I32, (hb, hd), 0) // rep
        out = jnp.zeros((hb, hd), F32)
        for gi in range(g):
            out = out + jnp.where(head_group == gi, acc[:, gi * hd:(gi + 1) * hd], 0.0)
        o_ref[0] = out.astype(o_ref.dtype)


def _sb_sample(q_bd, cache_k, cache_v, page_table, cfg):
    db, hb, gw = q_bd.shape
    npg = page_table.shape[1]
    ps = cache_k.shape[1]
    pg = min(cfg.pages_per_step, npg)
    assert npg % pg == 0 and ps == LANES
    page_of = lambda b, j, r, pt: pt[b * npg + (npg - 1 - (j * pg + r))]
    grid_spec = pltpu.PrefetchScalarGridSpec(
        num_scalar_prefetch=1,
        grid=(db, npg // pg),
        in_specs=[pl.BlockSpec((1, hb, gw), lambda b, j, pt: (b, 0, 0))]
                 + _page_specs(pg, (1, ps, gw), page_of) + _page_specs(pg, (1, ps, gw), page_of),
        out_specs=pl.BlockSpec((1, hb, cfg.head_dim), lambda b, j, pt: (b, 0, 0)),
        scratch_shapes=[pltpu.VMEM((hb, 1), F32), pltpu.VMEM((hb, gw), F32)],
    )
    return pl.pallas_call(
        functools.partial(_sb_sample_kernel, pg=pg, cfg=cfg),
        out_shape=jax.ShapeDtypeStruct((db, hb, cfg.head_dim), BF16),
        grid_spec=grid_spec,
        compiler_params=_cparams(("arbitrary", "arbitrary")),
        name="sb_sample",
    )(page_table.reshape(-1), q_bd, *([cache_k] * pg), *([cache_v] * pg))


def _merge_kernel(x_ref, oa_ref, ob_ref, ga_ref, gb_ref, m_ref, gf_ref, wua_ref, wub_ref, wo_ref, wr_ref, br_ref,
                  x1_ref, h2_ref, rt_ref, *, cfg):
    ng, epg = cfg.n_groups, cfg.epg
    sig = lambda v: 1.0 / (1.0 + jnp.exp(-v))
    a = jnp.dot(oa_ref[...], wua_ref[...], preferred_element_type=F32)
    b = jnp.dot(ob_ref[...], wub_ref[...], preferred_element_type=F32)
    merged = sig(ga_ref[...]) * a + sig(gb_ref[...]) * b
    y = jnp.dot(merged.astype(BF16), wo_ref[...], preferred_element_type=F32)
    x1 = x_ref[0] + m_ref[0, 2] * y
    x1_ref[...] = x1
    h2 = x1 * lax.rsqrt(jnp.mean(x1 * x1, axis=-1, keepdims=True) + RMS_EPS) * gf_ref[...]
    h2 = h2 * (1.0 + m_ref[0, 4]) + m_ref[0, 3]
    h2_ref[...] = h2
    logits = jnp.dot(h2.astype(BF16), wr_ref[...], preferred_element_type=F32) + br_ref[...]

    lane = lax.broadcasted_iota(I32, logits.shape, 1).astype(F32)
    first = lambda hit: jnp.min(jnp.where(hit, lane, float(LANES)), axis=-1, keepdims=True)
    in_g = lane < ng
    gl = jnp.where(in_g, logits, -jnp.inf)
    gmax = jnp.max(gl, axis=-1, keepdims=True)
    gidx = first(gl == gmax)
    p_g = 1.0 / jnp.sum(jnp.where(in_g, jnp.exp(logits - gmax), 0.0), axis=-1, keepdims=True)
    lo = ng + gidx * epg
    el = jnp.where((lane >= lo) & (lane < lo + epg), logits, -jnp.inf)
    v1 = jnp.max(el, axis=-1, keepdims=True)
    i1 = first(el == v1)
    el2 = jnp.where(lane == i1, -jnp.inf, el)
    v2 = jnp.max(el2, axis=-1, keepdims=True)
    i2 = first(el2 == v2)
    e21 = jnp.exp(v2 - v1)
    w1 = 1.0 / (1.0 + e21)
    w2 = e21 / (1.0 + e21)
    rt = jnp.where(lane == 0, i1 - ng, jnp.where(lane == 1, i2 - ng,
         jnp.where(lane == 2, p_g * w1, jnp.where(lane == 3, p_g * w2, 0.0))))
    rt_ref[...] = rt


def _merge(x3, oa, ob, u, mod, g_ffn, wua, wub, wo, wr, br, cfg, tm):
    bx, t, d = x3.shape
    m = bx * t
    off, _ = _layout(cfg)
    aw, bw = oa.shape[1], ob.shape[1]
    r = mod.shape[2]
    nt = t // tm
    const = lambda shape: pl.BlockSpec(shape, lambda b, i: (0,) * len(shape), pipeline_mode=pl.Buffered(1))
    return pl.pallas_call(
        functools.partial(_merge_kernel, cfg=cfg),
        out_shape=(jax.ShapeDtypeStruct((m, d), F32), jax.ShapeDtypeStruct((m, d), F32),
                   jax.ShapeDtypeStruct((m, LANES), F32)),
        grid=(bx, nt),
        in_specs=[pl.BlockSpec((1, tm, d), lambda b, i: (b, i, 0)),
                  pl.BlockSpec((tm, aw), lambda b, i: (b * nt + i, 0)),
                  pl.BlockSpec((tm, bw), lambda b, i: (b * nt + i, 0)),
                  pl.BlockSpec((tm, d), lambda b, i: (b * nt + i, off["ga"][0] // d)),
                  pl.BlockSpec((tm, d), lambda b, i: (b * nt + i, off["gb"][0] // d)),
                  pl.BlockSpec((1, 6, r, d), lambda b, i: (b, 0, 0, 0)),
                  const((1, d)), const((aw, d)), const((bw, d)), const((d, d)), const((d, LANES)),
                  const((1, LANES))],
        out_specs=(pl.BlockSpec((tm, d), lambda b, i: (b * nt + i, 0)),
                   pl.BlockSpec((tm, d), lambda b, i: (b * nt + i, 0)),
                   pl.BlockSpec((tm, LANES), lambda b, i: (b * nt + i, 0))),
        compiler_params=_cparams(("arbitrary", "arbitrary")),
        name="merge_out_router",
    )(x3, oa, ob, u, u, mod, g_ffn.reshape(1, d), wua, wub, wo, wr, br)


def _moe_kernel(be_ref, nb_ref, tok_ref, tokn_ref, h_hbm, wg_ref, wu_ref, wd_ref, y_ref,
                xbuf, sem, wg_s, wu_s, wd_s, *, rows):
    i = pl.program_id(0)
    nb = nb_ref[0]

    def row_copy(tok, slot, r):
        return pltpu.make_async_copy(h_hbm.at[pl.ds(tok, 1)], xbuf.at[slot, pl.ds(r, 1)], sem.at[slot])

    def issue(idx_ref, slot):
        def body(r, c):
            row_copy(idx_ref[0, 0, r], slot, r).start()
            return c

        lax.fori_loop(0, rows, body, 0)

    @pl.when(i == 0)
    def _prime():
        issue(tok_ref, 0)

    @pl.when(i + 1 < nb)
    def _prefetch():
        issue(tokn_ref, (i + 1) % 2)

    @pl.when(i < nb)
    def _compute():
        slot = i % 2

        def wait_body(r, c):
            row_copy(0, slot, r).wait()
            return c

        lax.fori_loop(0, rows, wait_body, 0)

        @pl.when((i == 0) | (be_ref[i] != be_ref[jnp.maximum(i - 1, 0)]))
        def _cast():
            wg_s[...] = wg_ref[...].astype(BF16)
            wu_s[...] = wu_ref[...].astype(BF16)
            wd_s[...] = wd_ref[...].astype(BF16)

        x = xbuf[slot].astype(BF16)
        gate = jnp.dot(x, wg_s[...], preferred_element_type=F32)
        up = jnp.dot(x, wu_s[...], preferred_element_type=F32)
        act = (gate * (1.0 / (1.0 + jnp.exp(-gate))) * up).astype(BF16)
        y_ref[...] = jnp.dot(act, wd_s[...], preferred_element_type=F32)

    @pl.when(i >= nb)
    def _idle():
        y_ref[...] = jnp.zeros(y_ref.shape, F32)


def _moe_experts(h2_all, slot_token, block_expert, n_used, w_gate_e, w_up_e, w_down_e, rows):
    n_blocks = block_expert.shape[0]
    e, d, f = w_gate_e.shape
    tok3 = slot_token.reshape(n_blocks, 1, rows)
    grid_spec = pltpu.PrefetchScalarGridSpec(
        num_scalar_prefetch=2,
        grid=(n_blocks,),
        in_specs=[pl.BlockSpec((1, 1, rows), lambda i, be, nb: (i, 0, 0), memory_space=pltpu.SMEM),
                  pl.BlockSpec((1, 1, rows), lambda i, be, nb: (jnp.minimum(i + 1, n_blocks - 1), 0, 0),
                               memory_space=pltpu.SMEM),
                  pl.BlockSpec(memory_space=pl.ANY),
                  pl.BlockSpec((None, d, f), lambda i, be, nb: (be[i], 0, 0)),
                  pl.BlockSpec((None, d, f), lambda i, be, nb: (be[i], 0, 0)),
                  pl.BlockSpec((None, f, d), lambda i, be, nb: (be[i], 0, 0))],
        out_specs=pl.BlockSpec((rows, d), lambda i, be, nb: (i, 0)),
        scratch_shapes=[pltpu.VMEM((2, rows, d), F32), pltpu.SemaphoreType.DMA((2,)),
                        pltpu.VMEM((d, f), BF16), pltpu.VMEM((d, f), BF16), pltpu.VMEM((f, d), BF16)],
    )
    return pl.pallas_call(
        functools.partial(_moe_kernel, rows=rows),
        out_shape=jax.ShapeDtypeStruct((n_blocks * rows, d), F32),
        grid_spec=grid_spec,
        compiler_params=_cparams(("arbitrary",)),
        name="moe_experts",
    )(block_expert, n_used, tok3, tok3, h2_all, w_gate_e, w_up_e, w_down_e)


def _moe_plan(expert, n_experts, rows, top_k):
    a = expert.shape[0]
    n_blocks = -(-(a + n_experts * (rows - 1)) // rows)
    order = jnp.argsort(expert)
    e_sorted = expert[order]
    counts = jnp.zeros((n_experts,), I32).at[expert].add(1)
    padded = (counts + rows - 1) // rows * rows
    start = jnp.cumsum(counts) - counts
    pstart = jnp.cumsum(padded) - padded
    dest = (pstart[e_sorted] + jnp.arange(a, dtype=I32) - start[e_sorted]).astype(I32)
    slot_token = jnp.zeros((n_blocks * rows,), I32).at[dest].set((order // top_k).astype(I32))
    pos = jnp.zeros((a,), I32).at[order].set(dest)
    block_expert = jnp.minimum(
        jnp.searchsorted(jnp.cumsum(padded), jnp.arange(n_blocks, dtype=I32) * rows, side="right"),
        n_experts - 1).astype(I32)
    n_used = (jnp.sum(padded) // rows).astype(I32).reshape(1)
    return slot_token, pos, block_expert, n_used


def _combine_kernel(pos_ref, posn_ref, x1_ref, rt_ref, m_ref, gfin_ref, ys_hbm, o_ref, ybuf, sem, *, tm):
    t = pl.program_id(0)
    nt = pl.num_programs(0)

    def row_copy(p, slot, r):
        return pltpu.make_async_copy(ys_hbm.at[pl.ds(p, 1)], ybuf.at[slot, pl.ds(r, 1)], sem.at[slot])

    def issue(idx_ref, slot):
        def body(r, c):
            row_copy(idx_ref[0, 0, r], slot, r).start()
            return c

        lax.fori_loop(0, 2 * tm, body, 0)

    @pl.when(t == 0)
    def _prime():
        issue(pos_ref, 0)

    @pl.when(t + 1 < nt)
    def _prefetch():
        issue(posn_ref, (t + 1) % 2)

    slot = t % 2

    def wait_body(r, c):
        row_copy(0, slot, r).wait()
        return c

    lax.fori_loop(0, 2 * tm, wait_body, 0)

    rt = rt_ref[...]
    moe = rt[:, 2:3] * ybuf[slot, 0:tm, :] + rt[:, 3:4] * ybuf[slot, tm:2 * tm, :]
    x2 = x1_ref[...] + m_ref[0, 5] * moe
    y = x2 * lax.rsqrt(jnp.mean(x2 * x2, axis=-1, keepdims=True) + RMS_EPS)
    o_ref[...] = y * gfin_ref[...]


def _combine(x1, rt, mod, g_final, ys, pos, t_per_b, tm):
    m, d = x1.shape
    nt_all = m // tm
    nt_b = t_per_b // tm
    r = mod.shape[2]
    pos3 = pos.reshape(nt_all, tm, 2).transpose(0, 2, 1).reshape(nt_all, 1, 2 * tm)
    return pl.pallas_call(
        functools.partial(_combine_kernel, tm=tm),
        out_shape=jax.ShapeDtypeStruct((m, d), F32),
        grid=(nt_all,),
        in_specs=[pl.BlockSpec((1, 1, 2 * tm), lambda t: (t, 0, 0), memory_space=pltpu.SMEM),
                  pl.BlockSpec((1, 1, 2 * tm), lambda t: (jnp.minimum(t + 1, nt_all - 1), 0, 0),
                               memory_space=pltpu.SMEM),
                  pl.BlockSpec((tm, d), lambda t: (t, 0)),
                  pl.BlockSpec((tm, LANES), lambda t: (t, 0)),
                  pl.BlockSpec((1, 6, r, d), lambda t: (t // nt_b, 0, 0, 0)),
                  pl.BlockSpec((1, d), lambda t: (0, 0)),
                  pl.BlockSpec(memory_space=pl.ANY)],
        out_specs=pl.BlockSpec((tm, d), lambda t: (t, 0)),
        scratch_shapes=[pltpu.VMEM((2, 2 * tm, d), F32), pltpu.SemaphoreType.DMA((2,))],
        compiler_params=_cparams(("arbitrary",)),
        name="moe_combine",
    )(pos3, pos3, x1, rt, mod, g_final.reshape(1, d), ys)


def _forward(cfg, x_prompt, x_sample, cache_a_k, cache_a_v, cache_idx_k, cache_b_k, cache_b_v,
             page_table, c_prompt, c_sample, w_ada, b_ada, g_mix, g_ffn, g_final, w_in,
             w_up_a, w_up_b, w_out, w_group, b_group, w_router, b_router, w_gate_e, w_up_e, w_down_e):
    depth = w_in.shape[0]
    bp, seq, d = x_prompt.shape
    db, dseq, _ = x_sample.shape
    assert dseq == 1, "sample path is written for single-token decode"
    hd = cfg.head_dim
    off, wpad = _layout(cfg)
    ps = cfg.page_size
    ng, epg = cfg.n_groups, cfg.epg
    n_exp = ng * epg
    col = lambda a, name: a[..., off[name][0]:off[name][0] + off[name][1]]

    xp, xs = x_prompt, x_sample.reshape(1, db, d)
    c_all = jnp.concatenate([c_prompt, c_sample, jnp.zeros((-(bp + db) % 8, d), F32)], axis=0)
    rows_p, rows_s = [], []
    for layer in range(depth):
        mod = _adaln(c_all, w_ada[layer], b_ada[layer])
        mod_p = mod[:bp].reshape(bp, 6, 1, d)
        mod_s = mod[bp:bp + db].reshape(db, 6, d).transpose(1, 0, 2).reshape(1, 6, db, d)
        w_al = _permute_w_in(w_in[layer], cfg)
        wua, wub, wo = w_up_a[layer].astype(BF16), w_up_b[layer].astype(BF16), w_out[layer].astype(BF16)
        wr = jnp.concatenate([w_group[layer], w_router[layer].transpose(1, 0, 2).reshape(d, n_exp),
                              jnp.zeros((d, LANES - ng - n_exp), F32)], axis=1).astype(BF16)
        br = jnp.concatenate([b_group[layer], b_router[layer].reshape(-1),
                              jnp.zeros((LANES - ng - n_exp,), F32)]).reshape(1, LANES)

        ts_p = min(256, seq)
        h_p = _rms_mod(xp, g_mix[layer], mod_p, ts_p).reshape(bp * seq, d)
        u_p = _matmul(h_p, w_al, min(cfg.tm_in, bp * seq), cfg.tn_in)
        u_p3 = u_p.reshape(bp, seq, wpad)
        oa_p = _dsa_prompt(u_p3, cfg).reshape(bp * seq, -1)
        ob_p = _sb_prompt(u_p3, cfg).reshape(bp * seq, -1)

        h_s = _rms_mod(xs, g_mix[layer], mod_s, db).reshape(db, d)
        u_s = _matmul(h_s, w_al, db, cfg.tn_in)
        qi3 = col(u_s, "qi").reshape(db, cfg.n_idx_heads, cfg.idx_dim)
        kiwi_s = col(u_s, "kiwi")
        wi3 = (kiwi_s[:, cfg.idx_dim:cfg.idx_dim + cfg.n_idx_heads]
               * (float(cfg.n_idx_heads * cfg.idx_dim) ** -0.5)).reshape(db, cfg.n_idx_heads, 1)
        c_ik = cache_idx_k[layer]
        c_ak = cache_a_k[layer].reshape(cache_a_k.shape[1], ps, hd)
        c_av = cache_a_v[layer].reshape(cache_a_v.shape[1], ps, hd)
        c_bk = cache_b_k[layer].reshape(cache_b_k.shape[1], ps, cfg.n_kv_b * hd)
        c_bv = cache_b_v[layer].reshape(cache_b_v.shape[1], ps, cfg.n_kv_b * hd)
        sc4 = _dsa_sample_scores(qi3, wi3, c_ik, page_table, cfg)
        bias, bias_new = _dsa_sample_select(sc4.reshape(db, -1), u_s, cfg)
        qa3 = col(u_s, "qa").reshape(db, cfg.n_heads_a, hd)
        oa_s = _dsa_sample_attend(qa3, bias.reshape(sc4.shape), bias_new, u_s.reshape(db, 1, wpad), c_ak, c_av,
                                  page_table, cfg).reshape(db, -1)
        rep = cfg.n_heads_b // cfg.n_kv_b
        qb3 = col(u_s, "qb").reshape(db, cfg.n_heads_b, 1, hd)
        gsel = (jnp.arange(cfg.n_heads_b)[:, None] // rep == jnp.arange(cfg.n_kv_b)[None, :])
        q_bd = (qb3 * gsel[None, :, :, None].astype(F32)).reshape(db, cfg.n_heads_b, cfg.n_kv_b * hd)
        ob_s = _sb_sample(q_bd, c_bk, c_bv, page_table, cfg).reshape(db, -1)

        x1_p, h2_p, rt_p = _merge(xp, oa_p, ob_p, u_p, mod_p, g_ffn[layer], wua, wub, wo, wr, br, cfg,
                                  min(cfg.tm_merge, seq))
        x1_s, h2_s, rt_s = _merge(xs, oa_s, ob_s, u_s, mod_s, g_ffn[layer], wua, wub, wo, wr, br, cfg, db)

        h2_all = jnp.concatenate([h2_p, h2_s], axis=0)
        rt_all = jnp.concatenate([rt_p, rt_s], axis=0)
        expert = rt_all[:, 0:cfg.top_k].astype(I32).reshape(-1)
        slot_token, pos, block_expert, n_used = _moe_plan(expert, n_exp, cfg.moe_rows, cfg.top_k)
        ys = _moe_experts(h2_all, slot_token, block_expert, n_used,
                          w_gate_e[layer], w_up_e[layer], w_down_e[layer], cfg.moe_rows)
        last = layer == depth - 1
        gfin = g_final if last else jnp.ones_like(g_final)
        assert last, "deeper stacks need an un-normalised combine for inner layers"
        npos = bp * seq * cfg.top_k
        y_p = _combine(x1_p, rt_p, mod_p, gfin, ys, pos[:npos], seq, min(cfg.tm_comb, seq))
        y_s = _combine(x1_s, rt_s, mod_s, gfin, ys, pos[npos:], db, db)

        pages = lambda a: a.reshape((bp, seq // ps, ps) + a.shape[2:])
        rows_p.append((pages(col(u_p3, "ka").reshape(bp, seq, 1, hd)),
                       pages(col(u_p3, "va").reshape(bp, seq, 1, hd)),
                       pages(col(u_p3, "kiwi")[..., :cfg.idx_dim]),
                       pages(col(u_p3, "kb").reshape(bp, seq, cfg.n_kv_b, hd)),
                       pages(col(u_p3, "vb").reshape(bp, seq, cfg.n_kv_b, hd))))
        rows_s.append((col(u_s, "ka").reshape(db, 1, 1, hd), col(u_s, "va").reshape(db, 1, 1, hd),
                       kiwi_s[:, :cfg.idx_dim].reshape(db, 1, cfg.idx_dim),
                       col(u_s, "kb").reshape(db, 1, cfg.n_kv_b, hd),
                       col(u_s, "vb").reshape(db, 1, cfg.n_kv_b, hd)))
        xp, xs = y_p.reshape(bp, seq, d), y_s.reshape(1, db, d)

    stack = lambda rows: tuple(jnp.stack(r) for r in zip(*rows))
    return (xp, xs.reshape(db, 1, d)) + stack(rows_p) + stack(rows_s)


def kernel(x_prompt, x_sample, cache_a_k, cache_a_v, cache_idx_k, cache_b_k, cache_b_v, page_table,
           c_prompt, c_sample, w_ada, b_ada, g_mix, g_ffn, g_final, w_in, w_up_a, w_up_b, w_out,
           w_group, b_group, w_router, b_router, w_gate_e, w_up_e, w_down_e):
    return _forward(Cfg(), x_prompt, x_sample, cache_a_k, cache_a_v, cache_idx_k, cache_b_k, cache_b_v,
                    page_table, c_prompt, c_sample, w_ada, b_ada, g_mix, g_ffn, g_final, w_in,
                    w_up_a, w_up_b, w_out, w_group, b_group, w_router, b_router, w_gate_e, w_up_e, w_down_e)
```

```python
import functools
from typing import NamedTuple

import numpy as np
import jax
import jax.numpy as jnp
from jax import lax
from jax.experimental import pallas as pl
from jax.experimental.pallas import tpu as pltpu

F32 = jnp.float32
BF16 = jnp.bfloat16
I32 = jnp.int32

LANES = 128
INT_MIN = -(2 ** 31)
RMS_EPS = 1e-6
NEG_BIG = -1e30
SB_DEAD_TAIL = -110.0
VMEM_LIMIT = 48 * 1024 * 1024


class Cfg(NamedTuple):
    d_model: int = 2048
    head_dim: int = 128
    n_heads_a: int = 8
    n_idx_heads: int = 16
    idx_dim: int = 64
    n_heads_b: int = 8
    n_kv_b: int = 4
    topk_max: int = 256
    n_groups: int = 8
    epg: int = 8
    top_k: int = 2
    d_ff: int = 512
    page_size: int = 128
    tn_in: int = 512
    tm_in: int = 1024
    tq_a: int = 256
    tk_a: int = 256
    t_b: int = 256
    tm_merge: int = 256
    moe_rows: int = 128
    tm_comb: int = 128
    pages_per_step: int = 8
    sb_pages_per_chunk: int = 4


def _cparams(sem):
    return pltpu.CompilerParams(dimension_semantics=sem, vmem_limit_bytes=VMEM_LIMIT)


def _round_up(a, b):
    return -(-a // b) * b


def _layout(cfg):
    d, hd = cfg.d_model, cfg.head_dim
    segs = [("ga", d), ("gb", d), ("qa", cfg.n_heads_a * hd), ("qi", cfg.n_idx_heads * cfg.idx_dim),
            ("qb", cfg.n_heads_b * hd), ("kb", cfg.n_kv_b * hd), ("vb", cfg.n_kv_b * hd),
            ("ka", hd), ("va", hd), ("kiwi", LANES)]
    off, o = {}, 0
    for name, w in segs:
        off[name] = (o, w)
        o += w
    return off, _round_up(o, cfg.tn_in)


def _permute_w_in(w_in, cfg):
    hd = cfg.head_dim
    sizes = [cfg.n_heads_a * hd, hd, hd, cfg.n_idx_heads * cfg.idx_dim, cfg.idx_dim, cfg.n_idx_heads,
             cfg.n_heads_b * hd, cfg.n_kv_b * hd, cfg.n_kv_b * hd, cfg.d_model, cfg.d_model]
    names = ["qa", "ka", "va", "qi", "ki", "wi", "qb", "kb", "vb", "ga", "gb"]
    cuts = np.concatenate([[0], np.cumsum(sizes)])
    src = {n: w_in[:, int(cuts[i]):int(cuts[i + 1])] for i, n in enumerate(names)}
    off, width = _layout(cfg)
    assert cfg.idx_dim + cfg.n_idx_heads <= LANES
    kiwi = jnp.concatenate(
        [src["ki"], src["wi"], jnp.zeros((w_in.shape[0], LANES - cfg.idx_dim - cfg.n_idx_heads), w_in.dtype)], axis=1)
    parts = [src["ga"], src["gb"], src["qa"], src["qi"], src["qb"], src["kb"], src["vb"], src["ka"], src["va"], kiwi]
    used = sum(p.shape[1] for p in parts)
    if width > used:
        parts.append(jnp.zeros((w_in.shape[0], width - used), w_in.dtype))
    return jnp.concatenate(parts, axis=1)


def _split(x):
    hi = x.astype(BF16)
    return hi, (x - hi.astype(F32)).astype(BF16)


def _dot3(a, b, dot=None):
    dot = dot or (lambda x, y: jnp.dot(x, y, preferred_element_type=F32))
    (ah, al), (bh, bl) = _split(a), _split(b)
    return dot(ah, bh) + dot(al, bh) + dot(ah, bl)


def _adaln_kernel(c_ref, w_ref, b_ref, o_ref):
    c = c_ref[...]
    o_ref[...] = _dot3(c * (1.0 / (1.0 + jnp.exp(-c))), w_ref[...]) + b_ref[...]


def _adaln(c_all, w_ada, b_ada):
    mc, d = c_all.shape
    n = w_ada.shape[1]
    tn = 1024 if n % 1024 == 0 else n
    return pl.pallas_call(
        _adaln_kernel,
        out_shape=jax.ShapeDtypeStruct((mc, n), F32),
        grid=(n // tn,),
        in_specs=[pl.BlockSpec((mc, d), lambda j: (0, 0)),
                  pl.BlockSpec((d, tn), lambda j: (0, j)),
                  pl.BlockSpec((1, tn), lambda j: (0, j))],
        out_specs=pl.BlockSpec((mc, tn), lambda j: (0, j)),
        compiler_params=_cparams(("arbitrary",)),
        name="adaln",
    )(c_all, w_ada, b_ada.reshape(1, n))


def _rms_mod_kernel(x_ref, g_ref, m_ref, o_ref):
    x = x_ref[0]
    y = x * lax.rsqrt(jnp.mean(x * x, axis=-1, keepdims=True) + RMS_EPS) * g_ref[...]
    o_ref[0] = (y * (1.0 + m_ref[0, 1]) + m_ref[0, 0]).astype(o_ref.dtype)


def _rms_mod(x3, g, mod, ts, out_dtype=None):
    bx, t, d = x3.shape
    r = mod.shape[2]
    out_dtype = out_dtype or BF16
    return pl.pallas_call(
        _rms_mod_kernel,
        out_shape=jax.ShapeDtypeStruct((bx, t, d), out_dtype),
        grid=(bx, t // ts),
        in_specs=[pl.BlockSpec((1, ts, d), lambda b, i: (b, i, 0)),
                  pl.BlockSpec((1, d), lambda b, i: (0, 0)),
                  pl.BlockSpec((1, 6, r, d), lambda b, i: (b, 0, 0, 0))],
        out_specs=pl.BlockSpec((1, ts, d), lambda b, i: (b, i, 0)),
        compiler_params=_cparams(("arbitrary", "arbitrary")),
        name="rms_mod",
    )(x3, g.reshape(1, d), mod)


def _mm_kernel(a_ref, w_ref, o_ref, wbf_ref):
    @pl.when(pl.program_id(1) == 0)
    def _():
        wbf_ref[...] = w_ref[...].astype(BF16)

    o_ref[...] = jnp.dot(a_ref[...], wbf_ref[...], preferred_element_type=F32).astype(o_ref.dtype)


def _mm3_kernel(a_ref, w_ref, o_ref, wbf_ref):
    del wbf_ref
    o_ref[...] = _dot3(a_ref[...], w_ref[...]).astype(o_ref.dtype)


def _matmul(a, w, tm, tn, out_dtype=F32):
    m, k = a.shape
    n = w.shape[1]
    return pl.pallas_call(
        _mm_kernel if a.dtype == BF16 else _mm3_kernel,
        out_shape=jax.ShapeDtypeStruct((m, n), out_dtype),
        grid=(n // tn, m // tm),
        in_specs=[pl.BlockSpec((tm, k), lambda j, i: (i, 0)),
                  pl.BlockSpec((k, tn), lambda j, i: (0, j))],
        out_specs=pl.BlockSpec((tm, tn), lambda j, i: (i, j)),
        scratch_shapes=[pltpu.VMEM((k, tn), BF16)],
        compiler_params=_cparams(("arbitrary", "arbitrary")),
        name="in_proj",
    )(a, w)


def _ordered_bits_to_float(u):
    key = u ^ jnp.int32(INT_MIN)
    return pltpu.bitcast(jnp.where(key < 0, key ^ jnp.int32(0x7FFFFFFF), key), F32)


def _kth_largest(count_ge, rows, k):
    def body(bi, res):
        cand = res | lax.shift_left(jnp.int32(1), jnp.int32(31) - bi)
        return jnp.where(count_ge(_ordered_bits_to_float(cand)) >= k, cand, res)

    thr = _ordered_bits_to_float(lax.fori_loop(0, 32, body, jnp.zeros((rows, 1), I32)))
    return jnp.where(thr != thr, -jnp.inf, thr)


def _tie_cutoff(count_tie_lt, rows, need, nbits):
    def body(bi, x):
        cand = x | lax.shift_left(jnp.int32(1), jnp.int32(nbits - 1) - bi)
        return jnp.where(count_tie_lt(cand) < need, cand, x)

    return lax.fori_loop(0, nbits, body, jnp.zeros((rows, 1), I32))


def _fold_lanes(x):
    acc = x[:, 0:LANES]
    for j in range(1, x.shape[1] // LANES):
        acc = acc + x[:, j * LANES:(j + 1) * LANES]
    return acc


def _dot_nt(a, b):
    return lax.dot_general(a, b, (((1,), (1,)), ((), ())), preferred_element_type=F32)


def _dsa_prompt_kernel(qa_ref, qi_ref, kwq_ref, kw_ref, ka_ref, va_ref, o_ref,
                       ka_s, kb_s, kbf_s, vbf_s, keys_s, bias_s, xcut_s, qbf_s, m_s, l_s, acc_s,
                       *, cfg, tq, tk, n_sel):
    hd, idim, ni, ha = cfg.head_dim, cfg.idx_dim, cfg.n_idx_heads, cfg.n_heads_a
    s_total = keys_s.shape[1]
    i = pl.program_id(1)
    t0 = i * tq
    nk = (t0 + tq + tk - 1) // tk

    @pl.when(i == 0)
    def _stage():
        kw = kw_ref[0]
        lane = lax.broadcasted_iota(I32, kw.shape, 1)
        k_lo = jnp.where(lane < idim, kw, 0.0)
        ka_s[...] = k_lo.astype(BF16)
        kb_s[...] = pltpu.roll(k_lo, idim, 1).astype(BF16)
        kbf_s[...] = ka_ref[0].astype(BF16)
        vbf_s[...] = va_ref[0].astype(BF16)

    w_q = kwq_ref[0] * (float(ni * idim) ** -0.5)
    row = t0 + lax.broadcasted_iota(I32, (tq, tk), 0)
    col0 = lax.broadcasted_iota(I32, (tq, tk), 1)

    def score_chunk(c, carry):
        ks = pl.multiple_of(c * tk, tk)
        k_even = ka_s[pl.ds(ks, tk), :]
        k_odd = kb_s[pl.ds(ks, tk), :]
        acc = jnp.zeros((tq, tk), F32)
        for p in range(ni // 2):
            qp = qi_ref[0, :, p * LANES:(p + 1) * LANES].astype(BF16)
            for par, kk in ((0, k_even), (1, k_odd)):
                h = 2 * p + par
                acc = acc + w_q[:, idim + h:idim + h + 1] * jnp.maximum(_dot_nt(qp, kk), 0.0)
        keys_s[:, pl.ds(ks, tk)] = jnp.where(ks + col0 <= row, acc, -jnp.inf)
        return carry

    lax.fori_loop(0, nk, score_chunk, 0)

    def count(pred):
        def body(c, cnt):
            ks = pl.multiple_of(c * tk, tk)
            return cnt + _fold_lanes(pred(keys_s[:, pl.ds(ks, tk)], ks + col0).astype(I32))

        cnt = lax.fori_loop(0, nk, body, jnp.zeros((tq, LANES), I32))
        return jnp.sum(cnt, axis=-1, keepdims=True)

    thr = _kth_largest(lambda t: count(lambda s, c: s >= t), tq, n_sel)

    xcut_s[...] = jnp.full(xcut_s.shape, s_total, I32)

    @pl.when(jnp.max(count(lambda s, c: s >= thr)) > n_sel)
    def _ties():
        need = n_sel - count(lambda s, c: s > thr)
        x = _tie_cutoff(lambda cand: count(lambda s, c: (s == thr) & (c < cand)), tq, need,
                        max(1, (s_total - 1).bit_length()))
        xcut_s[...] = jnp.broadcast_to(x, xcut_s.shape)

    xcut = xcut_s[:, 0:1]

    def mask_chunk(c, carry):
        ks = pl.multiple_of(c * tk, tk)
        col = ks + col0
        key = keys_s[:, pl.ds(ks, tk)]
        sel = ((key > thr) | ((key == thr) & (col <= xcut))) & (col <= row)
        bias_s[:, pl.ds(ks, tk)] = jnp.where(sel, 0.0, NEG_BIG)
        return carry

    lax.fori_loop(0, nk, mask_chunk, 0)
    scale = float(hd) ** -0.5

    for h in range(ha):
        qbf_s[h] = qa_ref[0, :, h * hd:(h + 1) * hd].astype(BF16)
    m_s[...] = jnp.full(m_s.shape, NEG_BIG, F32)
    l_s[...] = jnp.zeros(l_s.shape, F32)
    acc_s[...] = jnp.zeros(acc_s.shape, F32)

    def att_chunk(c, carry):
        ks = pl.multiple_of(c * tk, tk)
        kc = kbf_s[pl.ds(ks, tk), :]
        vc = vbf_s[pl.ds(ks, tk), :]
        bias = bias_s[:, pl.ds(ks, tk)]
        dist = (row - (ks + col0)).astype(F32)
        for h in range(ha):
            slope = float(2.0 ** (-8.0 * (h + 1) / ha))
            lg = _dot_nt(qbf_s[h], kc) * scale - slope * dist + bias
            m = m_s[h]
            m_new = jnp.maximum(m, jnp.max(lg, axis=-1, keepdims=True))
            p = jnp.exp(lg - m_new)
            alpha = jnp.exp(m - m_new)
            l_s[h] = alpha * l_s[h] + jnp.sum(p, axis=-1, keepdims=True)
            acc_s[h] = alpha * acc_s[h] + jnp.dot(p.astype(BF16), vc, preferred_element_type=F32)
            m_s[h] = m_new
        return carry

    lax.fori_loop(0, nk, att_chunk, 0)
    for h in range(ha):
        o_ref[0, :, h * hd:(h + 1) * hd] = (acc_s[h] / l_s[h]).astype(o_ref.dtype)


def _dsa_prompt(u3, cfg):
    b, s, _ = u3.shape
    off, _ = _layout(cfg)
    tq, tk = min(cfg.tq_a, s), min(cfg.tk_a, s)
    n_sel = min(cfg.topk_max, s // 4)
    aw = cfg.n_heads_a * cfg.head_dim
    iw = cfg.n_idx_heads * cfg.idx_dim
    assert 2 * cfg.idx_dim == LANES and s % tk == 0 and s % tq == 0 and tk >= n_sel
    cb = lambda name, w: off[name][0] // w
    kern = functools.partial(_dsa_prompt_kernel, cfg=cfg, tq=tq, tk=tk, n_sel=n_sel)
    return pl.pallas_call(
        kern,
        out_shape=jax.ShapeDtypeStruct((b, s, aw), BF16),
        grid=(b, s // tq),
        in_specs=[pl.BlockSpec((1, tq, aw), lambda bi, i: (bi, i, cb("qa", aw))),
                  pl.BlockSpec((1, tq, iw), lambda bi, i: (bi, i, cb("qi", iw))),
                  pl.BlockSpec((1, tq, LANES), lambda bi, i: (bi, i, cb("kiwi", LANES))),
                  pl.BlockSpec((1, s, LANES), lambda bi, i: (bi, 0, cb("kiwi", LANES))),
                  pl.BlockSpec((1, s, LANES), lambda bi, i: (bi, 0, cb("ka", LANES))),
                  pl.BlockSpec((1, s, LANES), lambda bi, i: (bi, 0, cb("va", LANES)))],
        out_specs=pl.BlockSpec((1, tq, aw), lambda bi, i: (bi, i, 0)),
        scratch_shapes=[pltpu.VMEM((s, LANES), BF16), pltpu.VMEM((s, LANES), BF16),
                        pltpu.VMEM((s, LANES), BF16), pltpu.VMEM((s, LANES), BF16),
                        pltpu.VMEM((tq, s), F32), pltpu.VMEM((tq, s), F32), pltpu.VMEM((tq, LANES), I32),
                        pltpu.VMEM((cfg.n_heads_a, tq, cfg.head_dim), BF16),
                        pltpu.VMEM((cfg.n_heads_a, tq, 1), F32), pltpu.VMEM((cfg.n_heads_a, tq, 1), F32),
                        pltpu.VMEM((cfg.n_heads_a, tq, cfg.head_dim), F32)],
        compiler_params=_cparams(("arbitrary", "arbitrary")),
        name="dsa_prompt",
    )(u3, u3, u3, u3, u3, u3)


def _sb_terms(q, kc, scale):
    z = _dot_nt(q, kc) * scale
    lk = -(jnp.maximum(z, 0.0) + jnp.log(1.0 + jnp.exp(-jnp.abs(z))))
    return z, lk


def _suffix_in_chunk(lk, upper):
    hi = lk.astype(BF16)
    lo = (lk - hi.astype(F32)).astype(BF16)
    return (jnp.dot(hi, upper, preferred_element_type=F32) + jnp.dot(lo, upper, preferred_element_type=F32))


def _strict_lower_ones(n):
    return (lax.broadcasted_iota(I32, (n, n), 0) > lax.broadcasted_iota(I32, (n, n), 1)).astype(BF16)


def _sb_prompt_kernel(q_ref, k_ref, v_ref, o_ref, kbf_s, vbf_s, *, t, hd, rep):
    i = pl.program_id(2)

    @pl.when(i == 0)
    def _stage():
        kbf_s[...] = k_ref[0].astype(BF16)
        vbf_s[...] = v_ref[0].astype(BF16)

    qs = [q_ref[0, :, r * hd:(r + 1) * hd].astype(BF16) for r in range(rep)]
    scale = float(hd) ** -0.5
    upper = _strict_lower_ones(t)
    before = lax.broadcasted_iota(I32, (t, t), 1) < lax.broadcasted_iota(I32, (t, t), 0)

    ks = pl.multiple_of(i * t, t)
    kc, vc = kbf_s[pl.ds(ks, t), :], vbf_s[pl.ds(ks, t), :]
    state = []
    for q in qs:
        z, lk = _sb_terms(q, kc, scale)
        lkm = jnp.where(before, lk, 0.0)
        tail = _suffix_in_chunk(lkm, upper)
        w = jnp.where(before, jnp.exp(z + lk + tail), 0.0)
        state.append(jnp.sum(lkm, axis=-1, keepdims=True))
        state.append(jnp.dot(w.astype(BF16), vc, preferred_element_type=F32))

    def live(st):
        top = st[0]
        for r in range(1, rep):
            top = jnp.maximum(top, st[2 * r])
        return jnp.max(top) > SB_DEAD_TAIL

    def body(loop):
        j, _, st = loop
        ks = pl.multiple_of((i - 1 - j) * t, t)
        kc, vc = kbf_s[pl.ds(ks, t), :], vbf_s[pl.ds(ks, t), :]
        out = []
        for r, q in enumerate(qs):
            carry, acc = st[2 * r], st[2 * r + 1]
            z, lk = _sb_terms(q, kc, scale)
            tail = carry + _suffix_in_chunk(lk, upper)
            w = jnp.exp(z + lk + tail)
            out.append(carry + jnp.sum(lk, axis=-1, keepdims=True))
            out.append(acc + jnp.dot(w.astype(BF16), vc, preferred_element_type=F32))
        return j + 1, live(out), tuple(out)

    _, _, st = lax.while_loop(lambda loop: (loop[0] < i) & loop[1], body, (jnp.int32(0), live(state), tuple(state)))
    for r in range(rep):
        o_ref[0, :, r * hd:(r + 1) * hd] = st[2 * r + 1].astype(o_ref.dtype)


def _sb_prompt(u3, cfg):
    b, s, _ = u3.shape
    off, _ = _layout(cfg)
    hd, hb, g = cfg.head_dim, cfg.n_heads_b, cfg.n_kv_b
    rep = hb // g
    t = min(cfg.t_b, s)
    assert s % t == 0
    qw = rep * hd
    assert off["qb"][0] % qw == 0
    qb0, kb0, vb0 = off["qb"][0] // qw, off["kb"][0] // hd, off["vb"][0] // hd
    kern = functools.partial(_sb_prompt_kernel, t=t, hd=hd, rep=rep)
    return pl.pallas_call(
        kern,
        out_shape=jax.ShapeDtypeStruct((b, s, hb * hd), BF16),
        grid=(b, g, s // t),
        in_specs=[pl.BlockSpec((1, t, qw), lambda bi, gi, i: (bi, i, qb0 + gi)),
                  pl.BlockSpec((1, s, hd), lambda bi, gi, i: (bi, 0, kb0 + gi)),
                  pl.BlockSpec((1, s, hd), lambda bi, gi, i: (bi, 0, vb0 + gi))],
        out_specs=pl.BlockSpec((1, t, qw), lambda bi, gi, i: (bi, i, gi)),
        scratch_shapes=[pltpu.VMEM((s, hd), BF16), pltpu.VMEM((s, hd), BF16)],
        compiler_params=_cparams(("arbitrary", "arbitrary", "arbitrary")),
        name="sb_prompt",
    )(u3, u3, u3)


def _page_specs(n, block, page_of):
    zeros = (0,) * (len(block) - 1)
    return [pl.BlockSpec((None,) + tuple(block[1:]),
                         functools.partial(lambda b, j, pt, r: (page_of(b, j, r, pt),) + zeros, r=r))
            for r in range(n)]


def _dsa_sample_score_kernel(pt_ref, q_ref, w_ref, *rest, pg):
    k_refs, o_ref = rest[:pg], rest[pg]
    q = q_ref[0]
    w = w_ref[0]
    for r in range(pg):
        d = _dot3(q, k_refs[r][...], _dot_nt)
        o_ref[0, 0, r:r + 1, :] = jnp.sum(w * jnp.maximum(d, 0.0), axis=0, keepdims=True)


def _dsa_sample_scores(qi3, wi3, cache_ik, page_table, cfg):
    db, ni, idim = qi3.shape
    pool, ps, _ = cache_ik.shape
    npg = page_table.shape[1]
    pg = min(cfg.pages_per_step, npg)
    assert npg % pg == 0 and ps == LANES
    page_of = lambda b, j, r, pt: pt[b * npg + j * pg + r]
    grid_spec = pltpu.PrefetchScalarGridSpec(
        num_scalar_prefetch=1,
        grid=(db, npg // pg),
        in_specs=[pl.BlockSpec((1, ni, idim), lambda b, j, pt: (b, 0, 0)),
                  pl.BlockSpec((1, ni, 1), lambda b, j, pt: (b, 0, 0))]
                 + _page_specs(pg, (1, ps, idim), page_of),
        out_specs=pl.BlockSpec((1, 1, pg, ps), lambda b, j, pt: (b, j, 0, 0)),
    )
    out = pl.pallas_call(
        functools.partial(_dsa_sample_score_kernel, pg=pg),
        out_shape=jax.ShapeDtypeStruct((db, npg // pg, pg, ps), F32),
        grid_spec=grid_spec,
        compiler_params=_cparams(("arbitrary", "arbitrary")),
        name="dsa_sample_scores",
    )(page_table.reshape(-1), qi3, wi3, *([cache_ik] * pg))
    return out


def _dsa_sample_select_kernel(sc_ref, qi_ref, kw_ref, bias_ref, bnew_ref, *, cfg, n_sel, chunk):
    idim, ni = cfg.idx_dim, cfg.n_idx_heads
    db, p = sc_ref.shape
    nchunk = p // chunk
    kw = kw_ref[...]
    ki = kw[:, 0:idim]
    s_new = jnp.zeros((db, 1), F32)
    for h in range(ni):
        d = jnp.sum(qi_ref[:, h * idim:(h + 1) * idim] * ki, axis=-1, keepdims=True)
        s_new = s_new + (kw[:, idim + h:idim + h + 1] * (float(ni * idim) ** -0.5)) * jnp.maximum(d, 0.0)
    kn1 = s_new
    idx0 = lax.broadcasted_iota(I32, (db, chunk), 1)

    def count(pred, new_hit):
        cnt = jnp.zeros((db, LANES), I32)
        for c in range(nchunk):
            cnt = cnt + _fold_lanes(pred(sc_ref[:, c * chunk:(c + 1) * chunk], c * chunk + idx0).astype(I32))
        return jnp.sum(cnt, axis=-1, keepdims=True) + new_hit.astype(I32)

    thr = _kth_largest(lambda t: count(lambda k, i: k >= t, kn1 >= t), db, n_sel)
    need = n_sel - count(lambda k, i: k > thr, kn1 > thr)
    xcut = _tie_cutoff(lambda x: count(lambda k, i: (k == thr) & (i < x), (kn1 == thr) & (p < x)),
                       db, need, max(1, p.bit_length()))
    for c in range(nchunk):
        k = sc_ref[:, c * chunk:(c + 1) * chunk]
        sel = (k > thr) | ((k == thr) & (c * chunk + idx0 <= xcut))
        bias_ref[:, c * chunk:(c + 1) * chunk] = jnp.where(sel, 0.0, NEG_BIG)
    sel_new = (kn1 > thr) | ((kn1 == thr) & (p <= xcut))
    bnew_ref[...] = jnp.broadcast_to(jnp.where(sel_new, 0.0, NEG_BIG), (db, LANES))


def _dsa_sample_select(scores, u_s, cfg):
    db, p = scores.shape
    off, _ = _layout(cfg)
    iw = cfg.n_idx_heads * cfg.idx_dim
    n_sel = min(cfg.topk_max, (p + 1) // 4)
    chunk = min(2048, p)
    assert p % chunk == 0 and p + 1 >= n_sel
    kern = functools.partial(_dsa_sample_select_kernel, cfg=cfg, n_sel=n_sel, chunk=chunk)
    return pl.pallas_call(
        kern,
        out_shape=(jax.ShapeDtypeStruct((db, p), F32), jax.ShapeDtypeStruct((db, LANES), F32)),
        grid=(1,),
        in_specs=[pl.BlockSpec((db, p), lambda i: (0, 0)),
                  pl.BlockSpec((db, iw), lambda i: (0, off["qi"][0] // iw)),
                  pl.BlockSpec((db, LANES), lambda i: (0, off["kiwi"][0] // LANES))],
        out_specs=(pl.BlockSpec((db, p), lambda i: (0, 0)), pl.BlockSpec((db, LANES), lambda i: (0, 0))),
        compiler_params=_cparams(("arbitrary",)),
        name="dsa_sample_select",
    )(scores, u_s, u_s)


def _dsa_sample_attend_kernel(pt_ref, q_ref, bias_ref, bnew_ref, knew_ref, vnew_ref, *rest,
                              pg, cfg, past):
    k_refs, v_refs = rest[:pg], rest[pg:2 * pg]
    o_ref, m_s, l_s, acc_s = rest[2 * pg:]
    hd, ha = cfg.head_dim, cfg.n_heads_a
    j = pl.program_id(1)
    scale = float(hd) ** -0.5

    @pl.when(j == 0)
    def _init():
        m_s[...] = jnp.full(m_s.shape, NEG_BIG, F32)
        l_s[...] = jnp.zeros(l_s.shape, F32)
        acc_s[...] = jnp.zeros(acc_s.shape, F32)

    q = q_ref[0]
    hidx = lax.broadcasted_iota(I32, (ha, 1), 0)
    slopes = jnp.zeros((ha, 1), F32)
    for h in range(ha):
        slopes = jnp.where(hidx == h, float(2.0 ** (-8.0 * (h + 1) / ha)), slopes)

    def update(lg, bias, vs):
        sel = jnp.broadcast_to(bias == 0.0, lg.shape)
        lg = lg + bias
        m = m_s[...]
        m_new = jnp.maximum(m, jnp.max(lg, axis=-1, keepdims=True))
        p = jnp.where(sel, jnp.exp(lg - m_new), 0.0)
        alpha = jnp.exp(m - m_new)
        l_s[...] = alpha * l_s[...] + jnp.sum(p, axis=-1, keepdims=True)
        w = vs[0].shape[0]
        pv = _dot3(p[:, 0:w], vs[0])
        for r in range(1, len(vs)):
            pv = pv + _dot3(p[:, r * w:(r + 1) * w], vs[r])
        acc_s[...] = alpha * acc_s[...] + pv
        m_s[...] = m_new

    n = pg * LANES
    pos = j * n + lax.broadcasted_iota(I32, (1, n), 1)
    lg = jnp.concatenate([_dot3(q, k_refs[r][...], _dot_nt) for r in range(pg)], axis=1)
    lg = lg * scale - slopes * (past - pos).astype(F32)
    update(lg, bias_ref[0, 0], [v_refs[r][...] for r in range(pg)])

    @pl.when(j == pl.num_programs(1) - 1)
    def _fin():
        kn = jnp.broadcast_to(knew_ref[0], (16, hd))
        vn = jnp.broadcast_to(vnew_ref[0], (16, hd))
        lg = _dot3(q, kn, _dot_nt) * scale
        first = lax.broadcasted_iota(I32, (1, 16), 1) == 0
        update(lg, jnp.where(first, bnew_ref[0][:, 0:16], NEG_BIG), [vn])
        o_ref[0] = (acc_s[...] / l_s[...]).astype(o_ref.dtype)


def _dsa_sample_attend(qa3, bias, bias_new, u_s3, cache_k, cache_v, page_table, cfg):
    db, ha, hd = qa3.shape
    npg = page_table.shape[1]
    pg = min(cfg.pages_per_step, npg)
    ps = cache_k.shape[1]
    bias4 = bias.reshape(db, npg // pg, 1, pg * ps)
    off, _ = _layout(cfg)
    page_of = lambda b, j, r, pt: pt[b * npg + j * pg + r]
    kern = functools.partial(_dsa_sample_attend_kernel, pg=pg, cfg=cfg, past=npg * ps)
    grid_spec = pltpu.PrefetchScalarGridSpec(
        num_scalar_prefetch=1,
        grid=(db, npg // pg),
        in_specs=[pl.BlockSpec((1, ha, hd), lambda b, j, pt: (b, 0, 0)),
                  pl.BlockSpec((1, 1, 1, pg * ps), lambda b, j, pt: (b, j, 0, 0)),
                  pl.BlockSpec((1, 1, LANES), lambda b, j, pt: (b, 0, 0)),
                  pl.BlockSpec((1, 1, hd), lambda b, j, pt: (b, 0, off["ka"][0] // hd)),
                  pl.BlockSpec((1, 1, hd), lambda b, j, pt: (b, 0, off["va"][0] // hd))]
                 + _page_specs(pg, (1, ps, hd), page_of) + _page_specs(pg, (1, ps, hd), page_of),
        out_specs=pl.BlockSpec((1, ha, hd), lambda b, j, pt: (b, 0, 0)),
        scratch_shapes=[pltpu.VMEM((ha, 1), F32), pltpu.VMEM((ha, 1), F32), pltpu.VMEM((ha, hd), F32)],
    )
    return pl.pallas_call(
        kern,
        out_shape=jax.ShapeDtypeStruct((db, ha, hd), F32),
        grid_spec=grid_spec,
        compiler_params=_cparams(("arbitrary", "arbitrary")),
        name="dsa_sample_attend",
    )(page_table.reshape(-1), qa3, bias4, bias_new.reshape(db, 1, LANES),
      u_s3, u_s3, *([cache_k] * pg), *([cache_v] * pg))


def _sb_sample_kernel(pt_ref, q_ref, k_hbm, v_hbm, o_ref, kbuf, vbuf, sem, *, pg, npg, cfg):
    hd, hb, g = cfg.head_dim, cfg.n_heads_b, cfg.n_kv_b
    rep = hb // g
    b = pl.program_id(0)
    nchunks = npg // pg
    scale = float(hd) ** -0.5

    def page_copies(c, slot):
        out = []
        for r in range(pg):
            page = pt_ref[b * npg + (npg - 1 - (c * pg + r))]
            out.append(pltpu.make_async_copy(k_hbm.at[page], kbuf.at[slot, r], sem.at[0, slot]))
            out.append(pltpu.make_async_copy(v_hbm.at[page], vbuf.at[slot, r], sem.at[1, slot]))
        return out

    q = q_ref[0]
    qg = [q[:, gi * hd:(gi + 1) * hd] for gi in range(g)]
    upper = _strict_lower_ones(LANES)

    for cp in page_copies(0, 0):
        cp.start()

    def body(loop):
        c, _, carry, accs = loop
        slot = c % 2

        @pl.when(c + 1 < nchunks)
        def _prefetch():
            for cp in page_copies(c + 1, 1 - slot):
                cp.start()

        for cp in page_copies(c, slot):
            cp.wait()
        zs = []
        for r in range(pg):
            z = _dot3(qg[0], kbuf[slot, r, :, 0, :], _dot_nt)
            for gi in range(1, g):
                z = z + _dot3(qg[gi], kbuf[slot, r, :, gi, :], _dot_nt)
            zs.append(z * scale)
        z_all = jnp.concatenate(zs, axis=0)
        lk_all = -(jnp.maximum(z_all, 0.0) + jnp.log(1.0 + jnp.exp(-jnp.abs(z_all))))
        tail_in = _suffix_in_chunk(lk_all, upper)
        tot = jnp.sum(lk_all, axis=-1, keepdims=True)
        accs = list(accs)
        for r in range(pg):
            sl = slice(r * hb, (r + 1) * hb)
            w = jnp.exp(z_all[sl] + lk_all[sl] + (carry + tail_in[sl]))
            for gi in range(g):
                accs[gi] = accs[gi] + _dot3(w, vbuf[slot, r, :, gi, :])
            carry = carry + tot[sl]
        return c + 1, jnp.max(carry) > SB_DEAD_TAIL, carry, tuple(accs)

    init = (jnp.int32(0), jnp.bool_(True), jnp.zeros((hb, 1), F32), tuple(jnp.zeros((hb, hd), F32) for _ in range(g)))
    c_end, _, _, accs = lax.while_loop(lambda loop: (loop[0] < nchunks) & loop[1], body, init)

    @pl.when(c_end < nchunks)
    def _drain():
        for cp in page_copies(c_end, c_end % 2):
            cp.wait()

    head_group = lax.broadcasted_iota(I32, (hb, hd), 0) // rep
    out = jnp.zeros((hb, hd), F32)
    for gi in range(g):
        out = out + jnp.where(head_group == gi, accs[gi], 0.0)
    o_ref[0] = out.astype(o_ref.dtype)


def _sb_sample(q_bd, cache_k, cache_v, page_table, cfg):
    db, hb, gw = q_bd.shape
    npg = page_table.shape[1]
    _, ps, g, hd = cache_k.shape
    pg = min(cfg.sb_pages_per_chunk, npg)
    assert npg % pg == 0 and ps == LANES and g * hd == gw
    grid_spec = pltpu.PrefetchScalarGridSpec(
        num_scalar_prefetch=1,
        grid=(db,),
        in_specs=[pl.BlockSpec((1, hb, gw), lambda b, pt: (b, 0, 0)),
                  pl.BlockSpec(memory_space=pl.ANY), pl.BlockSpec(memory_space=pl.ANY)],
        out_specs=pl.BlockSpec((1, hb, cfg.head_dim), lambda b, pt: (b, 0, 0)),
        scratch_shapes=[pltpu.VMEM((2, pg, ps, g, hd), F32), pltpu.VMEM((2, pg, ps, g, hd), F32),
                        pltpu.SemaphoreType.DMA((2, 2))],
    )
    return pl.pallas_call(
        functools.partial(_sb_sample_kernel, pg=pg, npg=npg, cfg=cfg),
        out_shape=jax.ShapeDtypeStruct((db, hb, cfg.head_dim), F32),
        grid_spec=grid_spec,
        compiler_params=_cparams(("arbitrary",)),
        name="sb_sample",
    )(page_table.reshape(-1), q_bd, cache_k, cache_v)


def _merge_kernel(x_ref, oa_ref, ob_ref, ga_ref, gb_ref, m_ref, gf_ref, wua_ref, wub_ref, wo_ref, wr_ref, br_ref,
                  x1_ref, h2_ref, rt_ref, *, cfg):
    ng, epg = cfg.n_groups, cfg.epg
    sig = lambda v: 1.0 / (1.0 + jnp.exp(-v))

    def mm(a, w_ref):
        dot = lambda p, q: jnp.dot(p, q, preferred_element_type=F32)
        if w_ref.shape[0] == 1:
            return dot(a.astype(BF16), w_ref[0])
        a_hi, a_lo = _split(a)
        return dot(a_hi, w_ref[0]) + dot(a_lo, w_ref[0]) + dot(a_hi, w_ref[1])

    merged = sig(ga_ref[...]) * mm(oa_ref[...], wua_ref) + sig(gb_ref[...]) * mm(ob_ref[...], wub_ref)
    x1 = x_ref[0] + m_ref[0, 2] * mm(merged, wo_ref)
    x1_ref[...] = x1
    h2 = x1 * lax.rsqrt(jnp.mean(x1 * x1, axis=-1, keepdims=True) + RMS_EPS) * gf_ref[...]
    h2 = h2 * (1.0 + m_ref[0, 4]) + m_ref[0, 3]
    h2_ref[...] = h2
    logits = mm(h2, wr_ref) + br_ref[...]

    lane = lax.broadcasted_iota(I32, logits.shape, 1).astype(F32)
    first = lambda hit: jnp.min(jnp.where(hit, lane, float(LANES)), axis=-1, keepdims=True)
    in_g = lane < ng
    gl = jnp.where(in_g, logits, -jnp.inf)
    gmax = jnp.max(gl, axis=-1, keepdims=True)
    gidx = first(gl == gmax)
    p_g = 1.0 / jnp.sum(jnp.where(in_g, jnp.exp(logits - gmax), 0.0), axis=-1, keepdims=True)
    lo = ng + gidx * epg
    el = jnp.where((lane >= lo) & (lane < lo + epg), logits, -jnp.inf)
    v1 = jnp.max(el, axis=-1, keepdims=True)
    i1 = first(el == v1)
    el2 = jnp.where(lane == i1, -jnp.inf, el)
    v2 = jnp.max(el2, axis=-1, keepdims=True)
    i2 = first(el2 == v2)
    e21 = jnp.exp(v2 - v1)
    w1 = 1.0 / (1.0 + e21)
    w2 = e21 / (1.0 + e21)
    rt = jnp.where(lane == 0, i1 - ng, jnp.where(lane == 1, i2 - ng,
         jnp.where(lane == 2, p_g * w1, jnp.where(lane == 3, p_g * w2, 0.0))))
    rt_ref[...] = rt


def _merge(x3, oa, ob, u, mod, g_ffn, wua, wub, wo, wr, br, cfg, tm):
    bx, t, d = x3.shape
    m = bx * t
    off, _ = _layout(cfg)
    aw, bw = oa.shape[1], ob.shape[1]
    r = mod.shape[2]
    nt = t // tm
    const = lambda shape: pl.BlockSpec(shape, lambda b, i: (0,) * len(shape), pipeline_mode=pl.Buffered(1))
    return pl.pallas_call(
        functools.partial(_merge_kernel, cfg=cfg),
        out_shape=(jax.ShapeDtypeStruct((m, d), F32), jax.ShapeDtypeStruct((m, d), F32),
                   jax.ShapeDtypeStruct((m, LANES), F32)),
        grid=(bx, nt),
        in_specs=[pl.BlockSpec((1, tm, d), lambda b, i: (b, i, 0)),
                  pl.BlockSpec((tm, aw), lambda b, i: (b * nt + i, 0)),
                  pl.BlockSpec((tm, bw), lambda b, i: (b * nt + i, 0)),
                  pl.BlockSpec((tm, d), lambda b, i: (b * nt + i, off["ga"][0] // d)),
                  pl.BlockSpec((tm, d), lambda b, i: (b * nt + i, off["gb"][0] // d)),
                  pl.BlockSpec((1, 6, r, d), lambda b, i: (b, 0, 0, 0)),
                  const((1, d)), const(wua.shape), const(wub.shape), const(wo.shape), const((2, d, LANES)),
                  const((1, LANES))],
        out_specs=(pl.BlockSpec((tm, d), lambda b, i: (b * nt + i, 0)),
                   pl.BlockSpec((tm, d), lambda b, i: (b * nt + i, 0)),
                   pl.BlockSpec((tm, LANES), lambda b, i: (b * nt + i, 0))),
        compiler_params=_cparams(("arbitrary", "arbitrary")),
        name="merge_out_router",
    )(x3, oa, ob, u, u, mod, g_ffn.reshape(1, d), wua, wub, wo, wr, br)


def _moe_kernel(be_ref, nb_ref, tok_ref, tokn_ref, h_hbm, wg_ref, wu_ref, wd_ref, y_ref,
                xbuf, sem, wg_s, wu_s, wd_s, *, rows):
    i = pl.program_id(0)
    nb = nb_ref[0]

    def row_copy(tok, slot, r):
        return pltpu.make_async_copy(h_hbm.at[pl.ds(tok, 1)], xbuf.at[slot, pl.ds(r, 1)], sem.at[slot])

    def issue(idx_ref, slot):
        def body(r, c):
            row_copy(idx_ref[0, 0, r], slot, r).start()
            return c

        lax.fori_loop(0, rows, body, 0, unroll=8)

    @pl.when(i == 0)
    def _prime():
        issue(tok_ref, 0)

    @pl.when(i + 1 < nb)
    def _prefetch():
        issue(tokn_ref, (i + 1) % 2)

    @pl.when(i < nb)
    def _compute():
        slot = i % 2
        pltpu.make_async_copy(h_hbm.at[pl.ds(0, rows)], xbuf.at[slot], sem.at[slot]).wait()

        @pl.when((i == 0) | (be_ref[i] != be_ref[jnp.maximum(i - 1, 0)]))
        def _cast():
            wg_s[...] = wg_ref[...].astype(BF16)
            wu_s[...] = wu_ref[...].astype(BF16)
            wd_s[...] = wd_ref[...].astype(BF16)

        x = xbuf[slot].astype(BF16)
        gate = jnp.dot(x, wg_s[...], preferred_element_type=F32)
        up = jnp.dot(x, wu_s[...], preferred_element_type=F32)
        act = (gate * (1.0 / (1.0 + jnp.exp(-gate))) * up).astype(BF16)
        y_ref[...] = jnp.dot(act, wd_s[...], preferred_element_type=F32)

    @pl.when(i >= nb)
    def _idle():
        y_ref[...] = jnp.zeros(y_ref.shape, F32)


def _moe_experts(h2_all, slot_token, block_expert, n_used, w_gate_e, w_up_e, w_down_e, rows):
    n_blocks = block_expert.shape[0]
    e, d, f = w_gate_e.shape
    tok3 = slot_token.reshape(n_blocks, 1, rows)
    grid_spec = pltpu.PrefetchScalarGridSpec(
        num_scalar_prefetch=2,
        grid=(n_blocks,),
        in_specs=[pl.BlockSpec((1, 1, rows), lambda i, be, nb: (i, 0, 0), memory_space=pltpu.SMEM),
                  pl.BlockSpec((1, 1, rows), lambda i, be, nb: (jnp.minimum(i + 1, n_blocks - 1), 0, 0),
                               memory_space=pltpu.SMEM),
                  pl.BlockSpec(memory_space=pl.ANY),
                  pl.BlockSpec((None, d, f), lambda i, be, nb: (be[i], 0, 0)),
                  pl.BlockSpec((None, d, f), lambda i, be, nb: (be[i], 0, 0)),
                  pl.BlockSpec((None, f, d), lambda i, be, nb: (be[i], 0, 0))],
        out_specs=pl.BlockSpec((rows, d), lambda i, be, nb: (i, 0)),
        scratch_shapes=[pltpu.VMEM((2, rows, d), F32), pltpu.SemaphoreType.DMA((2,)),
                        pltpu.VMEM((d, f), BF16), pltpu.VMEM((d, f), BF16), pltpu.VMEM((f, d), BF16)],
    )
    return pl.pallas_call(
        functools.partial(_moe_kernel, rows=rows),
        out_shape=jax.ShapeDtypeStruct((n_blocks * rows, d), F32),
        grid_spec=grid_spec,
        compiler_params=_cparams(("arbitrary",)),
        name="moe_experts",
    )(block_expert, n_used, tok3, tok3, h2_all, w_gate_e, w_up_e, w_down_e)


def _rank_kernel(rt_ref, base_ref, dest_ref, tot_ref, base_s, *, tm):
    @pl.when(pl.program_id(0) == 0)
    def _init():
        base_s[...] = base_ref[...]

    rt = rt_ref[...]
    lane = lax.broadcasted_iota(I32, rt.shape, 1).astype(F32)
    o1 = (lane == rt[:, 0:1]).astype(F32)
    o2 = (lane == rt[:, 1:2]).astype(F32)
    lower = _strict_lower_ones(tm)
    p1 = jnp.dot(lower, o1.astype(BF16), preferred_element_type=F32)
    p2 = jnp.dot(lower, o2.astype(BF16), preferred_element_type=F32)
    c1 = jnp.sum(o1, axis=0, keepdims=True)
    c2 = jnp.sum(o2, axis=0, keepdims=True)
    base = base_s[...]
    d1 = jnp.sum((base + p1) * o1, axis=-1, keepdims=True)
    d2 = jnp.sum((base + c1 + p2) * o2, axis=-1, keepdims=True)
    dest_ref[...] = jnp.where(lane == 0, d1, jnp.where(lane == 1, d2, 0.0))
    base_s[...] = base + c1 + c2
    tot_ref[...] = base + c1 + c2


def _rank(rt_pad, base, tm):
    n = rt_pad.shape[0]
    return pl.pallas_call(
        functools.partial(_rank_kernel, tm=tm),
        out_shape=(jax.ShapeDtypeStruct((n, LANES), F32), jax.ShapeDtypeStruct((1, LANES), F32)),
        grid=(n // tm,),
        in_specs=[pl.BlockSpec((tm, LANES), lambda i: (i, 0)), pl.BlockSpec((1, LANES), lambda i: (0, 0))],
        out_specs=(pl.BlockSpec((tm, LANES), lambda i: (i, 0)), pl.BlockSpec((1, LANES), lambda i: (0, 0))),
        scratch_shapes=[pltpu.VMEM((1, LANES), F32)],
        compiler_params=_cparams(("arbitrary",)),
        name="moe_rank",
    )(rt_pad, base)


def _moe_plan(rt_all, n_experts, rows, top_k, tm):
    n_tok = rt_all.shape[0]
    a = n_tok * top_k
    assert top_k == 2 and n_experts <= LANES
    n_blocks = -(-(a + n_experts * (rows - 1)) // rows)
    n_pad = _round_up(n_tok, tm)
    rt_pad = jnp.concatenate([rt_all, jnp.full((n_pad - n_tok, LANES), -1.0, F32)], axis=0)
    _, tot = _rank(rt_pad, jnp.zeros((1, LANES), F32), tm)
    counts = tot[0, :n_experts].astype(I32)
    padded = (counts + rows - 1) // rows * rows
    cum = jnp.cumsum(padded)
    pstart = jnp.zeros((1, LANES), F32).at[0, :n_experts].set((cum - padded).astype(F32))
    dest, _ = _rank(rt_pad, pstart, tm)
    pos = dest[:n_tok, 0:top_k].astype(I32).reshape(-1)
    slot_token = jnp.zeros((n_blocks * rows,), I32).at[pos].set(jnp.arange(a, dtype=I32) // top_k)
    starts = jnp.arange(n_blocks, dtype=I32) * rows
    block_expert = jnp.minimum(jnp.sum((cum[None, :] <= starts[:, None]).astype(I32), axis=1), n_experts - 1)
    n_used = (cum[-1] // rows).astype(I32).reshape(1)
    return slot_token, pos, block_expert, n_used


def _combine_kernel(pos_ref, posn_ref, x1_ref, rt_ref, m_ref, gfin_ref, ys_hbm, o_ref, ybuf, sem, *, tm):
    t = pl.program_id(0)
    nt = pl.num_programs(0)

    def row_copy(p, slot, r):
        return pltpu.make_async_copy(ys_hbm.at[pl.ds(p, 1)], ybuf.at[slot, pl.ds(r, 1)], sem.at[slot])

    def issue(idx_ref, slot):
        def body(r, c):
            row_copy(idx_ref[0, 0, r], slot, r).start()
            return c

        lax.fori_loop(0, 2 * tm, body, 0, unroll=8)

    @pl.when(t == 0)
    def _prime():
        issue(pos_ref, 0)

    @pl.when(t + 1 < nt)
    def _prefetch():
        issue(posn_ref, (t + 1) % 2)

    slot = t % 2
    pltpu.make_async_copy(ys_hbm.at[pl.ds(0, 2 * tm)], ybuf.at[slot], sem.at[slot]).wait()

    rt = rt_ref[...]
    moe = rt[:, 2:3] * ybuf[slot, 0:tm, :] + rt[:, 3:4] * ybuf[slot, tm:2 * tm, :]
    x2 = x1_ref[...] + m_ref[0, 5] * moe
    y = x2 * lax.rsqrt(jnp.mean(x2 * x2, axis=-1, keepdims=True) + RMS_EPS)
    o_ref[...] = y * gfin_ref[...]


def _combine(x1, rt, mod, g_final, ys, pos, t_per_b, tm):
    m, d = x1.shape
    nt_all = m // tm
    nt_b = t_per_b // tm
    r = mod.shape[2]
    pos3 = pos.reshape(nt_all, tm, 2).transpose(0, 2, 1).reshape(nt_all, 1, 2 * tm)
    return pl.pallas_call(
        functools.partial(_combine_kernel, tm=tm),
        out_shape=jax.ShapeDtypeStruct((m, d), F32),
        grid=(nt_all,),
        in_specs=[pl.BlockSpec((1, 1, 2 * tm), lambda t: (t, 0, 0), memory_space=pltpu.SMEM),
                  pl.BlockSpec((1, 1, 2 * tm), lambda t: (jnp.minimum(t + 1, nt_all - 1), 0, 0),
                               memory_space=pltpu.SMEM),
                  pl.BlockSpec((tm, d), lambda t: (t, 0)),
                  pl.BlockSpec((tm, LANES), lambda t: (t, 0)),
                  pl.BlockSpec((1, 6, r, d), lambda t: (t // nt_b, 0, 0, 0)),
                  pl.BlockSpec((1, d), lambda t: (0, 0)),
                  pl.BlockSpec(memory_space=pl.ANY)],
        out_specs=pl.BlockSpec((tm, d), lambda t: (t, 0)),
        scratch_shapes=[pltpu.VMEM((2, 2 * tm, d), F32), pltpu.SemaphoreType.DMA((2,))],
        compiler_params=_cparams(("arbitrary",)),
        name="moe_combine",
    )(pos3, pos3, x1, rt, mod, g_final.reshape(1, d), ys)


def _forward(cfg, x_prompt, x_sample, cache_a_k, cache_a_v, cache_idx_k, cache_b_k, cache_b_v,
             page_table, c_prompt, c_sample, w_ada, b_ada, g_mix, g_ffn, g_final, w_in,
             w_up_a, w_up_b, w_out, w_group, b_group, w_router, b_router, w_gate_e, w_up_e, w_down_e):
    depth = w_in.shape[0]
    bp, seq, d = x_prompt.shape
    db, dseq, _ = x_sample.shape
    assert dseq == 1, "sample path is written for single-token decode"
    hd = cfg.head_dim
    off, wpad = _layout(cfg)
    ps = cfg.page_size
    ng, epg = cfg.n_groups, cfg.epg
    n_exp = ng * epg
    col = lambda a, name: a[..., off[name][0]:off[name][0] + off[name][1]]

    xp, xs = x_prompt, x_sample.reshape(1, db, d)
    c_all = jnp.concatenate([c_prompt, c_sample, jnp.zeros((-(bp + db) % 8, d), F32)], axis=0)
    rows_p, rows_s = [], []
    for layer in range(depth):
        mod = _adaln(c_all, w_ada[layer], b_ada[layer])
        mod_p = mod[:bp].reshape(bp, 6, 1, d)
        mod_s = mod[bp:bp + db].reshape(db, 6, d).transpose(1, 0, 2).reshape(1, 6, db, d)
        w_al = _permute_w_in(w_in[layer], cfg)
        def parts(w32):
            hi = w32.astype(BF16)
            return jnp.stack([hi, (w32 - hi.astype(F32)).astype(BF16)])

        wua, wub, wo = parts(w_up_a[layer]), parts(w_up_b[layer]), parts(w_out[layer])
        wr = parts(jnp.concatenate([w_group[layer], w_router[layer].transpose(1, 0, 2).reshape(d, n_exp),
                                    jnp.zeros((d, LANES - ng - n_exp), F32)], axis=1))
        br = jnp.concatenate([b_group[layer], b_router[layer].reshape(-1),
                              jnp.zeros((LANES - ng - n_exp,), F32)]).reshape(1, LANES)

        ts_p = min(256, seq)
        h_p = _rms_mod(xp, g_mix[layer], mod_p, ts_p).reshape(bp * seq, d)
        u_p = _matmul(h_p, w_al, min(cfg.tm_in, bp * seq), cfg.tn_in)
        u_p3 = u_p.reshape(bp, seq, wpad)
        oa_p = _dsa_prompt(u_p3, cfg).reshape(bp * seq, -1)
        ob_p = _sb_prompt(u_p3, cfg).reshape(bp * seq, -1)

        h_s = _rms_mod(xs, g_mix[layer], mod_s, db, out_dtype=F32).reshape(db, d)
        u_s = _matmul(h_s, w_al, db, cfg.tn_in)
        qi3 = col(u_s, "qi").reshape(db, cfg.n_idx_heads, cfg.idx_dim)
        kiwi_s = col(u_s, "kiwi")
        wi3 = (kiwi_s[:, cfg.idx_dim:cfg.idx_dim + cfg.n_idx_heads]
               * (float(cfg.n_idx_heads * cfg.idx_dim) ** -0.5)).reshape(db, cfg.n_idx_heads, 1)
        c_ik = cache_idx_k[layer]
        c_ak = cache_a_k[layer].reshape(cache_a_k.shape[1], ps, hd)
        c_av = cache_a_v[layer].reshape(cache_a_v.shape[1], ps, hd)
        c_bk, c_bv = cache_b_k[layer], cache_b_v[layer]
        sc4 = _dsa_sample_scores(qi3, wi3, c_ik, page_table, cfg)
        bias, bias_new = _dsa_sample_select(sc4.reshape(db, -1), u_s, cfg)
        qa3 = col(u_s, "qa").reshape(db, cfg.n_heads_a, hd)
        oa_s = _dsa_sample_attend(qa3, bias, bias_new, u_s.reshape(db, 1, wpad), c_ak, c_av,
                                  page_table, cfg).reshape(db, -1)
        rep = cfg.n_heads_b // cfg.n_kv_b
        qb3 = col(u_s, "qb").reshape(db, cfg.n_heads_b, 1, hd)
        gsel = (jnp.arange(cfg.n_heads_b)[:, None] // rep == jnp.arange(cfg.n_kv_b)[None, :])
        q_bd = (qb3 * gsel[None, :, :, None].astype(F32)).reshape(db, cfg.n_heads_b, cfg.n_kv_b * hd)
        ob_s = _sb_sample(q_bd, c_bk, c_bv, page_table, cfg).reshape(db, -1)

        x1_p, h2_p, rt_p = _merge(xp, oa_p, ob_p, u_p, mod_p, g_ffn[layer], wua[:1], wub[:1], wo[:1], wr, br, cfg,
                                  min(cfg.tm_merge, seq))
        x1_s, h2_s, rt_s = _merge(xs, oa_s, ob_s, u_s, mod_s, g_ffn[layer], wua, wub, wo, wr, br, cfg, db)

        h2_all = jnp.concatenate([h2_p, h2_s], axis=0)
        rt_all = jnp.concatenate([rt_p, rt_s], axis=0)
        slot_token, pos, block_expert, n_used = _moe_plan(rt_all, n_exp, cfg.moe_rows, cfg.top_k,
                                                          min(256, bp * seq))
        ys = _moe_experts(h2_all, slot_token, block_expert, n_used,
                          w_gate_e[layer], w_up_e[layer], w_down_e[layer], cfg.moe_rows)
        last = layer == depth - 1
        gfin = g_final if last else jnp.ones_like(g_final)
        assert last, "deeper stacks need an un-normalised combine for inner layers"
        npos = bp * seq * cfg.top_k
        y_p = _combine(x1_p, rt_p, mod_p, gfin, ys, pos[:npos], seq, min(cfg.tm_comb, seq))
        y_s = _combine(x1_s, rt_s, mod_s, gfin, ys, pos[npos:], db, db)

        pages = lambda a: a.reshape((bp, seq // ps, ps) + a.shape[2:])
        rows_p.append((pages(col(u_p3, "ka").reshape(bp, seq, 1, hd)),
                       pages(col(u_p3, "va").reshape(bp, seq, 1, hd)),
                       pages(col(u_p3, "kiwi")[..., :cfg.idx_dim]),
                       pages(col(u_p3, "kb").reshape(bp, seq, cfg.n_kv_b, hd)),
                       pages(col(u_p3, "vb").reshape(bp, seq, cfg.n_kv_b, hd))))
        rows_s.append((col(u_s, "ka").reshape(db, 1, 1, hd), col(u_s, "va").reshape(db, 1, 1, hd),
                       kiwi_s[:, :cfg.idx_dim].reshape(db, 1, cfg.idx_dim),
                       col(u_s, "kb").reshape(db, 1, cfg.n_kv_b, hd),
                       col(u_s, "vb").reshape(db, 1, cfg.n_kv_b, hd)))
        xp, xs = y_p.reshape(bp, seq, d), y_s.reshape(1, db, d)

    stack = lambda rows: tuple(jnp.stack(r) for r in zip(*rows))
    return (xp, xs.reshape(db, 1, d)) + stack(rows_p) + stack(rows_s)


def kernel(x_prompt, x_sample, cache_a_k, cache_a_v, cache_idx_k, cache_b_k, cache_b_v, page_table,
           c_prompt, c_sample, w_ada, b_ada, g_mix, g_ffn, g_final, w_in, w_up_a, w_up_b, w_out,
           w_group, b_group, w_router, b_router, w_gate_e, w_up_e, w_down_e):
    return _forward(Cfg(), x_prompt, x_sample, cache_a_k, cache_a_v, cache_idx_k, cache_b_k, cache_b_v,
                    page_table, c_prompt, c_sample, w_ada, b_ada, g_mix, g_ffn, g_final, w_in,
                    w_up_a, w_up_b, w_out, w_group, b_group, w_router, b_router, w_gate_e, w_up_e, w_down_e)
```

```python
import functools
from typing import NamedTuple

import numpy as np
import jax
import jax.numpy as jnp
from jax import lax
from jax.experimental import pallas as pl
from jax.experimental.pallas import tpu as pltpu

F32 = jnp.float32
BF16 = jnp.bfloat16
I32 = jnp.int32

LANES = 128
INT_MIN = -(2 ** 31)
RMS_EPS = 1e-6
NEG_BIG = -1e30
SB_DEAD_TAIL = -110.0
VMEM_LIMIT = 48 * 1024 * 1024


class Cfg(NamedTuple):
    d_model: int = 2048
    head_dim: int = 128
    n_heads_a: int = 8
    n_idx_heads: int = 16
    idx_dim: int = 64
    n_heads_b: int = 8
    n_kv_b: int = 4
    topk_max: int = 256
    n_groups: int = 8
    epg: int = 8
    top_k: int = 2
    d_ff: int = 512
    page_size: int = 128
    tn_in: int = 512
    tm_in: int = 1024
    tq_a: int = 256
    tk_a: int = 256
    t_b: int = 256
    tm_merge: int = 256
    moe_rows: int = 128
    tm_comb: int = 128
    pages_per_step: int = 8
    sb_pages_per_chunk: int = 4


def _cparams(sem):
    return pltpu.CompilerParams(dimension_semantics=sem, vmem_limit_bytes=VMEM_LIMIT)


def _round_up(a, b):
    return -(-a // b) * b


def _layout(cfg):
    d, hd = cfg.d_model, cfg.head_dim
    segs = [("ga", d), ("gb", d), ("qa", cfg.n_heads_a * hd), ("qi", cfg.n_idx_heads * cfg.idx_dim),
            ("qb", cfg.n_heads_b * hd), ("kb", cfg.n_kv_b * hd), ("vb", cfg.n_kv_b * hd),
            ("ka", hd), ("va", hd), ("kiwi", LANES)]
    off, o = {}, 0
    for name, w in segs:
        off[name] = (o, w)
        o += w
    return off, _round_up(o, cfg.tn_in)


def _permute_w_in(w_in, cfg):
    hd = cfg.head_dim
    sizes = [cfg.n_heads_a * hd, hd, hd, cfg.n_idx_heads * cfg.idx_dim, cfg.idx_dim, cfg.n_idx_heads,
             cfg.n_heads_b * hd, cfg.n_kv_b * hd, cfg.n_kv_b * hd, cfg.d_model, cfg.d_model]
    names = ["qa", "ka", "va", "qi", "ki", "wi", "qb", "kb", "vb", "ga", "gb"]
    cuts = np.concatenate([[0], np.cumsum(sizes)])
    src = {n: w_in[:, int(cuts[i]):int(cuts[i + 1])] for i, n in enumerate(names)}
    off, width = _layout(cfg)
    assert cfg.idx_dim + cfg.n_idx_heads <= LANES
    kiwi = jnp.concatenate(
        [src["ki"], src["wi"], jnp.zeros((w_in.shape[0], LANES - cfg.idx_dim - cfg.n_idx_heads), w_in.dtype)], axis=1)
    parts = [src["ga"], src["gb"], src["qa"], src["qi"], src["qb"], src["kb"], src["vb"], src["ka"], src["va"], kiwi]
    used = sum(p.shape[1] for p in parts)
    if width > used:
        parts.append(jnp.zeros((w_in.shape[0], width - used), w_in.dtype))
    return jnp.concatenate(parts, axis=1)


def _split(x):
    hi = x.astype(BF16)
    return hi, (x - hi.astype(F32)).astype(BF16)


def _dot3(a, b, dot=None):
    dot = dot or (lambda x, y: jnp.dot(x, y, preferred_element_type=F32))
    (ah, al), (bh, bl) = _split(a), _split(b)
    return dot(ah, bh) + dot(al, bh) + dot(ah, bl)


def _adaln_kernel(c_ref, w_ref, b_ref, o_ref):
    c = c_ref[...]
    o_ref[...] = _dot3(c * (1.0 / (1.0 + jnp.exp(-c))), w_ref[...]) + b_ref[...]


def _adaln(c_all, w_ada, b_ada):
    mc, d = c_all.shape
    n = w_ada.shape[1]
    tn = 1024 if n % 1024 == 0 else n
    return pl.pallas_call(
        _adaln_kernel,
        out_shape=jax.ShapeDtypeStruct((mc, n), F32),
        grid=(n // tn,),
        in_specs=[pl.BlockSpec((mc, d), lambda j: (0, 0)),
                  pl.BlockSpec((d, tn), lambda j: (0, j)),
                  pl.BlockSpec((1, tn), lambda j: (0, j))],
        out_specs=pl.BlockSpec((mc, tn), lambda j: (0, j)),
        compiler_params=_cparams(("arbitrary",)),
        name="adaln",
    )(c_all, w_ada, b_ada.reshape(1, n))


def _rms_mod_kernel(x_ref, g_ref, m_ref, o_ref):
    x = x_ref[0]
    y = x * lax.rsqrt(jnp.mean(x * x, axis=-1, keepdims=True) + RMS_EPS) * g_ref[...]
    o_ref[0] = (y * (1.0 + m_ref[0, 1]) + m_ref[0, 0]).astype(o_ref.dtype)


def _rms_mod(x3, g, mod, ts, out_dtype=None):
    bx, t, d = x3.shape
    r = mod.shape[2]
    out_dtype = out_dtype or BF16
    return pl.pallas_call(
        _rms_mod_kernel,
        out_shape=jax.ShapeDtypeStruct((bx, t, d), out_dtype),
        grid=(bx, t // ts),
        in_specs=[pl.BlockSpec((1, ts, d), lambda b, i: (b, i, 0)),
                  pl.BlockSpec((1, d), lambda b, i: (0, 0)),
                  pl.BlockSpec((1, 6, r, d), lambda b, i: (b, 0, 0, 0))],
        out_specs=pl.BlockSpec((1, ts, d), lambda b, i: (b, i, 0)),
        compiler_params=_cparams(("arbitrary", "arbitrary")),
        name="rms_mod",
    )(x3, g.reshape(1, d), mod)


def _mm_kernel(a_ref, w_ref, o_ref, wbf_ref):
    @pl.when(pl.program_id(1) == 0)
    def _():
        wbf_ref[...] = w_ref[...].astype(BF16)

    o_ref[...] = jnp.dot(a_ref[...], wbf_ref[...], preferred_element_type=F32).astype(o_ref.dtype)


def _mm3_kernel(a_ref, w_ref, o_ref, wbf_ref):
    del wbf_ref
    o_ref[...] = _dot3(a_ref[...], w_ref[...]).astype(o_ref.dtype)


def _matmul(a, w, tm, tn, out_dtype=F32):
    m, k = a.shape
    n = w.shape[1]
    return pl.pallas_call(
        _mm_kernel if a.dtype == BF16 else _mm3_kernel,
        out_shape=jax.ShapeDtypeStruct((m, n), out_dtype),
        grid=(n // tn, m // tm),
        in_specs=[pl.BlockSpec((tm, k), lambda j, i: (i, 0)),
                  pl.BlockSpec((k, tn), lambda j, i: (0, j))],
        out_specs=pl.BlockSpec((tm, tn), lambda j, i: (i, j)),
        scratch_shapes=[pltpu.VMEM((k, tn), BF16)],
        compiler_params=_cparams(("arbitrary", "arbitrary")),
        name="in_proj",
    )(a, w)


def _ordered_bits_to_float(u):
    key = u ^ jnp.int32(INT_MIN)
    return pltpu.bitcast(jnp.where(key < 0, key ^ jnp.int32(0x7FFFFFFF), key), F32)


def _kth_largest(count_ge, shape, k):
    def body(bi, res):
        cand = res | lax.shift_left(jnp.int32(1), jnp.int32(31) - bi)
        return jnp.where(count_ge(_ordered_bits_to_float(cand)) >= k, cand, res)

    thr = _ordered_bits_to_float(lax.fori_loop(0, 32, body, jnp.zeros(shape, I32)))
    return jnp.where(thr != thr, -jnp.inf, thr)


def _tie_cutoff(count_tie_lt, shape, need, nbits):
    def body(bi, x):
        cand = x | lax.shift_left(jnp.int32(1), jnp.int32(nbits - 1) - bi)
        return jnp.where(count_tie_lt(cand) < need, cand, x)

    return lax.fori_loop(0, nbits, body, jnp.zeros(shape, I32))


def _fold_lanes(x):
    acc = x[:, 0:LANES]
    for j in range(1, x.shape[1] // LANES):
        acc = acc + x[:, j * LANES:(j + 1) * LANES]
    return acc


def _dot_nt(a, b):
    return lax.dot_general(a, b, (((1,), (1,)), ((), ())), preferred_element_type=F32)


def _dsa_prompt_kernel(qa_ref, qi_ref, kwq_ref, kw_ref, ka_ref, va_ref, o_ref,
                       ka_s, kb_s, kbf_s, vt_s, sc_s, dist_s, xcut_s, qbf_s, m_s, l_s, acc_s, lg_s, p_s, al_s,
                       *, cfg, tq, tk, n_sel):
    hd, idim, ni, ha = cfg.head_dim, cfg.idx_dim, cfg.n_idx_heads, cfg.n_heads_a
    s_total = sc_s.shape[0]
    i = pl.program_id(1)
    t0 = i * tq
    nk = (t0 + tq + tk - 1) // tk

    @pl.when(i == 0)
    def _stage():
        kw = kw_ref[0]
        lane = lax.broadcasted_iota(I32, kw.shape, 1)
        k_lo = jnp.where(lane < idim, kw, 0.0)
        ka_s[...] = k_lo.astype(BF16)
        kb_s[...] = pltpu.roll(k_lo, idim, 1).astype(BF16)
        kbf_s[...] = ka_ref[0].astype(BF16)
        for c in range(s_total // tk):
            vt_s[:, c * tk:(c + 1) * tk] = va_ref[0, c * tk:(c + 1) * tk, :].T.astype(BF16)

    w_t = (kwq_ref[0] * (float(ni * idim) ** -0.5)).T
    qpos = t0 + lax.broadcasted_iota(I32, (tk, tq), 1)
    krow = lax.broadcasted_iota(I32, (tk, tq), 0)

    def score_chunk(c, carry):
        ks = pl.multiple_of(c * tk, tk)
        k_even = ka_s[pl.ds(ks, tk), :]
        k_odd = kb_s[pl.ds(ks, tk), :]
        acc = jnp.zeros((tk, tq), F32)
        for p in range(ni // 2):
            qp = qi_ref[0, :, p * LANES:(p + 1) * LANES].astype(BF16)
            for par, kk in ((0, k_even), (1, k_odd)):
                h = 2 * p + par
                acc = acc + w_t[idim + h:idim + h + 1, :] * jnp.maximum(_dot_nt(kk, qp), 0.0)
        sc_s[pl.ds(ks, tk), :] = jnp.where(ks + krow <= qpos, acc, -jnp.inf)
        return carry

    lax.fori_loop(0, nk, score_chunk, 0)

    def count(pred):
        def body(c, cnt):
            ks = pl.multiple_of(c * tk, tk)
            hit = pred(sc_s[pl.ds(ks, tk), :], ks + krow).astype(I32)
            return cnt + jnp.sum(hit.reshape(tk // 8, 8, tq), axis=0)

        cnt = lax.fori_loop(0, nk, body, jnp.zeros((8, tq), I32))
        return jnp.sum(cnt, axis=0, keepdims=True)

    thr = _kth_largest(lambda t: count(lambda s, k: s >= t), (1, tq), n_sel)

    xcut_s[...] = jnp.full(xcut_s.shape, s_total, I32)

    @pl.when(jnp.max(count(lambda s, k: s >= thr)) > n_sel)
    def _ties():
        need = n_sel - count(lambda s, k: s > thr)
        x = _tie_cutoff(lambda cand: count(lambda s, k: (s == thr) & (k < cand)), (1, tq), need,
                        max(1, (s_total - 1).bit_length()))
        xcut_s[...] = jnp.broadcast_to(x, xcut_s.shape)

    xcut = xcut_s[0:1, :]

    def mask_chunk(c, carry):
        ks = pl.multiple_of(c * tk, tk)
        kpos = ks + krow
        s = sc_s[pl.ds(ks, tk), :]
        sel = ((s > thr) | ((s == thr) & (kpos <= xcut))) & (kpos <= qpos)
        dist_s[pl.ds(ks, tk), :] = jnp.where(sel, (qpos - kpos).astype(F32), -NEG_BIG)
        return carry

    lax.fori_loop(0, nk, mask_chunk, 0)
    scale = float(hd) ** -0.5

    for h in range(ha):
        qbf_s[h] = qa_ref[0, :, h * hd:(h + 1) * hd].astype(BF16)
    m_s[...] = jnp.full(m_s.shape, NEG_BIG, F32)
    l_s[...] = jnp.zeros(l_s.shape, F32)
    acc_s[...] = jnp.zeros(acc_s.shape, F32)

    def att_chunk(c, carry):
        ks = pl.multiple_of(c * tk, tk)
        kc = kbf_s[pl.ds(ks, tk), :]
        vt = vt_s[:, pl.ds(ks, tk)]
        dist = dist_s[pl.ds(ks, tk), :]
        for h in range(ha):
            lg_s[h] = _dot_nt(kc, qbf_s[h])
        for h in range(ha):
            slope = float(2.0 ** (-8.0 * (h + 1) / ha))
            lg = lg_s[h] * scale - slope * dist
            m = m_s[h]
            m_new = jnp.maximum(m, jnp.max(lg, axis=0, keepdims=True))
            p = jnp.exp(lg - m_new)
            alpha = jnp.exp(m - m_new)
            l_s[h] = alpha * l_s[h] + jnp.sum(p, axis=0, keepdims=True)
            p_s[h] = p.astype(BF16)
            al_s[h] = alpha
            m_s[h] = m_new
        for h in range(ha):
            acc_s[h] = al_s[h] * acc_s[h] + jnp.dot(vt, p_s[h], preferred_element_type=F32)
        return carry

    lax.fori_loop(0, nk, att_chunk, 0)
    for h in range(ha):
        o_ref[0, :, h * hd:(h + 1) * hd] = (acc_s[h] / l_s[h]).T.astype(o_ref.dtype)


def _dsa_prompt(u3, cfg):
    b, s, _ = u3.shape
    off, _ = _layout(cfg)
    tq, tk = min(cfg.tq_a, s), min(cfg.tk_a, s)
    n_sel = min(cfg.topk_max, s // 4)
    aw = cfg.n_heads_a * cfg.head_dim
    iw = cfg.n_idx_heads * cfg.idx_dim
    assert 2 * cfg.idx_dim == LANES and s % tk == 0 and s % tq == 0 and tk >= n_sel
    cb = lambda name, w: off[name][0] // w
    kern = functools.partial(_dsa_prompt_kernel, cfg=cfg, tq=tq, tk=tk, n_sel=n_sel)
    return pl.pallas_call(
        kern,
        out_shape=jax.ShapeDtypeStruct((b, s, aw), BF16),
        grid=(b, s // tq),
        in_specs=[pl.BlockSpec((1, tq, aw), lambda bi, i: (bi, i, cb("qa", aw))),
                  pl.BlockSpec((1, tq, iw), lambda bi, i: (bi, i, cb("qi", iw))),
                  pl.BlockSpec((1, tq, LANES), lambda bi, i: (bi, i, cb("kiwi", LANES))),
                  pl.BlockSpec((1, s, LANES), lambda bi, i: (bi, 0, cb("kiwi", LANES))),
                  pl.BlockSpec((1, s, LANES), lambda bi, i: (bi, 0, cb("ka", LANES))),
                  pl.BlockSpec((1, s, LANES), lambda bi, i: (bi, 0, cb("va", LANES)))],
        out_specs=pl.BlockSpec((1, tq, aw), lambda bi, i: (bi, i, 0)),
        scratch_shapes=[pltpu.VMEM((s, LANES), BF16), pltpu.VMEM((s, LANES), BF16),
                        pltpu.VMEM((s, LANES), BF16), pltpu.VMEM((cfg.head_dim, s), BF16),
                        pltpu.VMEM((s, tq), F32), pltpu.VMEM((s, tq), F32), pltpu.VMEM((8, tq), I32),
                        pltpu.VMEM((cfg.n_heads_a, tq, cfg.head_dim), BF16),
                        pltpu.VMEM((cfg.n_heads_a, 1, tq), F32), pltpu.VMEM((cfg.n_heads_a, 1, tq), F32),
                        pltpu.VMEM((cfg.n_heads_a, cfg.head_dim, tq), F32),
                        pltpu.VMEM((cfg.n_heads_a, tk, tq), F32), pltpu.VMEM((cfg.n_heads_a, tk, tq), BF16),
                        pltpu.VMEM((cfg.n_heads_a, 1, tq), F32)],
        compiler_params=_cparams(("arbitrary", "arbitrary")),
        name="dsa_prompt",
    )(u3, u3, u3, u3, u3, u3)


def _sb_terms(q, kc, scale):
    z = _dot_nt(q, kc) * scale
    lk = -(jnp.maximum(z, 0.0) + jnp.log(1.0 + jnp.exp(-jnp.abs(z))))
    return z, lk


def _suffix_in_chunk(lk, upper):
    hi = lk.astype(BF16)
    lo = (lk - hi.astype(F32)).astype(BF16)
    return (jnp.dot(hi, upper, preferred_element_type=F32) + jnp.dot(lo, upper, preferred_element_type=F32))


def _strict_lower_ones(n):
    return (lax.broadcasted_iota(I32, (n, n), 0) > lax.broadcasted_iota(I32, (n, n), 1)).astype(BF16)


def _sb_prompt_kernel(q_ref, k_ref, v_ref, o_ref, kbf_s, vbf_s, *, t, hd, rep):
    i = pl.program_id(2)

    @pl.when(i == 0)
    def _stage():
        kbf_s[...] = k_ref[0].astype(BF16)
        vbf_s[...] = v_ref[0].astype(BF16)

    qs = [q_ref[0, :, r * hd:(r + 1) * hd].astype(BF16) for r in range(rep)]
    scale = float(hd) ** -0.5
    upper = _strict_lower_ones(t)
    before = lax.broadcasted_iota(I32, (t, t), 1) < lax.broadcasted_iota(I32, (t, t), 0)

    ks = pl.multiple_of(i * t, t)
    kc, vc = kbf_s[pl.ds(ks, t), :], vbf_s[pl.ds(ks, t), :]
    state = []
    for q in qs:
        z, lk = _sb_terms(q, kc, scale)
        lkm = jnp.where(before, lk, 0.0)
        tail = _suffix_in_chunk(lkm, upper)
        w = jnp.where(before, jnp.exp(z + lk + tail), 0.0)
        state.append(jnp.sum(lkm, axis=-1, keepdims=True))
        state.append(jnp.dot(w.astype(BF16), vc, preferred_element_type=F32))

    def live(st):
        top = st[0]
        for r in range(1, rep):
            top = jnp.maximum(top, st[2 * r])
        return jnp.max(top) > SB_DEAD_TAIL

    def body(loop):
        j, _, st = loop
        ks = pl.multiple_of((i - 1 - j) * t, t)
        kc, vc = kbf_s[pl.ds(ks, t), :], vbf_s[pl.ds(ks, t), :]
        out = []
        for r, q in enumerate(qs):
            carry, acc = st[2 * r], st[2 * r + 1]
            z, lk = _sb_terms(q, kc, scale)
            tail = carry + _suffix_in_chunk(lk, upper)
            w = jnp.exp(z + lk + tail)
            out.append(carry + jnp.sum(lk, axis=-1, keepdims=True))
            out.append(acc + jnp.dot(w.astype(BF16), vc, preferred_element_type=F32))
        return j + 1, live(out), tuple(out)

    _, _, st = lax.while_loop(lambda loop: (loop[0] < i) & loop[1], body, (jnp.int32(0), live(state), tuple(state)))
    for r in range(rep):
        o_ref[0, :, r * hd:(r + 1) * hd] = st[2 * r + 1].astype(o_ref.dtype)


def _sb_prompt(u3, cfg):
    b, s, _ = u3.shape
    off, _ = _layout(cfg)
    hd, hb, g = cfg.head_dim, cfg.n_heads_b, cfg.n_kv_b
    rep = hb // g
    t = min(cfg.t_b, s)
    assert s % t == 0
    qw = rep * hd
    assert off["qb"][0] % qw == 0
    qb0, kb0, vb0 = off["qb"][0] // qw, off["kb"][0] // hd, off["vb"][0] // hd
    kern = functools.partial(_sb_prompt_kernel, t=t, hd=hd, rep=rep)
    return pl.pallas_call(
        kern,
        out_shape=jax.ShapeDtypeStruct((b, s, hb * hd), BF16),
        grid=(b, g, s // t),
        in_specs=[pl.BlockSpec((1, t, qw), lambda bi, gi, i: (bi, i, qb0 + gi)),
                  pl.BlockSpec((1, s, hd), lambda bi, gi, i: (bi, 0, kb0 + gi)),
                  pl.BlockSpec((1, s, hd), lambda bi, gi, i: (bi, 0, vb0 + gi))],
        out_specs=pl.BlockSpec((1, t, qw), lambda bi, gi, i: (bi, i, gi)),
        scratch_shapes=[pltpu.VMEM((s, hd), BF16), pltpu.VMEM((s, hd), BF16)],
        compiler_params=_cparams(("arbitrary", "arbitrary", "arbitrary")),
        name="sb_prompt",
    )(u3, u3, u3)


def _page_specs(n, block, page_of):
    zeros = (0,) * (len(block) - 1)
    return [pl.BlockSpec((None,) + tuple(block[1:]),
                         functools.partial(lambda b, j, pt, r: (page_of(b, j, r, pt),) + zeros, r=r))
            for r in range(n)]


def _dsa_sample_score_kernel(pt_ref, q_ref, w_ref, *rest, pg):
    k_refs, o_ref = rest[:pg], rest[pg]
    q = q_ref[0]
    w = w_ref[0]
    for r in range(pg):
        d = _dot3(q, k_refs[r][...], _dot_nt)
        o_ref[0, 0, r:r + 1, :] = jnp.sum(w * jnp.maximum(d, 0.0), axis=0, keepdims=True)


def _dsa_sample_scores(qi3, wi3, cache_ik, page_table, cfg):
    db, ni, idim = qi3.shape
    pool, ps, _ = cache_ik.shape
    npg = page_table.shape[1]
    pg = min(cfg.pages_per_step, npg)
    assert npg % pg == 0 and ps == LANES
    page_of = lambda b, j, r, pt: pt[b * npg + j * pg + r]
    grid_spec = pltpu.PrefetchScalarGridSpec(
        num_scalar_prefetch=1,
        grid=(db, npg // pg),
        in_specs=[pl.BlockSpec((1, ni, idim), lambda b, j, pt: (b, 0, 0)),
                  pl.BlockSpec((1, ni, 1), lambda b, j, pt: (b, 0, 0))]
                 + _page_specs(pg, (1, ps, idim), page_of),
        out_specs=pl.BlockSpec((1, 1, pg, ps), lambda b, j, pt: (b, j, 0, 0)),
    )
    out = pl.pallas_call(
        functools.partial(_dsa_sample_score_kernel, pg=pg),
        out_shape=jax.ShapeDtypeStruct((db, npg // pg, pg, ps), F32),
        grid_spec=grid_spec,
        compiler_params=_cparams(("arbitrary", "arbitrary")),
        name="dsa_sample_scores",
    )(page_table.reshape(-1), qi3, wi3, *([cache_ik] * pg))
    return out


def _dsa_sample_select_kernel(sc_ref, qi_ref, kw_ref, bias_ref, bnew_ref, *, cfg, n_sel, chunk):
    idim, ni = cfg.idx_dim, cfg.n_idx_heads
    db, p = sc_ref.shape
    nchunk = p // chunk
    kw = kw_ref[...]
    ki = kw[:, 0:idim]
    s_new = jnp.zeros((db, 1), F32)
    for h in range(ni):
        d = jnp.sum(qi_ref[:, h * idim:(h + 1) * idim] * ki, axis=-1, keepdims=True)
        s_new = s_new + (kw[:, idim + h:idim + h + 1] * (float(ni * idim) ** -0.5)) * jnp.maximum(d, 0.0)
    kn1 = s_new
    idx0 = lax.broadcasted_iota(I32, (db, chunk), 1)

    def count(pred, new_hit):
        cnt = jnp.zeros((db, LANES), I32)
        for c in range(nchunk):
            cnt = cnt + _fold_lanes(pred(sc_ref[:, c * chunk:(c + 1) * chunk], c * chunk + idx0).astype(I32))
        return jnp.sum(cnt, axis=-1, keepdims=True) + new_hit.astype(I32)

    thr = _kth_largest(lambda t: count(lambda k, i: k >= t, kn1 >= t), (db, 1), n_sel)
    need = n_sel - count(lambda k, i: k > thr, kn1 > thr)
    xcut = _tie_cutoff(lambda x: count(lambda k, i: (k == thr) & (i < x), (kn1 == thr) & (p < x)),
                       (db, 1), need, max(1, p.bit_length()))
    for c in range(nchunk):
        k = sc_ref[:, c * chunk:(c + 1) * chunk]
        sel = (k > thr) | ((k == thr) & (c * chunk + idx0 <= xcut))
        bias_ref[:, c * chunk:(c + 1) * chunk] = jnp.where(sel, 0.0, NEG_BIG)
    sel_new = (kn1 > thr) | ((kn1 == thr) & (p <= xcut))
    bnew_ref[...] = jnp.broadcast_to(jnp.where(sel_new, 0.0, NEG_BIG), (db, LANES))


def _dsa_sample_select(scores, u_s, cfg):
    db, p = scores.shape
    off, _ = _layout(cfg)
    iw = cfg.n_idx_heads * cfg.idx_dim
    n_sel = min(cfg.topk_max, (p + 1) // 4)
    chunk = min(2048, p)
    assert p % chunk == 0 and p + 1 >= n_sel
    kern = functools.partial(_dsa_sample_select_kernel, cfg=cfg, n_sel=n_sel, chunk=chunk)
    return pl.pallas_call(
        kern,
        out_shape=(jax.ShapeDtypeStruct((db, p), F32), jax.ShapeDtypeStruct((db, LANES), F32)),
        grid=(1,),
        in_specs=[pl.BlockSpec((db, p), lambda i: (0, 0)),
                  pl.BlockSpec((db, iw), lambda i: (0, off["qi"][0] // iw)),
                  pl.BlockSpec((db, LANES), lambda i: (0, off["kiwi"][0] // LANES))],
        out_specs=(pl.BlockSpec((db, p), lambda i: (0, 0)), pl.BlockSpec((db, LANES), lambda i: (0, 0))),
        compiler_params=_cparams(("arbitrary",)),
        name="dsa_sample_select",
    )(scores, u_s, u_s)


def _dsa_sample_attend_kernel(pt_ref, q_ref, bias_ref, bnew_ref, knew_ref, vnew_ref, *rest,
                              pg, cfg, past):
    k_refs, v_refs = rest[:pg], rest[pg:2 * pg]
    o_ref, m_s, l_s, acc_s = rest[2 * pg:]
    hd, ha = cfg.head_dim, cfg.n_heads_a
    j = pl.program_id(1)
    scale = float(hd) ** -0.5

    @pl.when(j == 0)
    def _init():
        m_s[...] = jnp.full(m_s.shape, NEG_BIG, F32)
        l_s[...] = jnp.zeros(l_s.shape, F32)
        acc_s[...] = jnp.zeros(acc_s.shape, F32)

    q = q_ref[0]
    hidx = lax.broadcasted_iota(I32, (ha, 1), 0)
    slopes = jnp.zeros((ha, 1), F32)
    for h in range(ha):
        slopes = jnp.where(hidx == h, float(2.0 ** (-8.0 * (h + 1) / ha)), slopes)

    def update(lg, bias, vs):
        sel = jnp.broadcast_to(bias == 0.0, lg.shape)
        lg = lg + bias
        m = m_s[...]
        m_new = jnp.maximum(m, jnp.max(lg, axis=-1, keepdims=True))
        p = jnp.where(sel, jnp.exp(lg - m_new), 0.0)
        alpha = jnp.exp(m - m_new)
        l_s[...] = alpha * l_s[...] + jnp.sum(p, axis=-1, keepdims=True)
        w = vs[0].shape[0]
        pv = _dot3(p[:, 0:w], vs[0])
        for r in range(1, len(vs)):
            pv = pv + _dot3(p[:, r * w:(r + 1) * w], vs[r])
        acc_s[...] = alpha * acc_s[...] + pv
        m_s[...] = m_new

    n = pg * LANES
    pos = j * n + lax.broadcasted_iota(I32, (1, n), 1)
    lg = jnp.concatenate([_dot3(q, k_refs[r][...], _dot_nt) for r in range(pg)], axis=1)
    lg = lg * scale - slopes * (past - pos).astype(F32)
    update(lg, bias_ref[0, 0], [v_refs[r][...] for r in range(pg)])

    @pl.when(j == pl.num_programs(1) - 1)
    def _fin():
        kn = jnp.broadcast_to(knew_ref[0], (16, hd))
        vn = jnp.broadcast_to(vnew_ref[0], (16, hd))
        lg = _dot3(q, kn, _dot_nt) * scale
        first = lax.broadcasted_iota(I32, (1, 16), 1) == 0
        update(lg, jnp.where(first, bnew_ref[0][:, 0:16], NEG_BIG), [vn])
        o_ref[0] = (acc_s[...] / l_s[...]).astype(o_ref.dtype)


def _dsa_sample_attend(qa3, bias, bias_new, u_s3, cache_k, cache_v, page_table, cfg):
    db, ha, hd = qa3.shape
    npg = page_table.shape[1]
    pg = min(cfg.pages_per_step, npg)
    ps = cache_k.shape[1]
    bias4 = bias.reshape(db, npg // pg, 1, pg * ps)
    off, _ = _layout(cfg)
    page_of = lambda b, j, r, pt: pt[b * npg + j * pg + r]
    kern = functools.partial(_dsa_sample_attend_kernel, pg=pg, cfg=cfg, past=npg * ps)
    grid_spec = pltpu.PrefetchScalarGridSpec(
        num_scalar_prefetch=1,
        grid=(db, npg // pg),
        in_specs=[pl.BlockSpec((1, ha, hd), lambda b, j, pt: (b, 0, 0)),
                  pl.BlockSpec((1, 1, 1, pg * ps), lambda b, j, pt: (b, j, 0, 0)),
                  pl.BlockSpec((1, 1, LANES), lambda b, j, pt: (b, 0, 0)),
                  pl.BlockSpec((1, 1, hd), lambda b, j, pt: (b, 0, off["ka"][0] // hd)),
                  pl.BlockSpec((1, 1, hd), lambda b, j, pt: (b, 0, off["va"][0] // hd))]
                 + _page_specs(pg, (1, ps, hd), page_of) + _page_specs(pg, (1, ps, hd), page_of),
        out_specs=pl.BlockSpec((1, ha, hd), lambda b, j, pt: (b, 0, 0)),
        scratch_shapes=[pltpu.VMEM((ha, 1), F32), pltpu.VMEM((ha, 1), F32), pltpu.VMEM((ha, hd), F32)],
    )
    return pl.pallas_call(
        kern,
        out_shape=jax.ShapeDtypeStruct((db, ha, hd), F32),
        grid_spec=grid_spec,
        compiler_params=_cparams(("arbitrary", "arbitrary")),
        name="dsa_sample_attend",
    )(page_table.reshape(-1), qa3, bias4, bias_new.reshape(db, 1, LANES),
      u_s3, u_s3, *([cache_k] * pg), *([cache_v] * pg))


def _sb_sample_kernel(pt_ref, q_ref, k_hbm, v_hbm, o_ref, kbuf, vbuf, sem, *, pg, npg, cfg):
    hd, hb, g = cfg.head_dim, cfg.n_heads_b, cfg.n_kv_b
    rep = hb // g
    b = pl.program_id(0)
    nchunks = npg // pg
    scale = float(hd) ** -0.5

    def page_copies(c, slot):
        out = []
        for r in range(pg):
            page = pt_ref[b * npg + (npg - 1 - (c * pg + r))]
            out.append(pltpu.make_async_copy(k_hbm.at[page], kbuf.at[slot, r], sem.at[0, slot]))
            out.append(pltpu.make_async_copy(v_hbm.at[page], vbuf.at[slot, r], sem.at[1, slot]))
        return out

    q = q_ref[0]
    qg = [q[:, gi * hd:(gi + 1) * hd] for gi in range(g)]
    upper = _strict_lower_ones(LANES)

    for cp in page_copies(0, 0):
        cp.start()

    def body(loop):
        c, _, carry, accs = loop
        slot = c % 2

        @pl.when(c + 1 < nchunks)
        def _prefetch():
            for cp in page_copies(c + 1, 1 - slot):
                cp.start()

        for cp in page_copies(c, slot):
            cp.wait()
        zs = []
        for r in range(pg):
            z = _dot3(qg[0], kbuf[slot, r, :, 0, :], _dot_nt)
            for gi in range(1, g):
                z = z + _dot3(qg[gi], kbuf[slot, r, :, gi, :], _dot_nt)
            zs.append(z * scale)
        z_all = jnp.concatenate(zs, axis=0)
        lk_all = -(jnp.maximum(z_all, 0.0) + jnp.log(1.0 + jnp.exp(-jnp.abs(z_all))))
        tail_in = _suffix_in_chunk(lk_all, upper)
        tot = jnp.sum(lk_all, axis=-1, keepdims=True)
        accs = list(accs)
        for r in range(pg):
            sl = slice(r * hb, (r + 1) * hb)
            w = jnp.exp(z_all[sl] + lk_all[sl] + (carry + tail_in[sl]))
            for gi in range(g):
                accs[gi] = accs[gi] + _dot3(w, vbuf[slot, r, :, gi, :])
            carry = carry + tot[sl]
        return c + 1, jnp.max(carry) > SB_DEAD_TAIL, carry, tuple(accs)

    init = (jnp.int32(0), jnp.bool_(True), jnp.zeros((hb, 1), F32), tuple(jnp.zeros((hb, hd), F32) for _ in range(g)))
    c_end, _, _, accs = lax.while_loop(lambda loop: (loop[0] < nchunks) & loop[1], body, init)

    @pl.when(c_end < nchunks)
    def _drain():
        for cp in page_copies(c_end, c_end % 2):
            cp.wait()

    head_group = lax.broadcasted_iota(I32, (hb, hd), 0) // rep
    out = jnp.zeros((hb, hd), F32)
    for gi in range(g):
        out = out + jnp.where(head_group == gi, accs[gi], 0.0)
    o_ref[0] = out.astype(o_ref.dtype)


def _sb_sample(q_bd, cache_k, cache_v, page_table, cfg):
    db, hb, gw = q_bd.shape
    npg = page_table.shape[1]
    _, ps, g, hd = cache_k.shape
    pg = min(cfg.sb_pages_per_chunk, npg)
    assert npg % pg == 0 and ps == LANES and g * hd == gw
    grid_spec = pltpu.PrefetchScalarGridSpec(
        num_scalar_prefetch=1,
        grid=(db,),
        in_specs=[pl.BlockSpec((1, hb, gw), lambda b, pt: (b, 0, 0)),
                  pl.BlockSpec(memory_space=pl.ANY), pl.BlockSpec(memory_space=pl.ANY)],
        out_specs=pl.BlockSpec((1, hb, cfg.head_dim), lambda b, pt: (b, 0, 0)),
        scratch_shapes=[pltpu.VMEM((2, pg, ps, g, hd), F32), pltpu.VMEM((2, pg, ps, g, hd), F32),
                        pltpu.SemaphoreType.DMA((2, 2))],
    )
    return pl.pallas_call(
        functools.partial(_sb_sample_kernel, pg=pg, npg=npg, cfg=cfg),
        out_shape=jax.ShapeDtypeStruct((db, hb, cfg.head_dim), F32),
        grid_spec=grid_spec,
        compiler_params=_cparams(("arbitrary",)),
        name="sb_sample",
    )(page_table.reshape(-1), q_bd, cache_k, cache_v)


def _merge_kernel(x_ref, oa_ref, ob_ref, ga_ref, gb_ref, m_ref, gf_ref, wua_ref, wub_ref, wo_ref, wr_ref, br_ref,
                  x1_ref, h2_ref, rt_ref, *, cfg):
    ng, epg = cfg.n_groups, cfg.epg
    sig = lambda v: 1.0 / (1.0 + jnp.exp(-v))

    def mm(a, w_ref):
        dot = lambda p, q: jnp.dot(p, q, preferred_element_type=F32)
        if w_ref.shape[0] == 1:
            return dot(a.astype(BF16), w_ref[0])
        a_hi, a_lo = _split(a)
        return dot(a_hi, w_ref[0]) + dot(a_lo, w_ref[0]) + dot(a_hi, w_ref[1])

    merged = sig(ga_ref[...]) * mm(oa_ref[...], wua_ref) + sig(gb_ref[...]) * mm(ob_ref[...], wub_ref)
    x1 = x_ref[0] + m_ref[0, 2] * mm(merged, wo_ref)
    x1_ref[...] = x1
    h2 = x1 * lax.rsqrt(jnp.mean(x1 * x1, axis=-1, keepdims=True) + RMS_EPS) * gf_ref[...]
    h2 = h2 * (1.0 + m_ref[0, 4]) + m_ref[0, 3]
    h2_ref[...] = h2
    logits = mm(h2, wr_ref) + br_ref[...]

    lane = lax.broadcasted_iota(I32, logits.shape, 1).astype(F32)
    first = lambda hit: jnp.min(jnp.where(hit, lane, float(LANES)), axis=-1, keepdims=True)
    in_g = lane < ng
    gl = jnp.where(in_g, logits, -jnp.inf)
    gmax = jnp.max(gl, axis=-1, keepdims=True)
    gidx = first(gl == gmax)
    p_g = 1.0 / jnp.sum(jnp.where(in_g, jnp.exp(logits - gmax), 0.0), axis=-1, keepdims=True)
    lo = ng + gidx * epg
    el = jnp.where((lane >= lo) & (lane < lo + epg), logits, -jnp.inf)
    v1 = jnp.max(el, axis=-1, keepdims=True)
    i1 = first(el == v1)
    el2 = jnp.where(lane == i1, -jnp.inf, el)
    v2 = jnp.max(el2, axis=-1, keepdims=True)
    i2 = first(el2 == v2)
    e21 = jnp.exp(v2 - v1)
    w1 = 1.0 / (1.0 + e21)
    w2 = e21 / (1.0 + e21)
    rt = jnp.where(lane == 0, i1 - ng, jnp.where(lane == 1, i2 - ng,
         jnp.where(lane == 2, p_g * w1, jnp.where(lane == 3, p_g * w2, 0.0))))
    rt_ref[...] = rt


def _merge(x3, oa, ob, u, mod, g_ffn, wua, wub, wo, wr, br, cfg, tm):
    bx, t, d = x3.shape
    m = bx * t
    off, _ = _layout(cfg)
    aw, bw = oa.shape[1], ob.shape[1]
    r = mod.shape[2]
    nt = t // tm
    const = lambda shape: pl.BlockSpec(shape, lambda b, i: (0,) * len(shape), pipeline_mode=pl.Buffered(1))
    return pl.pallas_call(
        functools.partial(_merge_kernel, cfg=cfg),
        out_shape=(jax.ShapeDtypeStruct((m, d), F32), jax.ShapeDtypeStruct((m, d), F32),
                   jax.ShapeDtypeStruct((m, LANES), F32)),
        grid=(bx, nt),
        in_specs=[pl.BlockSpec((1, tm, d), lambda b, i: (b, i, 0)),
                  pl.BlockSpec((tm, aw), lambda b, i: (b * nt + i, 0)),
                  pl.BlockSpec((tm, bw), lambda b, i: (b * nt + i, 0)),
                  pl.BlockSpec((tm, d), lambda b, i: (b * nt + i, off["ga"][0] // d)),
                  pl.BlockSpec((tm, d), lambda b, i: (b * nt + i, off["gb"][0] // d)),
                  pl.BlockSpec((1, 6, r, d), lambda b, i: (b, 0, 0, 0)),
                  const((1, d)), const(wua.shape), const(wub.shape), const(wo.shape), const((2, d, LANES)),
                  const((1, LANES))],
        out_specs=(pl.BlockSpec((tm, d), lambda b, i: (b * nt + i, 0)),
                   pl.BlockSpec((tm, d), lambda b, i: (b * nt + i, 0)),
                   pl.BlockSpec((tm, LANES), lambda b, i: (b * nt + i, 0))),
        compiler_params=_cparams(("arbitrary", "arbitrary")),
        name="merge_out_router",
    )(x3, oa, ob, u, u, mod, g_ffn.reshape(1, d), wua, wub, wo, wr, br)


def _moe_kernel(be_ref, nb_ref, tok_ref, tokn_ref, h_hbm, wg_ref, wu_ref, wd_ref, y_ref,
                xbuf, sem, wg_s, wu_s, wd_s, *, rows):
    i = pl.program_id(0)
    nb = nb_ref[0]

    def row_copy(tok, slot, r):
        return pltpu.make_async_copy(h_hbm.at[pl.ds(tok, 1)], xbuf.at[slot, pl.ds(r, 1)], sem.at[slot])

    def issue(idx_ref, slot):
        def body(r, c):
            row_copy(idx_ref[0, 0, r], slot, r).start()
            return c

        lax.fori_loop(0, rows, body, 0, unroll=8)

    @pl.when(i == 0)
    def _prime():
        issue(tok_ref, 0)

    @pl.when(i + 1 < nb)
    def _prefetch():
        issue(tokn_ref, (i + 1) % 2)

    @pl.when(i < nb)
    def _compute():
        slot = i % 2
        pltpu.make_async_copy(h_hbm.at[pl.ds(0, rows)], xbuf.at[slot], sem.at[slot]).wait()

        @pl.when((i == 0) | (be_ref[i] != be_ref[jnp.maximum(i - 1, 0)]))
        def _cast():
            wg_s[...] = wg_ref[...].astype(BF16)
            wu_s[...] = wu_ref[...].astype(BF16)
            wd_s[...] = wd_ref[...].astype(BF16)

        x = xbuf[slot].astype(BF16)
        gate = jnp.dot(x, wg_s[...], preferred_element_type=F32)
        up = jnp.dot(x, wu_s[...], preferred_element_type=F32)
        act = (gate * (1.0 / (1.0 + jnp.exp(-gate))) * up).astype(BF16)
        y_ref[...] = jnp.dot(act, wd_s[...], preferred_element_type=F32)

    @pl.when(i >= nb)
    def _idle():
        y_ref[...] = jnp.zeros(y_ref.shape, F32)


def _moe_experts(h2_all, slot_token, block_expert, n_used, w_gate_e, w_up_e, w_down_e, rows):
    n_blocks = block_expert.shape[0]
    e, d, f = w_gate_e.shape
    tok3 = slot_token.reshape(n_blocks, 1, rows)
    grid_spec = pltpu.PrefetchScalarGridSpec(
        num_scalar_prefetch=2,
        grid=(n_blocks,),
        in_specs=[pl.BlockSpec((1, 1, rows), lambda i, be, nb: (i, 0, 0), memory_space=pltpu.SMEM),
                  pl.BlockSpec((1, 1, rows), lambda i, be, nb: (jnp.minimum(i + 1, n_blocks - 1), 0, 0),
                               memory_space=pltpu.SMEM),
                  pl.BlockSpec(memory_space=pl.ANY),
                  pl.BlockSpec((None, d, f), lambda i, be, nb: (be[i], 0, 0)),
                  pl.BlockSpec((None, d, f), lambda i, be, nb: (be[i], 0, 0)),
                  pl.BlockSpec((None, f, d), lambda i, be, nb: (be[i], 0, 0))],
        out_specs=pl.BlockSpec((rows, d), lambda i, be, nb: (i, 0)),
        scratch_shapes=[pltpu.VMEM((2, rows, d), F32), pltpu.SemaphoreType.DMA((2,)),
                        pltpu.VMEM((d, f), BF16), pltpu.VMEM((d, f), BF16), pltpu.VMEM((f, d), BF16)],
    )
    return pl.pallas_call(
        functools.partial(_moe_kernel, rows=rows),
        out_shape=jax.ShapeDtypeStruct((n_blocks * rows, d), F32),
        grid_spec=grid_spec,
        compiler_params=_cparams(("arbitrary",)),
        name="moe_experts",
    )(block_expert, n_used, tok3, tok3, h2_all, w_gate_e, w_up_e, w_down_e)


def _rank_kernel(rt_ref, base_ref, dest_ref, tot_ref, base_s, *, tm):
    @pl.when(pl.program_id(0) == 0)
    def _init():
        base_s[...] = base_ref[...]

    rt = rt_ref[...]
    lane = lax.broadcasted_iota(I32, rt.shape, 1).astype(F32)
    o1 = (lane == rt[:, 0:1]).astype(F32)
    o2 = (lane == rt[:, 1:2]).astype(F32)
    lower = _strict_lower_ones(tm)
    p1 = jnp.dot(lower, o1.astype(BF16), preferred_element_type=F32)
    p2 = jnp.dot(lower, o2.astype(BF16), preferred_element_type=F32)
    c1 = jnp.sum(o1, axis=0, keepdims=True)
    c2 = jnp.sum(o2, axis=0, keepdims=True)
    base = base_s[...]
    d1 = jnp.sum((base + p1) * o1, axis=-1, keepdims=True)
    d2 = jnp.sum((base + c1 + p2) * o2, axis=-1, keepdims=True)
    dest_ref[...] = jnp.where(lane == 0, d1, jnp.where(lane == 1, d2, 0.0))
    base_s[...] = base + c1 + c2
    tot_ref[...] = base + c1 + c2


def _rank(rt_pad, base, tm):
    n = rt_pad.shape[0]
    return pl.pallas_call(
        functools.partial(_rank_kernel, tm=tm),
        out_shape=(jax.ShapeDtypeStruct((n, LANES), F32), jax.ShapeDtypeStruct((1, LANES), F32)),
        grid=(n // tm,),
        in_specs=[pl.BlockSpec((tm, LANES), lambda i: (i, 0)), pl.BlockSpec((1, LANES), lambda i: (0, 0))],
        out_specs=(pl.BlockSpec((tm, LANES), lambda i: (i, 0)), pl.BlockSpec((1, LANES), lambda i: (0, 0))),
        scratch_shapes=[pltpu.VMEM((1, LANES), F32)],
        compiler_params=_cparams(("arbitrary",)),
        name="moe_rank",
    )(rt_pad, base)


def _moe_plan(rt_all, n_experts, rows, top_k, tm):
    n_tok = rt_all.shape[0]
    a = n_tok * top_k
    assert top_k == 2 and n_experts <= LANES
    n_blocks = -(-(a + n_experts * (rows - 1)) // rows)
    n_pad = _round_up(n_tok, tm)
    rt_pad = jnp.concatenate([rt_all, jnp.full((n_pad - n_tok, LANES), -1.0, F32)], axis=0)
    _, tot = _rank(rt_pad, jnp.zeros((1, LANES), F32), tm)
    counts = tot[0, :n_experts].astype(I32)
    padded = (counts + rows - 1) // rows * rows
    cum = jnp.cumsum(padded)
    pstart = jnp.zeros((1, LANES), F32).at[0, :n_experts].set((cum - padded).astype(F32))
    dest, _ = _rank(rt_pad, pstart, tm)
    pos = dest[:n_tok, 0:top_k].astype(I32).reshape(-1)
    slot_token = jnp.zeros((n_blocks * rows,), I32).at[pos].set(jnp.arange(a, dtype=I32) // top_k)
    starts = jnp.arange(n_blocks, dtype=I32) * rows
    block_expert = jnp.minimum(jnp.sum((cum[None, :] <= starts[:, None]).astype(I32), axis=1), n_experts - 1)
    n_used = (cum[-1] // rows).astype(I32).reshape(1)
    return slot_token, pos, block_expert, n_used


def _combine_kernel(pos_ref, posn_ref, x1_ref, rt_ref, m_ref, gfin_ref, ys_hbm, o_ref, ybuf, sem, *, tm):
    t = pl.program_id(0)
    nt = pl.num_programs(0)

    def row_copy(p, slot, r):
        return pltpu.make_async_copy(ys_hbm.at[pl.ds(p, 1)], ybuf.at[slot, pl.ds(r, 1)], sem.at[slot])

    def issue(idx_ref, slot):
        def body(r, c):
            row_copy(idx_ref[0, 0, r], slot, r).start()
            return c

        lax.fori_loop(0, 2 * tm, body, 0, unroll=8)

    @pl.when(t == 0)
    def _prime():
        issue(pos_ref, 0)

    @pl.when(t + 1 < nt)
    def _prefetch():
        issue(posn_ref, (t + 1) % 2)

    slot = t % 2
    pltpu.make_async_copy(ys_hbm.at[pl.ds(0, 2 * tm)], ybuf.at[slot], sem.at[slot]).wait()

    rt = rt_ref[...]
    moe = rt[:, 2:3] * ybuf[slot, 0:tm, :] + rt[:, 3:4] * ybuf[slot, tm:2 * tm, :]
    x2 = x1_ref[...] + m_ref[0, 5] * moe
    y = x2 * lax.rsqrt(jnp.mean(x2 * x2, axis=-1, keepdims=True) + RMS_EPS)
    o_ref[...] = y * gfin_ref[...]


def _combine(x1, rt, mod, g_final, ys, pos, t_per_b, tm):
    m, d = x1.shape
    nt_all = m // tm
    nt_b = t_per_b // tm
    r = mod.shape[2]
    pos3 = pos.reshape(nt_all, tm, 2).transpose(0, 2, 1).reshape(nt_all, 1, 2 * tm)
    return pl.pallas_call(
        functools.partial(_combine_kernel, tm=tm),
        out_shape=jax.ShapeDtypeStruct((m, d), F32),
        grid=(nt_all,),
        in_specs=[pl.BlockSpec((1, 1, 2 * tm), lambda t: (t, 0, 0), memory_space=pltpu.SMEM),
                  pl.BlockSpec((1, 1, 2 * tm), lambda t: (jnp.minimum(t + 1, nt_all - 1), 0, 0),
                               memory_space=pltpu.SMEM),
                  pl.BlockSpec((tm, d), lambda t: (t, 0)),
                  pl.BlockSpec((tm, LANES), lambda t: (t, 0)),
                  pl.BlockSpec((1, 6, r, d), lambda t: (t // nt_b, 0, 0, 0)),
                  pl.BlockSpec((1, d), lambda t: (0, 0)),
                  pl.BlockSpec(memory_space=pl.ANY)],
        out_specs=pl.BlockSpec((tm, d), lambda t: (t, 0)),
        scratch_shapes=[pltpu.VMEM((2, 2 * tm, d), F32), pltpu.SemaphoreType.DMA((2,))],
        compiler_params=_cparams(("arbitrary",)),
        name="moe_combine",
    )(pos3, pos3, x1, rt, mod, g_final.reshape(1, d), ys)


def _forward(cfg, x_prompt, x_sample, cache_a_k, cache_a_v, cache_idx_k, cache_b_k, cache_b_v,
             page_table, c_prompt, c_sample, w_ada, b_ada, g_mix, g_ffn, g_final, w_in,
             w_up_a, w_up_b, w_out, w_group, b_group, w_router, b_router, w_gate_e, w_up_e, w_down_e):
    depth = w_in.shape[0]
    bp, seq, d = x_prompt.shape
    db, dseq, _ = x_sample.shape
    assert dseq == 1, "sample path is written for single-token decode"
    hd = cfg.head_dim
    off, wpad = _layout(cfg)
    ps = cfg.page_size
    ng, epg = cfg.n_groups, cfg.epg
    n_exp = ng * epg
    col = lambda a, name: a[..., off[name][0]:off[name][0] + off[name][1]]

    xp, xs = x_prompt, x_sample.reshape(1, db, d)
    c_all = jnp.concatenate([c_prompt, c_sample, jnp.zeros((-(bp + db) % 8, d), F32)], axis=0)
    rows_p, rows_s = [], []
    for layer in range(depth):
        mod = _adaln(c_all, w_ada[layer], b_ada[layer])
        mod_p = mod[:bp].reshape(bp, 6, 1, d)
        mod_s = mod[bp:bp + db].reshape(db, 6, d).transpose(1, 0, 2).reshape(1, 6, db, d)
        w_al = _permute_w_in(w_in[layer], cfg)
        def parts(w32):
            hi = w32.astype(BF16)
            return jnp.stack([hi, (w32 - hi.astype(F32)).astype(BF16)])

        wua, wub, wo = parts(w_up_a[layer]), parts(w_up_b[layer]), parts(w_out[layer])
        wr = parts(jnp.concatenate([w_group[layer], w_router[layer].transpose(1, 0, 2).reshape(d, n_exp),
                                    jnp.zeros((d, LANES - ng - n_exp), F32)], axis=1))
        br = jnp.concatenate([b_group[layer], b_router[layer].reshape(-1),
                              jnp.zeros((LANES - ng - n_exp,), F32)]).reshape(1, LANES)

        ts_p = min(256, seq)
        h_p = _rms_mod(xp, g_mix[layer], mod_p, ts_p).reshape(bp * seq, d)
        u_p = _matmul(h_p, w_al, min(cfg.tm_in, bp * seq), cfg.tn_in)
        u_p3 = u_p.reshape(bp, seq, wpad)
        oa_p = _dsa_prompt(u_p3, cfg).reshape(bp * seq, -1)
        ob_p = _sb_prompt(u_p3, cfg).reshape(bp * seq, -1)

        h_s = _rms_mod(xs, g_mix[layer], mod_s, db, out_dtype=F32).reshape(db, d)
        u_s = _matmul(h_s, w_al, db, cfg.tn_in)
        qi3 = col(u_s, "qi").reshape(db, cfg.n_idx_heads, cfg.idx_dim)
        kiwi_s = col(u_s, "kiwi")
        wi3 = (kiwi_s[:, cfg.idx_dim:cfg.idx_dim + cfg.n_idx_heads]
               * (float(cfg.n_idx_heads * cfg.idx_dim) ** -0.5)).reshape(db, cfg.n_idx_heads, 1)
        c_ik = cache_idx_k[layer]
        c_ak = cache_a_k[layer].reshape(cache_a_k.shape[1], ps, hd)
        c_av = cache_a_v[layer].reshape(cache_a_v.shape[1], ps, hd)
        c_bk, c_bv = cache_b_k[layer], cache_b_v[layer]
        sc4 = _dsa_sample_scores(qi3, wi3, c_ik, page_table, cfg)
        bias, bias_new = _dsa_sample_select(sc4.reshape(db, -1), u_s, cfg)
        qa3 = col(u_s, "qa").reshape(db, cfg.n_heads_a, hd)
        oa_s = _dsa_sample_attend(qa3, bias, bias_new, u_s.reshape(db, 1, wpad), c_ak, c_av,
                                  page_table, cfg).reshape(db, -1)
        rep = cfg.n_heads_b // cfg.n_kv_b
        qb3 = col(u_s, "qb").reshape(db, cfg.n_heads_b, 1, hd)
        gsel = (jnp.arange(cfg.n_heads_b)[:, None] // rep == jnp.arange(cfg.n_kv_b)[None, :])
        q_bd = (qb3 * gsel[None, :, :, None].astype(F32)).reshape(db, cfg.n_heads_b, cfg.n_kv_b * hd)
        ob_s = _sb_sample(q_bd, c_bk, c_bv, page_table, cfg).reshape(db, -1)

        x1_p, h2_p, rt_p = _merge(xp, oa_p, ob_p, u_p, mod_p, g_ffn[layer], wua[:1], wub[:1], wo[:1], wr, br, cfg,
                                  min(cfg.tm_merge, seq))
        x1_s, h2_s, rt_s = _merge(xs, oa_s, ob_s, u_s, mod_s, g_ffn[layer], wua, wub, wo, wr, br, cfg, db)

        h2_all = jnp.concatenate([h2_p, h2_s], axis=0)
        rt_all = jnp.concatenate([rt_p, rt_s], axis=0)
        slot_token, pos, block_expert, n_used = _moe_plan(rt_all, n_exp, cfg.moe_rows, cfg.top_k,
                                                          min(256, bp * seq))
        ys = _moe_experts(h2_all, slot_token, block_expert, n_used,
                          w_gate_e[layer], w_up_e[layer], w_down_e[layer], cfg.moe_rows)
        last = layer == depth - 1
        gfin = g_final if last else jnp.ones_like(g_final)
        assert last, "deeper stacks need an un-normalised combine for inner layers"
        npos = bp * seq * cfg.top_k
        y_p = _combine(x1_p, rt_p, mod_p, gfin, ys, pos[:npos], seq, min(cfg.tm_comb, seq))
        y_s = _combine(x1_s, rt_s, mod_s, gfin, ys, pos[npos:], db, db)

        pages = lambda a: a.reshape((bp, seq // ps, ps) + a.shape[2:])
        rows_p.append((pages(col(u_p3, "ka").reshape(bp, seq, 1, hd)),
                       pages(col(u_p3, "va").reshape(bp, seq, 1, hd)),
                       pages(col(u_p3, "kiwi")[..., :cfg.idx_dim]),
                       pages(col(u_p3, "kb").reshape(bp, seq, cfg.n_kv_b, hd)),
                       pages(col(u_p3, "vb").reshape(bp, seq, cfg.n_kv_b, hd))))
        rows_s.append((col(u_s, "ka").reshape(db, 1, 1, hd), col(u_s, "va").reshape(db, 1, 1, hd),
                       kiwi_s[:, :cfg.idx_dim].reshape(db, 1, cfg.idx_dim),
                       col(u_s, "kb").reshape(db, 1, cfg.n_kv_b, hd),
                       col(u_s, "vb").reshape(db, 1, cfg.n_kv_b, hd)))
        xp, xs = y_p.reshape(bp, seq, d), y_s.reshape(1, db, d)

    stack = lambda rows: tuple(jnp.stack(r) for r in zip(*rows))
    return (xp, xs.reshape(db, 1, d)) + stack(rows_p) + stack(rows_s)


def kernel(x_prompt, x_sample, cache_a_k, cache_a_v, cache_idx_k, cache_b_k, cache_b_v, page_table,
           c_prompt, c_sample, w_ada, b_ada, g_mix, g_ffn, g_final, w_in, w_up_a, w_up_b, w_out,
           w_group, b_group, w_router, b_router, w_gate_e, w_up_e, w_down_e):
    return _forward(Cfg(), x_prompt, x_sample, cache_a_k, cache_a_v, cache_idx_k, cache_b_k, cache_b_v,
                    page_table, c_prompt, c_sample, w_ada, b_ada, g_mix, g_ffn, g_final, w_in,
                    w_up_a, w_up_b, w_out, w_group, b_group, w_router, b_router, w_gate_e, w_up_e, w_down_e)
```

```python
import functools
from typing import NamedTuple

import numpy as np
import jax
import jax.numpy as jnp
from jax import lax
from jax.experimental import pallas as pl
from jax.experimental.pallas import tpu as pltpu

F32 = jnp.float32
BF16 = jnp.bfloat16
I32 = jnp.int32

LANES = 128
INT_MIN = -(2 ** 31)
RMS_EPS = 1e-6
NEG_BIG = -1e30
SB_DEAD_TAIL = -110.0
VMEM_LIMIT = 48 * 1024 * 1024


class Cfg(NamedTuple):
    d_model: int = 2048
    head_dim: int = 128
    n_heads_a: int = 8
    n_idx_heads: int = 16
    idx_dim: int = 64
    n_heads_b: int = 8
    n_kv_b: int = 4
    topk_max: int = 256
    n_groups: int = 8
    epg: int = 8
    top_k: int = 2
    d_ff: int = 512
    page_size: int = 128
    tn_in: int = 512
    tm_in: int = 1024
    tq_a: int = 256
    tk_a: int = 256
    t_b: int = 256
    tm_merge: int = 256
    moe_rows: int = 128
    tm_comb: int = 128
    pages_per_step: int = 16
    sb_pages_per_chunk: int = 4


def _cparams(sem):
    return pltpu.CompilerParams(dimension_semantics=sem, vmem_limit_bytes=VMEM_LIMIT)


def _round_up(a, b):
    return -(-a // b) * b


def _layout(cfg):
    d, hd = cfg.d_model, cfg.head_dim
    segs = [("ga", d), ("gb", d), ("qa", cfg.n_heads_a * hd), ("qi", cfg.n_idx_heads * cfg.idx_dim),
            ("qb", cfg.n_heads_b * hd), ("kb", cfg.n_kv_b * hd), ("vb", cfg.n_kv_b * hd),
            ("ka", hd), ("va", hd), ("kiwi", LANES)]
    off, o = {}, 0
    for name, w in segs:
        off[name] = (o, w)
        o += w
    return off, _round_up(o, cfg.tn_in)


def _permute_w_in(w_in, cfg):
    hd = cfg.head_dim
    sizes = [cfg.n_heads_a * hd, hd, hd, cfg.n_idx_heads * cfg.idx_dim, cfg.idx_dim, cfg.n_idx_heads,
             cfg.n_heads_b * hd, cfg.n_kv_b * hd, cfg.n_kv_b * hd, cfg.d_model, cfg.d_model]
    names = ["qa", "ka", "va", "qi", "ki", "wi", "qb", "kb", "vb", "ga", "gb"]
    cuts = np.concatenate([[0], np.cumsum(sizes)])
    src = {n: w_in[:, int(cuts[i]):int(cuts[i + 1])] for i, n in enumerate(names)}
    off, width = _layout(cfg)
    assert cfg.idx_dim + cfg.n_idx_heads <= LANES
    kiwi = jnp.concatenate(
        [src["ki"], src["wi"], jnp.zeros((w_in.shape[0], LANES - cfg.idx_dim - cfg.n_idx_heads), w_in.dtype)], axis=1)
    parts = [src["ga"], src["gb"], src["qa"], src["qi"], src["qb"], src["kb"], src["vb"], src["ka"], src["va"], kiwi]
    used = sum(p.shape[1] for p in parts)
    if width > used:
        parts.append(jnp.zeros((w_in.shape[0], width - used), w_in.dtype))
    return jnp.concatenate(parts, axis=1)


def _split(x):
    hi = x.astype(BF16)
    return hi, (x - hi.astype(F32)).astype(BF16)


def _dot3(a, b, dot=None):
    dot = dot or (lambda x, y: jnp.dot(x, y, preferred_element_type=F32))
    (ah, al), (bh, bl) = _split(a), _split(b)
    return dot(ah, bh) + dot(al, bh) + dot(ah, bl)


def _adaln_kernel(c_ref, w_ref, b_ref, o_ref):
    c = c_ref[...]
    o_ref[...] = _dot3(c * (1.0 / (1.0 + jnp.exp(-c))), w_ref[...]) + b_ref[...]


def _adaln(c_all, w_ada, b_ada):
    mc, d = c_all.shape
    n = w_ada.shape[1]
    tn = 1024 if n % 1024 == 0 else n
    return pl.pallas_call(
        _adaln_kernel,
        out_shape=jax.ShapeDtypeStruct((mc, n), F32),
        grid=(n // tn,),
        in_specs=[pl.BlockSpec((mc, d), lambda j: (0, 0)),
                  pl.BlockSpec((d, tn), lambda j: (0, j)),
                  pl.BlockSpec((1, tn), lambda j: (0, j))],
        out_specs=pl.BlockSpec((mc, tn), lambda j: (0, j)),
        compiler_params=_cparams(("arbitrary",)),
        name="adaln",
    )(c_all, w_ada, b_ada.reshape(1, n))


def _rms_mod_kernel(x_ref, g_ref, m_ref, o_ref):
    x = x_ref[0]
    y = x * lax.rsqrt(jnp.mean(x * x, axis=-1, keepdims=True) + RMS_EPS) * g_ref[...]
    o_ref[0] = (y * (1.0 + m_ref[0, 1]) + m_ref[0, 0]).astype(o_ref.dtype)


def _rms_mod(x3, g, mod, ts, out_dtype=None):
    bx, t, d = x3.shape
    r = mod.shape[2]
    out_dtype = out_dtype or BF16
    return pl.pallas_call(
        _rms_mod_kernel,
        out_shape=jax.ShapeDtypeStruct((bx, t, d), out_dtype),
        grid=(bx, t // ts),
        in_specs=[pl.BlockSpec((1, ts, d), lambda b, i: (b, i, 0)),
                  pl.BlockSpec((1, d), lambda b, i: (0, 0)),
                  pl.BlockSpec((1, 6, r, d), lambda b, i: (b, 0, 0, 0))],
        out_specs=pl.BlockSpec((1, ts, d), lambda b, i: (b, i, 0)),
        compiler_params=_cparams(("arbitrary", "arbitrary")),
        name="rms_mod",
    )(x3, g.reshape(1, d), mod)


def _mm_kernel(a_ref, w_ref, o_ref, wbf_ref):
    @pl.when(pl.program_id(1) == 0)
    def _():
        wbf_ref[...] = w_ref[...].astype(BF16)

    o_ref[...] = jnp.dot(a_ref[...], wbf_ref[...], preferred_element_type=F32).astype(o_ref.dtype)


def _mm3_kernel(a_ref, w_ref, o_ref, wbf_ref):
    del wbf_ref
    o_ref[...] = _dot3(a_ref[...], w_ref[...]).astype(o_ref.dtype)


def _matmul(a, w, tm, tn, out_dtype=F32):
    m, k = a.shape
    n = w.shape[1]
    return pl.pallas_call(
        _mm_kernel if a.dtype == BF16 else _mm3_kernel,
        out_shape=jax.ShapeDtypeStruct((m, n), out_dtype),
        grid=(n // tn, m // tm),
        in_specs=[pl.BlockSpec((tm, k), lambda j, i: (i, 0)),
                  pl.BlockSpec((k, tn), lambda j, i: (0, j))],
        out_specs=pl.BlockSpec((tm, tn), lambda j, i: (i, j)),
        scratch_shapes=[pltpu.VMEM((k, tn), BF16)],
        compiler_params=_cparams(("arbitrary", "arbitrary")),
        name="in_proj",
    )(a, w)


def _ordered_bits_to_float(u):
    key = u ^ jnp.int32(INT_MIN)
    return pltpu.bitcast(jnp.where(key < 0, key ^ jnp.int32(0x7FFFFFFF), key), F32)


def _kth_largest(count_ge, shape, k):
    def body(bi, res):
        cand = res | lax.shift_left(jnp.int32(1), jnp.int32(31) - bi)
        return jnp.where(count_ge(_ordered_bits_to_float(cand)) >= k, cand, res)

    thr = _ordered_bits_to_float(lax.fori_loop(0, 32, body, jnp.zeros(shape, I32)))
    return jnp.where(thr != thr, -jnp.inf, thr)


def _tie_cutoff(count_tie_lt, shape, need, nbits):
    def body(bi, x):
        cand = x | lax.shift_left(jnp.int32(1), jnp.int32(nbits - 1) - bi)
        return jnp.where(count_tie_lt(cand) < need, cand, x)

    return lax.fori_loop(0, nbits, body, jnp.zeros(shape, I32))


def _fold_lanes(x):
    acc = x[:, 0:LANES]
    for j in range(1, x.shape[1] // LANES):
        acc = acc + x[:, j * LANES:(j + 1) * LANES]
    return acc


def _dot_nt(a, b):
    return lax.dot_general(a, b, (((1,), (1,)), ((), ())), preferred_element_type=F32)


def _dsa_prompt_kernel(qa_ref, qi_ref, kwq_ref, kw_ref, ka_ref, va_ref, o_ref,
                       ka_s, kb_s, kbf_s, vt_s, sc_s, dist_s, xcut_s, qbf_s, m_s, l_s, acc_s, lg_s, p_s, al_s,
                       *, cfg, tq, tk, n_sel):
    hd, idim, ni, ha = cfg.head_dim, cfg.idx_dim, cfg.n_idx_heads, cfg.n_heads_a
    s_total = sc_s.shape[0]
    i = pl.program_id(1)
    t0 = i * tq
    nk = (t0 + tq + tk - 1) // tk

    @pl.when(i == 0)
    def _stage():
        kw = kw_ref[0]
        lane = lax.broadcasted_iota(I32, kw.shape, 1)
        k_lo = jnp.where(lane < idim, kw, 0.0)
        ka_s[...] = k_lo.astype(BF16)
        kb_s[...] = pltpu.roll(k_lo, idim, 1).astype(BF16)
        kbf_s[...] = ka_ref[0].astype(BF16)
        for c in range(s_total // tk):
            vt_s[:, c * tk:(c + 1) * tk] = va_ref[0, c * tk:(c + 1) * tk, :].T.astype(BF16)

    w_t = (kwq_ref[0] * (float(ni * idim) ** -0.5)).T
    qpos = t0 + lax.broadcasted_iota(I32, (tk, tq), 1)
    krow = lax.broadcasted_iota(I32, (tk, tq), 0)

    def score_chunk(c, carry):
        ks = pl.multiple_of(c * tk, tk)
        k_even = ka_s[pl.ds(ks, tk), :]
        k_odd = kb_s[pl.ds(ks, tk), :]
        acc = jnp.zeros((tk, tq), F32)
        for p in range(ni // 2):
            qp = qi_ref[0, :, p * LANES:(p + 1) * LANES].astype(BF16)
            for par, kk in ((0, k_even), (1, k_odd)):
                h = 2 * p + par
                acc = acc + w_t[idim + h:idim + h + 1, :] * jnp.maximum(_dot_nt(kk, qp), 0.0)
        sc_s[pl.ds(ks, tk), :] = jnp.where(ks + krow <= qpos, acc, -jnp.inf)
        return carry

    lax.fori_loop(0, nk, score_chunk, 0)

    def count(pred):
        def body(c, cnt):
            ks = pl.multiple_of(c * tk, tk)
            hit = pred(sc_s[pl.ds(ks, tk), :], ks + krow).astype(I32)
            return cnt + jnp.sum(hit.reshape(tk // 8, 8, tq), axis=0)

        cnt = lax.fori_loop(0, nk, body, jnp.zeros((8, tq), I32))
        return jnp.sum(cnt, axis=0, keepdims=True)

    thr = _kth_largest(lambda t: count(lambda s, k: s >= t), (1, tq), n_sel)

    xcut_s[...] = jnp.full(xcut_s.shape, s_total, I32)

    @pl.when(jnp.max(count(lambda s, k: s >= thr)) > n_sel)
    def _ties():
        need = n_sel - count(lambda s, k: s > thr)
        x = _tie_cutoff(lambda cand: count(lambda s, k: (s == thr) & (k < cand)), (1, tq), need,
                        max(1, (s_total - 1).bit_length()))
        xcut_s[...] = jnp.broadcast_to(x, xcut_s.shape)

    xcut = xcut_s[0:1, :]

    def mask_chunk(c, carry):
        ks = pl.multiple_of(c * tk, tk)
        kpos = ks + krow
        s = sc_s[pl.ds(ks, tk), :]
        sel = ((s > thr) | ((s == thr) & (kpos <= xcut))) & (kpos <= qpos)
        dist_s[pl.ds(ks, tk), :] = jnp.where(sel, (qpos - kpos).astype(F32), -NEG_BIG)
        return carry

    lax.fori_loop(0, nk, mask_chunk, 0)
    scale = float(hd) ** -0.5

    for h in range(ha):
        qbf_s[h] = qa_ref[0, :, h * hd:(h + 1) * hd].astype(BF16)
    m_s[...] = jnp.full(m_s.shape, NEG_BIG, F32)
    l_s[...] = jnp.zeros(l_s.shape, F32)
    acc_s[...] = jnp.zeros(acc_s.shape, F32)

    def att_chunk(c, carry):
        ks = pl.multiple_of(c * tk, tk)
        kc = kbf_s[pl.ds(ks, tk), :]
        vt = vt_s[:, pl.ds(ks, tk)]
        dist = dist_s[pl.ds(ks, tk), :]
        for h in range(ha):
            lg_s[h] = _dot_nt(kc, qbf_s[h])
        for h in range(ha):
            slope = float(2.0 ** (-8.0 * (h + 1) / ha))
            lg = lg_s[h] * scale - slope * dist
            m = m_s[h]
            m_new = jnp.maximum(m, jnp.max(lg, axis=0, keepdims=True))
            p = jnp.exp(lg - m_new)
            alpha = jnp.exp(m - m_new)
            l_s[h] = alpha * l_s[h] + jnp.sum(p, axis=0, keepdims=True)
            p_s[h] = p.astype(BF16)
            al_s[h] = alpha
            m_s[h] = m_new
        for h in range(ha):
            acc_s[h] = al_s[h] * acc_s[h] + jnp.dot(vt, p_s[h], preferred_element_type=F32)
        return carry

    lax.fori_loop(0, nk, att_chunk, 0)
    for h in range(ha):
        o_ref[0, :, h * hd:(h + 1) * hd] = (acc_s[h] / l_s[h]).T.astype(o_ref.dtype)


def _dsa_prompt(u3, cfg):
    b, s, _ = u3.shape
    off, _ = _layout(cfg)
    tq, tk = min(cfg.tq_a, s), min(cfg.tk_a, s)
    n_sel = min(cfg.topk_max, s // 4)
    aw = cfg.n_heads_a * cfg.head_dim
    iw = cfg.n_idx_heads * cfg.idx_dim
    assert 2 * cfg.idx_dim == LANES and s % tk == 0 and s % tq == 0 and tk >= n_sel
    cb = lambda name, w: off[name][0] // w
    kern = functools.partial(_dsa_prompt_kernel, cfg=cfg, tq=tq, tk=tk, n_sel=n_sel)
    return pl.pallas_call(
        kern,
        out_shape=jax.ShapeDtypeStruct((b, s, aw), BF16),
        grid=(b, s // tq),
        in_specs=[pl.BlockSpec((1, tq, aw), lambda bi, i: (bi, i, cb("qa", aw))),
                  pl.BlockSpec((1, tq, iw), lambda bi, i: (bi, i, cb("qi", iw))),
                  pl.BlockSpec((1, tq, LANES), lambda bi, i: (bi, i, cb("kiwi", LANES))),
                  pl.BlockSpec((1, s, LANES), lambda bi, i: (bi, 0, cb("kiwi", LANES))),
                  pl.BlockSpec((1, s, LANES), lambda bi, i: (bi, 0, cb("ka", LANES))),
                  pl.BlockSpec((1, s, LANES), lambda bi, i: (bi, 0, cb("va", LANES)))],
        out_specs=pl.BlockSpec((1, tq, aw), lambda bi, i: (bi, i, 0)),
        scratch_shapes=[pltpu.VMEM((s, LANES), BF16), pltpu.VMEM((s, LANES), BF16),
                        pltpu.VMEM((s, LANES), BF16), pltpu.VMEM((cfg.head_dim, s), BF16),
                        pltpu.VMEM((s, tq), F32), pltpu.VMEM((s, tq), F32), pltpu.VMEM((8, tq), I32),
                        pltpu.VMEM((cfg.n_heads_a, tq, cfg.head_dim), BF16),
                        pltpu.VMEM((cfg.n_heads_a, 1, tq), F32), pltpu.VMEM((cfg.n_heads_a, 1, tq), F32),
                        pltpu.VMEM((cfg.n_heads_a, cfg.head_dim, tq), F32),
                        pltpu.VMEM((cfg.n_heads_a, tk, tq), F32), pltpu.VMEM((cfg.n_heads_a, tk, tq), BF16),
                        pltpu.VMEM((cfg.n_heads_a, 1, tq), F32)],
        compiler_params=_cparams(("arbitrary", "arbitrary")),
        name="dsa_prompt",
    )(u3, u3, u3, u3, u3, u3)


def _sb_terms(q, kc, scale):
    z = _dot_nt(q, kc) * scale
    lk = -(jnp.maximum(z, 0.0) + jnp.log(1.0 + jnp.exp(-jnp.abs(z))))
    return z, lk


def _suffix_in_chunk(lk, upper):
    hi = lk.astype(BF16)
    lo = (lk - hi.astype(F32)).astype(BF16)
    return (jnp.dot(hi, upper, preferred_element_type=F32) + jnp.dot(lo, upper, preferred_element_type=F32))


def _strict_lower_ones(n):
    return (lax.broadcasted_iota(I32, (n, n), 0) > lax.broadcasted_iota(I32, (n, n), 1)).astype(BF16)


def _sb_prompt_kernel(q_ref, k_ref, v_ref, o_ref, kbf_s, vbf_s, *, t, hd, rep):
    i = pl.program_id(2)

    @pl.when(i == 0)
    def _stage():
        kbf_s[...] = k_ref[0].astype(BF16)
        vbf_s[...] = v_ref[0].astype(BF16)

    qs = [q_ref[0, :, r * hd:(r + 1) * hd].astype(BF16) for r in range(rep)]
    scale = float(hd) ** -0.5
    upper = _strict_lower_ones(t)
    before = lax.broadcasted_iota(I32, (t, t), 1) < lax.broadcasted_iota(I32, (t, t), 0)

    ks = pl.multiple_of(i * t, t)
    kc, vc = kbf_s[pl.ds(ks, t), :], vbf_s[pl.ds(ks, t), :]
    state = []
    for q in qs:
        z, lk = _sb_terms(q, kc, scale)
        lkm = jnp.where(before, lk, 0.0)
        tail = _suffix_in_chunk(lkm, upper)
        w = jnp.where(before, jnp.exp(z + lk + tail), 0.0)
        state.append(jnp.sum(lkm, axis=-1, keepdims=True))
        state.append(jnp.dot(w.astype(BF16), vc, preferred_element_type=F32))

    def live(st):
        top = st[0]
        for r in range(1, rep):
            top = jnp.maximum(top, st[2 * r])
        return jnp.max(top) > SB_DEAD_TAIL

    def body(loop):
        j, _, st = loop
        ks = pl.multiple_of((i - 1 - j) * t, t)
        kc, vc = kbf_s[pl.ds(ks, t), :], vbf_s[pl.ds(ks, t), :]
        out = []
        for r, q in enumerate(qs):
            carry, acc = st[2 * r], st[2 * r + 1]
            z, lk = _sb_terms(q, kc, scale)
            tail = carry + _suffix_in_chunk(lk, upper)
            w = jnp.exp(z + lk + tail)
            out.append(carry + jnp.sum(lk, axis=-1, keepdims=True))
            out.append(acc + jnp.dot(w.astype(BF16), vc, preferred_element_type=F32))
        return j + 1, live(out), tuple(out)

    _, _, st = lax.while_loop(lambda loop: (loop[0] < i) & loop[1], body, (jnp.int32(0), live(state), tuple(state)))
    for r in range(rep):
        o_ref[0, :, r * hd:(r + 1) * hd] = st[2 * r + 1].astype(o_ref.dtype)


def _sb_prompt(u3, cfg):
    b, s, _ = u3.shape
    off, _ = _layout(cfg)
    hd, hb, g = cfg.head_dim, cfg.n_heads_b, cfg.n_kv_b
    rep = hb // g
    t = min(cfg.t_b, s)
    assert s % t == 0
    qw = rep * hd
    assert off["qb"][0] % qw == 0
    qb0, kb0, vb0 = off["qb"][0] // qw, off["kb"][0] // hd, off["vb"][0] // hd
    kern = functools.partial(_sb_prompt_kernel, t=t, hd=hd, rep=rep)
    return pl.pallas_call(
        kern,
        out_shape=jax.ShapeDtypeStruct((b, s, hb * hd), BF16),
        grid=(b, g, s // t),
        in_specs=[pl.BlockSpec((1, t, qw), lambda bi, gi, i: (bi, i, qb0 + gi)),
                  pl.BlockSpec((1, s, hd), lambda bi, gi, i: (bi, 0, kb0 + gi)),
                  pl.BlockSpec((1, s, hd), lambda bi, gi, i: (bi, 0, vb0 + gi))],
        out_specs=pl.BlockSpec((1, t, qw), lambda bi, gi, i: (bi, i, gi)),
        scratch_shapes=[pltpu.VMEM((s, hd), BF16), pltpu.VMEM((s, hd), BF16)],
        compiler_params=_cparams(("arbitrary", "arbitrary", "arbitrary")),
        name="sb_prompt",
    )(u3, u3, u3)


def _page_specs(n, block, page_of):
    zeros = (0,) * (len(block) - 1)
    return [pl.BlockSpec((None,) + tuple(block[1:]),
                         functools.partial(lambda b, j, pt, r: (page_of(b, j, r, pt),) + zeros, r=r))
            for r in range(n)]


def _dsa_sample_score_kernel(pt_ref, q_ref, w_ref, *rest, pg):
    k_refs, o_ref = rest[:pg], rest[pg]
    q = q_ref[0]
    w = w_ref[0]
    for r in range(pg):
        d = _dot3(q, k_refs[r][...], _dot_nt)
        o_ref[0, 0, r:r + 1, :] = jnp.sum(w * jnp.maximum(d, 0.0), axis=0, keepdims=True)


def _dsa_sample_scores(qi3, wi3, cache_ik, page_table, cfg):
    db, ni, idim = qi3.shape
    pool, ps, _ = cache_ik.shape
    npg = page_table.shape[1]
    pg = min(cfg.pages_per_step, npg)
    assert npg % pg == 0 and ps == LANES
    page_of = lambda b, j, r, pt: pt[b * npg + j * pg + r]
    grid_spec = pltpu.PrefetchScalarGridSpec(
        num_scalar_prefetch=1,
        grid=(db, npg // pg),
        in_specs=[pl.BlockSpec((1, ni, idim), lambda b, j, pt: (b, 0, 0)),
                  pl.BlockSpec((1, ni, 1), lambda b, j, pt: (b, 0, 0))]
                 + _page_specs(pg, (1, ps, idim), page_of),
        out_specs=pl.BlockSpec((1, 1, pg, ps), lambda b, j, pt: (b, j, 0, 0)),
    )
    out = pl.pallas_call(
        functools.partial(_dsa_sample_score_kernel, pg=pg),
        out_shape=jax.ShapeDtypeStruct((db, npg // pg, pg, ps), F32),
        grid_spec=grid_spec,
        compiler_params=_cparams(("arbitrary", "arbitrary")),
        name="dsa_sample_scores",
    )(page_table.reshape(-1), qi3, wi3, *([cache_ik] * pg))
    return out


def _dsa_sample_select_kernel(sc_ref, qi_ref, kw_ref, bias_ref, bnew_ref, *, cfg, n_sel, chunk):
    idim, ni = cfg.idx_dim, cfg.n_idx_heads
    db, p = sc_ref.shape
    nchunk = p // chunk
    kw = kw_ref[...]
    ki = kw[:, 0:idim]
    s_new = jnp.zeros((db, 1), F32)
    for h in range(ni):
        d = jnp.sum(qi_ref[:, h * idim:(h + 1) * idim] * ki, axis=-1, keepdims=True)
        s_new = s_new + (kw[:, idim + h:idim + h + 1] * (float(ni * idim) ** -0.5)) * jnp.maximum(d, 0.0)
    kn1 = s_new
    idx0 = lax.broadcasted_iota(I32, (db, chunk), 1)

    def count(pred, new_hit):
        cnt = jnp.zeros((db, LANES), I32)
        for c in range(nchunk):
            cnt = cnt + _fold_lanes(pred(sc_ref[:, c * chunk:(c + 1) * chunk], c * chunk + idx0).astype(I32))
        return jnp.sum(cnt, axis=-1, keepdims=True) + new_hit.astype(I32)

    thr = _kth_largest(lambda t: count(lambda k, i: k >= t, kn1 >= t), (db, 1), n_sel)
    need = n_sel - count(lambda k, i: k > thr, kn1 > thr)
    xcut = _tie_cutoff(lambda x: count(lambda k, i: (k == thr) & (i < x), (kn1 == thr) & (p < x)),
                       (db, 1), need, max(1, p.bit_length()))
    for c in range(nchunk):
        k = sc_ref[:, c * chunk:(c + 1) * chunk]
        sel = (k > thr) | ((k == thr) & (c * chunk + idx0 <= xcut))
        bias_ref[:, c * chunk:(c + 1) * chunk] = jnp.where(sel, 0.0, NEG_BIG)
    sel_new = (kn1 > thr) | ((kn1 == thr) & (p <= xcut))
    bnew_ref[...] = jnp.broadcast_to(jnp.where(sel_new, 0.0, NEG_BIG), (db, LANES))


def _dsa_sample_select(scores, u_s, cfg):
    db, p = scores.shape
    off, _ = _layout(cfg)
    iw = cfg.n_idx_heads * cfg.idx_dim
    n_sel = min(cfg.topk_max, (p + 1) // 4)
    chunk = min(2048, p)
    assert p % chunk == 0 and p + 1 >= n_sel
    kern = functools.partial(_dsa_sample_select_kernel, cfg=cfg, n_sel=n_sel, chunk=chunk)
    return pl.pallas_call(
        kern,
        out_shape=(jax.ShapeDtypeStruct((db, p), F32), jax.ShapeDtypeStruct((db, LANES), F32)),
        grid=(1,),
        in_specs=[pl.BlockSpec((db, p), lambda i: (0, 0)),
                  pl.BlockSpec((db, iw), lambda i: (0, off["qi"][0] // iw)),
                  pl.BlockSpec((db, LANES), lambda i: (0, off["kiwi"][0] // LANES))],
        out_specs=(pl.BlockSpec((db, p), lambda i: (0, 0)), pl.BlockSpec((db, LANES), lambda i: (0, 0))),
        compiler_params=_cparams(("arbitrary",)),
        name="dsa_sample_select",
    )(scores, u_s, u_s)


def _dsa_sample_attend_kernel(pt_ref, q_ref, bias_ref, bnew_ref, knew_ref, vnew_ref, *rest,
                              pg, cfg, past):
    k_refs, v_refs = rest[:pg], rest[pg:2 * pg]
    o_ref, m_s, l_s, acc_s = rest[2 * pg:]
    hd, ha = cfg.head_dim, cfg.n_heads_a
    j = pl.program_id(1)
    scale = float(hd) ** -0.5

    @pl.when(j == 0)
    def _init():
        m_s[...] = jnp.full(m_s.shape, NEG_BIG, F32)
        l_s[...] = jnp.zeros(l_s.shape, F32)
        acc_s[...] = jnp.zeros(acc_s.shape, F32)

    q = q_ref[0]
    hidx = lax.broadcasted_iota(I32, (ha, 1), 0)
    slopes = jnp.zeros((ha, 1), F32)
    for h in range(ha):
        slopes = jnp.where(hidx == h, float(2.0 ** (-8.0 * (h + 1) / ha)), slopes)

    def update(lg, bias, vs):
        sel = jnp.broadcast_to(bias == 0.0, lg.shape)
        lg = lg + bias
        m = m_s[...]
        m_new = jnp.maximum(m, jnp.max(lg, axis=-1, keepdims=True))
        p = jnp.where(sel, jnp.exp(lg - m_new), 0.0)
        alpha = jnp.exp(m - m_new)
        l_s[...] = alpha * l_s[...] + jnp.sum(p, axis=-1, keepdims=True)
        w = vs[0].shape[0]
        pv = _dot3(p[:, 0:w], vs[0])
        for r in range(1, len(vs)):
            pv = pv + _dot3(p[:, r * w:(r + 1) * w], vs[r])
        acc_s[...] = alpha * acc_s[...] + pv
        m_s[...] = m_new

    n = pg * LANES
    pos = j * n + lax.broadcasted_iota(I32, (1, n), 1)
    lg = jnp.concatenate([_dot3(q, k_refs[r][...], _dot_nt) for r in range(pg)], axis=1)
    lg = lg * scale - slopes * (past - pos).astype(F32)
    update(lg, bias_ref[0, 0], [v_refs[r][...] for r in range(pg)])

    @pl.when(j == pl.num_programs(1) - 1)
    def _fin():
        kn = jnp.broadcast_to(knew_ref[0], (16, hd))
        vn = jnp.broadcast_to(vnew_ref[0], (16, hd))
        lg = _dot3(q, kn, _dot_nt) * scale
        first = lax.broadcasted_iota(I32, (1, 16), 1) == 0
        update(lg, jnp.where(first, bnew_ref[0][:, 0:16], NEG_BIG), [vn])
        o_ref[0] = (acc_s[...] / l_s[...]).astype(o_ref.dtype)


def _dsa_sample_attend(qa3, bias, bias_new, u_s3, cache_k, cache_v, page_table, cfg):
    db, ha, hd = qa3.shape
    npg = page_table.shape[1]
    pg = min(cfg.pages_per_step, npg)
    ps = cache_k.shape[1]
    bias4 = bias.reshape(db, npg // pg, 1, pg * ps)
    off, _ = _layout(cfg)
    page_of = lambda b, j, r, pt: pt[b * npg + j * pg + r]
    kern = functools.partial(_dsa_sample_attend_kernel, pg=pg, cfg=cfg, past=npg * ps)
    grid_spec = pltpu.PrefetchScalarGridSpec(
        num_scalar_prefetch=1,
        grid=(db, npg // pg),
        in_specs=[pl.BlockSpec((1, ha, hd), lambda b, j, pt: (b, 0, 0)),
                  pl.BlockSpec((1, 1, 1, pg * ps), lambda b, j, pt: (b, j, 0, 0)),
                  pl.BlockSpec((1, 1, LANES), lambda b, j, pt: (b, 0, 0)),
                  pl.BlockSpec((1, 1, hd), lambda b, j, pt: (b, 0, off["ka"][0] // hd)),
                  pl.BlockSpec((1, 1, hd), lambda b, j, pt: (b, 0, off["va"][0] // hd))]
                 + _page_specs(pg, (1, ps, hd), page_of) + _page_specs(pg, (1, ps, hd), page_of),
        out_specs=pl.BlockSpec((1, ha, hd), lambda b, j, pt: (b, 0, 0)),
        scratch_shapes=[pltpu.VMEM((ha, 1), F32), pltpu.VMEM((ha, 1), F32), pltpu.VMEM((ha, hd), F32)],
    )
    return pl.pallas_call(
        kern,
        out_shape=jax.ShapeDtypeStruct((db, ha, hd), F32),
        grid_spec=grid_spec,
        compiler_params=_cparams(("arbitrary", "arbitrary")),
        name="dsa_sample_attend",
    )(page_table.reshape(-1), qa3, bias4, bias_new.reshape(db, 1, LANES),
      u_s3, u_s3, *([cache_k] * pg), *([cache_v] * pg))


def _sb_sample_kernel(pt_ref, q_ref, k_hbm, v_hbm, o_ref, kbuf, vbuf, sem, *, pg, npg, cfg):
    hd, hb, g = cfg.head_dim, cfg.n_heads_b, cfg.n_kv_b
    rep = hb // g
    b = pl.program_id(0)
    nchunks = npg // pg
    scale = float(hd) ** -0.5

    def page_copies(c, slot):
        out = []
        for r in range(pg):
            page = pt_ref[b * npg + (npg - 1 - (c * pg + r))]
            out.append(pltpu.make_async_copy(k_hbm.at[page], kbuf.at[slot, r], sem.at[0, slot]))
            out.append(pltpu.make_async_copy(v_hbm.at[page], vbuf.at[slot, r], sem.at[1, slot]))
        return out

    q = q_ref[0]
    qg = [q[:, gi * hd:(gi + 1) * hd] for gi in range(g)]
    upper = _strict_lower_ones(LANES)

    for cp in page_copies(0, 0):
        cp.start()

    def body(loop):
        c, _, carry, accs = loop
        slot = c % 2

        @pl.when(c + 1 < nchunks)
        def _prefetch():
            for cp in page_copies(c + 1, 1 - slot):
                cp.start()

        for cp in page_copies(c, slot):
            cp.wait()
        zs = []
        for r in range(pg):
            z = _dot3(qg[0], kbuf[slot, r, :, 0, :], _dot_nt)
            for gi in range(1, g):
                z = z + _dot3(qg[gi], kbuf[slot, r, :, gi, :], _dot_nt)
            zs.append(z * scale)
        z_all = jnp.concatenate(zs, axis=0)
        lk_all = -(jnp.maximum(z_all, 0.0) + jnp.log(1.0 + jnp.exp(-jnp.abs(z_all))))
        tail_in = _suffix_in_chunk(lk_all, upper)
        tot = jnp.sum(lk_all, axis=-1, keepdims=True)
        accs = list(accs)
        for r in range(pg):
            sl = slice(r * hb, (r + 1) * hb)
            w = jnp.exp(z_all[sl] + lk_all[sl] + (carry + tail_in[sl]))
            for gi in range(g):
                accs[gi] = accs[gi] + _dot3(w, vbuf[slot, r, :, gi, :])
            carry = carry + tot[sl]
        return c + 1, jnp.max(carry) > SB_DEAD_TAIL, carry, tuple(accs)

    init = (jnp.int32(0), jnp.bool_(True), jnp.zeros((hb, 1), F32), tuple(jnp.zeros((hb, hd), F32) for _ in range(g)))
    c_end, _, _, accs = lax.while_loop(lambda loop: (loop[0] < nchunks) & loop[1], body, init)

    @pl.when(c_end < nchunks)
    def _drain():
        for cp in page_copies(c_end, c_end % 2):
            cp.wait()

    head_group = lax.broadcasted_iota(I32, (hb, hd), 0) // rep
    out = jnp.zeros((hb, hd), F32)
    for gi in range(g):
        out = out + jnp.where(head_group == gi, accs[gi], 0.0)
    o_ref[0] = out.astype(o_ref.dtype)


def _sb_sample(q_bd, cache_k, cache_v, page_table, cfg):
    db, hb, gw = q_bd.shape
    npg = page_table.shape[1]
    _, ps, g, hd = cache_k.shape
    pg = min(cfg.sb_pages_per_chunk, npg)
    assert npg % pg == 0 and ps == LANES and g * hd == gw
    grid_spec = pltpu.PrefetchScalarGridSpec(
        num_scalar_prefetch=1,
        grid=(db,),
        in_specs=[pl.BlockSpec((1, hb, gw), lambda b, pt: (b, 0, 0)),
                  pl.BlockSpec(memory_space=pl.ANY), pl.BlockSpec(memory_space=pl.ANY)],
        out_specs=pl.BlockSpec((1, hb, cfg.head_dim), lambda b, pt: (b, 0, 0)),
        scratch_shapes=[pltpu.VMEM((2, pg, ps, g, hd), F32), pltpu.VMEM((2, pg, ps, g, hd), F32),
                        pltpu.SemaphoreType.DMA((2, 2))],
    )
    return pl.pallas_call(
        functools.partial(_sb_sample_kernel, pg=pg, npg=npg, cfg=cfg),
        out_shape=jax.ShapeDtypeStruct((db, hb, cfg.head_dim), F32),
        grid_spec=grid_spec,
        compiler_params=_cparams(("arbitrary",)),
        name="sb_sample",
    )(page_table.reshape(-1), q_bd, cache_k, cache_v)


def _merge_kernel(x_ref, oa_ref, ob_ref, ga_ref, gb_ref, m_ref, gf_ref, wua_ref, wub_ref, wo_ref, wr_ref, br_ref,
                  x1_ref, h2_ref, rt_ref, *, cfg):
    ng, epg = cfg.n_groups, cfg.epg
    sig = lambda v: 1.0 / (1.0 + jnp.exp(-v))

    def mm(a, w_ref):
        dot = lambda p, q: jnp.dot(p, q, preferred_element_type=F32)
        if w_ref.shape[0] == 1:
            return dot(a.astype(BF16), w_ref[0])
        a_hi, a_lo = _split(a)
        return dot(a_hi, w_ref[0]) + dot(a_lo, w_ref[0]) + dot(a_hi, w_ref[1])

    merged = sig(ga_ref[...]) * mm(oa_ref[...], wua_ref) + sig(gb_ref[...]) * mm(ob_ref[...], wub_ref)
    x1 = x_ref[0] + m_ref[0, 2] * mm(merged, wo_ref)
    x1_ref[...] = x1
    h2 = x1 * lax.rsqrt(jnp.mean(x1 * x1, axis=-1, keepdims=True) + RMS_EPS) * gf_ref[...]
    h2 = h2 * (1.0 + m_ref[0, 4]) + m_ref[0, 3]
    h2_ref[...] = h2
    logits = mm(h2, wr_ref) + br_ref[...]

    lane = lax.broadcasted_iota(I32, logits.shape, 1).astype(F32)
    first = lambda hit: jnp.min(jnp.where(hit, lane, float(LANES)), axis=-1, keepdims=True)
    in_g = lane < ng
    gl = jnp.where(in_g, logits, -jnp.inf)
    gmax = jnp.max(gl, axis=-1, keepdims=True)
    gidx = first(gl == gmax)
    p_g = 1.0 / jnp.sum(jnp.where(in_g, jnp.exp(logits - gmax), 0.0), axis=-1, keepdims=True)
    lo = ng + gidx * epg
    el = jnp.where((lane >= lo) & (lane < lo + epg), logits, -jnp.inf)
    v1 = jnp.max(el, axis=-1, keepdims=True)
    i1 = first(el == v1)
    el2 = jnp.where(lane == i1, -jnp.inf, el)
    v2 = jnp.max(el2, axis=-1, keepdims=True)
    i2 = first(el2 == v2)
    e21 = jnp.exp(v2 - v1)
    w1 = 1.0 / (1.0 + e21)
    w2 = e21 / (1.0 + e21)
    rt = jnp.where(lane == 0, i1 - ng, jnp.where(lane == 1, i2 - ng,
         jnp.where(lane == 2, p_g * w1, jnp.where(lane == 3, p_g * w2, 0.0))))
    rt_ref[...] = rt


def _merge(x3, oa, ob, u, mod, g_ffn, wua, wub, wo, wr, br, cfg, tm):
    bx, t, d = x3.shape
    m = bx * t
    off, _ = _layout(cfg)
    aw, bw = oa.shape[1], ob.shape[1]
    r = mod.shape[2]
    nt = t // tm
    const = lambda shape: pl.BlockSpec(shape, lambda b, i: (0,) * len(shape), pipeline_mode=pl.Buffered(1))
    return pl.pallas_call(
        functools.partial(_merge_kernel, cfg=cfg),
        out_shape=(jax.ShapeDtypeStruct((m, d), F32), jax.ShapeDtypeStruct((m, d), F32),
                   jax.ShapeDtypeStruct((m, LANES), F32)),
        grid=(bx, nt),
        in_specs=[pl.BlockSpec((1, tm, d), lambda b, i: (b, i, 0)),
                  pl.BlockSpec((tm, aw), lambda b, i: (b * nt + i, 0)),
                  pl.BlockSpec((tm, bw), lambda b, i: (b * nt + i, 0)),
                  pl.BlockSpec((tm, d), lambda b, i: (b * nt + i, off["ga"][0] // d)),
                  pl.BlockSpec((tm, d), lambda b, i: (b * nt + i, off["gb"][0] // d)),
                  pl.BlockSpec((1, 6, r, d), lambda b, i: (b, 0, 0, 0)),
                  const((1, d)), const(wua.shape), const(wub.shape), const(wo.shape), const((2, d, LANES)),
                  const((1, LANES))],
        out_specs=(pl.BlockSpec((tm, d), lambda b, i: (b * nt + i, 0)),
                   pl.BlockSpec((tm, d), lambda b, i: (b * nt + i, 0)),
                   pl.BlockSpec((tm, LANES), lambda b, i: (b * nt + i, 0))),
        compiler_params=_cparams(("arbitrary", "arbitrary")),
        name="merge_out_router",
    )(x3, oa, ob, u, u, mod, g_ffn.reshape(1, d), wua, wub, wo, wr, br)


def _moe_kernel(be_ref, nb_ref, tok_ref, tokn_ref, h_hbm, wg_ref, wu_ref, wd_ref, y_ref,
                xbuf, sem, wg_s, wu_s, wd_s, *, rows):
    i = pl.program_id(0)
    nb = nb_ref[0]

    def row_copy(tok, slot, r):
        return pltpu.make_async_copy(h_hbm.at[pl.ds(tok, 1)], xbuf.at[slot, pl.ds(r, 1)], sem.at[slot])

    def issue(idx_ref, slot):
        def body(r, c):
            row_copy(idx_ref[0, 0, r], slot, r).start()
            return c

        lax.fori_loop(0, rows, body, 0, unroll=8)

    def wait_slot(slot):
        pltpu.make_async_copy(h_hbm.at[pl.ds(0, rows)], xbuf.at[slot], sem.at[slot]).wait()

    @pl.when(i == 0)
    def _prime():
        issue(tok_ref, 0)

    @pl.when(i < nb)
    def _compute():
        slot = i % 2
        wait_slot(slot)

        @pl.when((i == 0) | (be_ref[i] != be_ref[jnp.maximum(i - 1, 0)]))
        def _cast():
            wg_s[...] = wg_ref[...].astype(BF16)
            wu_s[...] = wu_ref[...].astype(BF16)
            wd_s[...] = wd_ref[...].astype(BF16)

        x = xbuf[slot].astype(BF16)
        for r in range(rows):
            row_copy(tokn_ref[0, 0, r], 1 - slot, r).start()
        gate = jnp.dot(x, wg_s[...], preferred_element_type=F32)
        up = jnp.dot(x, wu_s[...], preferred_element_type=F32)
        act = (gate * (1.0 / (1.0 + jnp.exp(-gate))) * up).astype(BF16)
        y_ref[...] = jnp.dot(act, wd_s[...], preferred_element_type=F32)

        @pl.when(i == nb - 1)
        def _drain():
            wait_slot(1 - slot)

    @pl.when(i >= nb)
    def _idle():
        y_ref[...] = jnp.zeros(y_ref.shape, F32)


def _moe_experts(h2_all, slot_token, block_expert, n_used, w_gate_e, w_up_e, w_down_e, rows):
    n_blocks = block_expert.shape[0]
    e, d, f = w_gate_e.shape
    tok3 = slot_token.reshape(n_blocks, 1, rows)
    grid_spec = pltpu.PrefetchScalarGridSpec(
        num_scalar_prefetch=2,
        grid=(n_blocks,),
        in_specs=[pl.BlockSpec((1, 1, rows), lambda i, be, nb: (i, 0, 0), memory_space=pltpu.SMEM),
                  pl.BlockSpec((1, 1, rows), lambda i, be, nb: (jnp.minimum(i + 1, nb[0] - 1), 0, 0),
                               memory_space=pltpu.SMEM),
                  pl.BlockSpec(memory_space=pl.ANY),
                  pl.BlockSpec((None, d, f), lambda i, be, nb: (be[i], 0, 0)),
                  pl.BlockSpec((None, d, f), lambda i, be, nb: (be[i], 0, 0)),
                  pl.BlockSpec((None, f, d), lambda i, be, nb: (be[i], 0, 0))],
        out_specs=pl.BlockSpec((rows, d), lambda i, be, nb: (i, 0)),
        scratch_shapes=[pltpu.VMEM((2, rows, d), F32), pltpu.SemaphoreType.DMA((2,)),
                        pltpu.VMEM((d, f), BF16), pltpu.VMEM((d, f), BF16), pltpu.VMEM((f, d), BF16)],
    )
    return pl.pallas_call(
        functools.partial(_moe_kernel, rows=rows),
        out_shape=jax.ShapeDtypeStruct((n_blocks * rows, d), F32),
        grid_spec=grid_spec,
        compiler_params=_cparams(("arbitrary",)),
        name="moe_experts",
    )(block_expert, n_used, tok3, tok3, h2_all, w_gate_e, w_up_e, w_down_e)


def _rank_kernel(rt_ref, base_ref, dest_ref, tot_ref, base_s, *, tm):
    @pl.when(pl.program_id(0) == 0)
    def _init():
        base_s[...] = base_ref[...]

    rt = rt_ref[...]
    lane = lax.broadcasted_iota(I32, rt.shape, 1).astype(F32)
    o1 = (lane == rt[:, 0:1]).astype(F32)
    o2 = (lane == rt[:, 1:2]).astype(F32)
    lower = _strict_lower_ones(tm)
    p1 = jnp.dot(lower, o1.astype(BF16), preferred_element_type=F32)
    p2 = jnp.dot(lower, o2.astype(BF16), preferred_element_type=F32)
    c1 = jnp.sum(o1, axis=0, keepdims=True)
    c2 = jnp.sum(o2, axis=0, keepdims=True)
    base = base_s[...]
    d1 = jnp.sum((base + p1) * o1, axis=-1, keepdims=True)
    d2 = jnp.sum((base + c1 + p2) * o2, axis=-1, keepdims=True)
    dest_ref[...] = jnp.where(lane == 0, d1, jnp.where(lane == 1, d2, 0.0))
    base_s[...] = base + c1 + c2
    tot_ref[...] = base + c1 + c2


def _rank(rt_pad, base, tm):
    n = rt_pad.shape[0]
    return pl.pallas_call(
        functools.partial(_rank_kernel, tm=tm),
        out_shape=(jax.ShapeDtypeStruct((n, LANES), F32), jax.ShapeDtypeStruct((1, LANES), F32)),
        grid=(n // tm,),
        in_specs=[pl.BlockSpec((tm, LANES), lambda i: (i, 0)), pl.BlockSpec((1, LANES), lambda i: (0, 0))],
        out_specs=(pl.BlockSpec((tm, LANES), lambda i: (i, 0)), pl.BlockSpec((1, LANES), lambda i: (0, 0))),
        scratch_shapes=[pltpu.VMEM((1, LANES), F32)],
        compiler_params=_cparams(("arbitrary",)),
        name="moe_rank",
    )(rt_pad, base)


def _moe_plan(rt_all, n_experts, rows, top_k, tm):
    n_tok = rt_all.shape[0]
    a = n_tok * top_k
    assert top_k == 2 and n_experts <= LANES
    n_blocks = -(-(a + n_experts * (rows - 1)) // rows)
    n_pad = _round_up(n_tok, tm)
    rt_pad = jnp.concatenate([rt_all, jnp.full((n_pad - n_tok, LANES), -1.0, F32)], axis=0)
    _, tot = _rank(rt_pad, jnp.zeros((1, LANES), F32), tm)
    counts = tot[0, :n_experts].astype(I32)
    padded = (counts + rows - 1) // rows * rows
    cum = jnp.cumsum(padded)
    pstart = jnp.zeros((1, LANES), F32).at[0, :n_experts].set((cum - padded).astype(F32))
    dest, _ = _rank(rt_pad, pstart, tm)
    pos = (dest[:n_tok, 0].astype(I32), dest[:n_tok, 1].astype(I32))
    tok = jnp.arange(n_tok, dtype=I32)
    slot_token = jnp.zeros((n_blocks * rows,), I32).at[jnp.concatenate(pos)].set(jnp.concatenate([tok, tok]))
    starts = jnp.arange(n_blocks, dtype=I32) * rows
    block_expert = jnp.minimum(jnp.sum((cum[None, :] <= starts[:, None]).astype(I32), axis=1), n_experts - 1)
    n_used = (cum[-1] // rows).astype(I32).reshape(1)
    return slot_token, pos, block_expert, n_used


def _combine_kernel(pos_ref, posn_ref, x1_ref, rt_ref, m_ref, gfin_ref, ys_hbm, o_ref, ybuf, sem, *, tm):
    t = pl.program_id(0)
    nt = pl.num_programs(0)

    def row_copy(p, slot, r):
        return pltpu.make_async_copy(ys_hbm.at[pl.ds(p, 1)], ybuf.at[slot, pl.ds(r, 1)], sem.at[slot])

    def issue(idx_ref, slot):
        def body(r, c):
            row_copy(idx_ref[0, 0, r], slot, r).start()
            return c

        lax.fori_loop(0, 2 * tm, body, 0, unroll=8)

    @pl.when(t == 0)
    def _prime():
        issue(pos_ref, 0)

    @pl.when(t + 1 < nt)
    def _prefetch():
        issue(posn_ref, (t + 1) % 2)

    slot = t % 2
    pltpu.make_async_copy(ys_hbm.at[pl.ds(0, 2 * tm)], ybuf.at[slot], sem.at[slot]).wait()

    rt = rt_ref[...]
    moe = rt[:, 2:3] * ybuf[slot, 0:tm, :] + rt[:, 3:4] * ybuf[slot, tm:2 * tm, :]
    x2 = x1_ref[...] + m_ref[0, 5] * moe
    y = x2 * lax.rsqrt(jnp.mean(x2 * x2, axis=-1, keepdims=True) + RMS_EPS)
    o_ref[...] = y * gfin_ref[...]


def _combine(x1, rt, mod, g_final, ys, pos, t_per_b, tm):
    m, d = x1.shape
    nt_all = m // tm
    nt_b = t_per_b // tm
    r = mod.shape[2]
    pos3 = jnp.concatenate([p.reshape(nt_all, 1, tm) for p in pos], axis=2)
    return pl.pallas_call(
        functools.partial(_combine_kernel, tm=tm),
        out_shape=jax.ShapeDtypeStruct((m, d), F32),
        grid=(nt_all,),
        in_specs=[pl.BlockSpec((1, 1, 2 * tm), lambda t: (t, 0, 0), memory_space=pltpu.SMEM),
                  pl.BlockSpec((1, 1, 2 * tm), lambda t: (jnp.minimum(t + 1, nt_all - 1), 0, 0),
                               memory_space=pltpu.SMEM),
                  pl.BlockSpec((tm, d), lambda t: (t, 0)),
                  pl.BlockSpec((tm, LANES), lambda t: (t, 0)),
                  pl.BlockSpec((1, 6, r, d), lambda t: (t // nt_b, 0, 0, 0)),
                  pl.BlockSpec((1, d), lambda t: (0, 0)),
                  pl.BlockSpec(memory_space=pl.ANY)],
        out_specs=pl.BlockSpec((tm, d), lambda t: (t, 0)),
        scratch_shapes=[pltpu.VMEM((2, 2 * tm, d), F32), pltpu.SemaphoreType.DMA((2,))],
        compiler_params=_cparams(("arbitrary",)),
        name="moe_combine",
    )(pos3, pos3, x1, rt, mod, g_final.reshape(1, d), ys)


def _forward(cfg, x_prompt, x_sample, cache_a_k, cache_a_v, cache_idx_k, cache_b_k, cache_b_v,
             page_table, c_prompt, c_sample, w_ada, b_ada, g_mix, g_ffn, g_final, w_in,
             w_up_a, w_up_b, w_out, w_group, b_group, w_router, b_router, w_gate_e, w_up_e, w_down_e):
    depth = w_in.shape[0]
    bp, seq, d = x_prompt.shape
    db, dseq, _ = x_sample.shape
    assert dseq == 1, "sample path is written for single-token decode"
    hd = cfg.head_dim
    off, wpad = _layout(cfg)
    ps = cfg.page_size
    ng, epg = cfg.n_groups, cfg.epg
    n_exp = ng * epg
    col = lambda a, name: a[..., off[name][0]:off[name][0] + off[name][1]]

    xp, xs = x_prompt, x_sample.reshape(1, db, d)
    c_all = jnp.concatenate([c_prompt, c_sample, jnp.zeros((-(bp + db) % 8, d), F32)], axis=0)
    rows_p, rows_s = [], []
    for layer in range(depth):
        mod = _adaln(c_all, w_ada[layer], b_ada[layer])
        mod_p = mod[:bp].reshape(bp, 6, 1, d)
        mod_s = mod[bp:bp + db].reshape(db, 6, d).transpose(1, 0, 2).reshape(1, 6, db, d)
        w_al = _permute_w_in(w_in[layer], cfg)
        def parts(w32):
            hi = w32.astype(BF16)
            return jnp.stack([hi, (w32 - hi.astype(F32)).astype(BF16)])

        wua, wub, wo = parts(w_up_a[layer]), parts(w_up_b[layer]), parts(w_out[layer])
        wr = parts(jnp.concatenate([w_group[layer], w_router[layer].transpose(1, 0, 2).reshape(d, n_exp),
                                    jnp.zeros((d, LANES - ng - n_exp), F32)], axis=1))
        br = jnp.concatenate([b_group[layer], b_router[layer].reshape(-1),
                              jnp.zeros((LANES - ng - n_exp,), F32)]).reshape(1, LANES)

        ts_p = min(256, seq)
        h_p = _rms_mod(xp, g_mix[layer], mod_p, ts_p).reshape(bp * seq, d)
        u_p = _matmul(h_p, w_al, min(cfg.tm_in, bp * seq), cfg.tn_in)
        u_p3 = u_p.reshape(bp, seq, wpad)
        oa_p = _dsa_prompt(u_p3, cfg).reshape(bp * seq, -1)
        ob_p = _sb_prompt(u_p3, cfg).reshape(bp * seq, -1)

        h_s = _rms_mod(xs, g_mix[layer], mod_s, db, out_dtype=F32).reshape(db, d)
        u_s = _matmul(h_s, w_al, db, cfg.tn_in)
        qi3 = col(u_s, "qi").reshape(db, cfg.n_idx_heads, cfg.idx_dim)
        kiwi_s = col(u_s, "kiwi")
        wi3 = (kiwi_s[:, cfg.idx_dim:cfg.idx_dim + cfg.n_idx_heads]
               * (float(cfg.n_idx_heads * cfg.idx_dim) ** -0.5)).reshape(db, cfg.n_idx_heads, 1)
        c_ik = cache_idx_k[layer]
        c_ak = cache_a_k[layer].reshape(cache_a_k.shape[1], ps, hd)
        c_av = cache_a_v[layer].reshape(cache_a_v.shape[1], ps, hd)
        c_bk, c_bv = cache_b_k[layer], cache_b_v[layer]
        sc4 = _dsa_sample_scores(qi3, wi3, c_ik, page_table, cfg)
        bias, bias_new = _dsa_sample_select(sc4.reshape(db, -1), u_s, cfg)
        qa3 = col(u_s, "qa").reshape(db, cfg.n_heads_a, hd)
        oa_s = _dsa_sample_attend(qa3, bias, bias_new, u_s.reshape(db, 1, wpad), c_ak, c_av,
                                  page_table, cfg).reshape(db, -1)
        rep = cfg.n_heads_b // cfg.n_kv_b
        qb3 = col(u_s, "qb").reshape(db, cfg.n_heads_b, 1, hd)
        gsel = (jnp.arange(cfg.n_heads_b)[:, None] // rep == jnp.arange(cfg.n_kv_b)[None, :])
        q_bd = (qb3 * gsel[None, :, :, None].astype(F32)).reshape(db, cfg.n_heads_b, cfg.n_kv_b * hd)
        ob_s = _sb_sample(q_bd, c_bk, c_bv, page_table, cfg).reshape(db, -1)

        x1_p, h2_p, rt_p = _merge(xp, oa_p, ob_p, u_p, mod_p, g_ffn[layer], wua[:1], wub[:1], wo[:1], wr, br, cfg,
                                  min(cfg.tm_merge, seq))
        x1_s, h2_s, rt_s = _merge(xs, oa_s, ob_s, u_s, mod_s, g_ffn[layer], wua, wub, wo, wr, br, cfg, db)

        h2_all = jnp.concatenate([h2_p, h2_s], axis=0)
        rt_all = jnp.concatenate([rt_p, rt_s], axis=0)
        slot_token, pos, block_expert, n_used = _moe_plan(rt_all, n_exp, cfg.moe_rows, cfg.top_k,
                                                          min(256, bp * seq))
        ys = _moe_experts(h2_all, slot_token, block_expert, n_used,
                          w_gate_e[layer], w_up_e[layer], w_down_e[layer], cfg.moe_rows)
        last = layer == depth - 1
        gfin = g_final if last else jnp.ones_like(g_final)
        assert last, "deeper stacks need an un-normalised combine for inner layers"
        n_p = bp * seq
        y_p = _combine(x1_p, rt_p, mod_p, gfin, ys, tuple(p[:n_p] for p in pos), seq, min(cfg.tm_comb, seq))
        y_s = _combine(x1_s, rt_s, mod_s, gfin, ys, tuple(p[n_p:] for p in pos), db, db)

        pages = lambda a: a.reshape((bp, seq // ps, ps) + a.shape[2:])
        rows_p.append((pages(col(u_p3, "ka").reshape(bp, seq, 1, hd)),
                       pages(col(u_p3, "va").reshape(bp, seq, 1, hd)),
                       pages(col(u_p3, "kiwi")[..., :cfg.idx_dim]),
                       pages(col(u_p3, "kb").reshape(bp, seq, cfg.n_kv_b, hd)),
                       pages(col(u_p3, "vb").reshape(bp, seq, cfg.n_kv_b, hd))))
        rows_s.append((col(u_s, "ka").reshape(db, 1, 1, hd), col(u_s, "va").reshape(db, 1, 1, hd),
                       kiwi_s[:, :cfg.idx_dim].reshape(db, 1, cfg.idx_dim),
                       col(u_s, "kb").reshape(db, 1, cfg.n_kv_b, hd),
                       col(u_s, "vb").reshape(db, 1, cfg.n_kv_b, hd)))
        xp, xs = y_p.reshape(bp, seq, d), y_s.reshape(1, db, d)

    stack = lambda rows: tuple(jnp.stack(r) for r in zip(*rows))
    return (xp, xs.reshape(db, 1, d)) + stack(rows_p) + stack(rows_s)


def kernel(x_prompt, x_sample, cache_a_k, cache_a_v, cache_idx_k, cache_b_k, cache_b_v, page_table,
           c_prompt, c_sample, w_ada, b_ada, g_mix, g_ffn, g_final, w_in, w_up_a, w_up_b, w_out,
           w_group, b_group, w_router, b_router, w_gate_e, w_up_e, w_down_e):
    return _forward(Cfg(), x_prompt, x_sample, cache_a_k, cache_a_v, cache_idx_k, cache_b_k, cache_b_v,
                    page_table, c_prompt, c_sample, w_ada, b_ada, g_mix, g_ffn, g_final, w_in,
                    w_up_a, w_up_b, w_out, w_group, b_group, w_router, b_router, w_gate_e, w_up_e, w_down_e)
```

```python
import functools
from typing import NamedTuple

import numpy as np
import jax
import jax.numpy as jnp
from jax import lax
from jax.experimental import pallas as pl
from jax.experimental.pallas import tpu as pltpu

F32 = jnp.float32
BF16 = jnp.bfloat16
I32 = jnp.int32

LANES = 128
INT_MIN = -(2 ** 31)
RMS_EPS = 1e-6
NEG_BIG = -1e30
SB_DEAD_TAIL = -110.0
VMEM_LIMIT = 48 * 1024 * 1024


class Cfg(NamedTuple):
    d_model: int = 2048
    head_dim: int = 128
    n_heads_a: int = 8
    n_idx_heads: int = 16
    idx_dim: int = 64
    n_heads_b: int = 8
    n_kv_b: int = 4
    topk_max: int = 256
    n_groups: int = 8
    epg: int = 8
    top_k: int = 2
    d_ff: int = 512
    page_size: int = 128
    tn_in: int = 512
    tm_in: int = 1024
    tq_a: int = 256
    tk_a: int = 256
    t_b: int = 256
    tm_merge: int = 256
    moe_rows: int = 128
    tm_comb: int = 128
    pages_per_step: int = 16
    sb_pages_per_chunk: int = 4


def _cparams(sem):
    return pltpu.CompilerParams(dimension_semantics=sem, vmem_limit_bytes=VMEM_LIMIT)


def _round_up(a, b):
    return -(-a // b) * b


def _layout(cfg):
    d, hd = cfg.d_model, cfg.head_dim
    segs = [("ga", d), ("gb", d), ("qa", cfg.n_heads_a * hd), ("qi", cfg.n_idx_heads * cfg.idx_dim),
            ("qb", cfg.n_heads_b * hd), ("kb", cfg.n_kv_b * hd), ("vb", cfg.n_kv_b * hd),
            ("ka", hd), ("va", hd), ("kiwi", LANES)]
    off, o = {}, 0
    for name, w in segs:
        off[name] = (o, w)
        o += w
    return off, _round_up(o, cfg.tn_in)


def _permute_w_in(w_in, cfg):
    hd = cfg.head_dim
    sizes = [cfg.n_heads_a * hd, hd, hd, cfg.n_idx_heads * cfg.idx_dim, cfg.idx_dim, cfg.n_idx_heads,
             cfg.n_heads_b * hd, cfg.n_kv_b * hd, cfg.n_kv_b * hd, cfg.d_model, cfg.d_model]
    names = ["qa", "ka", "va", "qi", "ki", "wi", "qb", "kb", "vb", "ga", "gb"]
    cuts = np.concatenate([[0], np.cumsum(sizes)])
    src = {n: w_in[:, int(cuts[i]):int(cuts[i + 1])] for i, n in enumerate(names)}
    off, width = _layout(cfg)
    assert cfg.idx_dim + cfg.n_idx_heads <= LANES
    kiwi = jnp.concatenate(
        [src["ki"], src["wi"], jnp.zeros((w_in.shape[0], LANES - cfg.idx_dim - cfg.n_idx_heads), w_in.dtype)], axis=1)
    parts = [src["ga"], src["gb"], src["qa"], src["qi"], src["qb"], src["kb"], src["vb"], src["ka"], src["va"], kiwi]
    used = sum(p.shape[1] for p in parts)
    if width > used:
        parts.append(jnp.zeros((w_in.shape[0], width - used), w_in.dtype))
    return jnp.concatenate(parts, axis=1)


def _split(x):
    hi = x.astype(BF16)
    return hi, (x - hi.astype(F32)).astype(BF16)


def _dot3(a, b, dot=None):
    dot = dot or (lambda x, y: jnp.dot(x, y, preferred_element_type=F32))
    (ah, al), (bh, bl) = _split(a), _split(b)
    return dot(ah, bh) + dot(al, bh) + dot(ah, bl)


def _adaln_kernel(c_ref, w_ref, b_ref, o_ref):
    c = c_ref[...]
    o_ref[...] = _dot3(c * (1.0 / (1.0 + jnp.exp(-c))), w_ref[...]) + b_ref[...]


def _adaln(c_all, w_ada, b_ada):
    mc, d = c_all.shape
    n = w_ada.shape[1]
    tn = 1024 if n % 1024 == 0 else n
    return pl.pallas_call(
        _adaln_kernel,
        out_shape=jax.ShapeDtypeStruct((mc, n), F32),
        grid=(n // tn,),
        in_specs=[pl.BlockSpec((mc, d), lambda j: (0, 0)),
                  pl.BlockSpec((d, tn), lambda j: (0, j)),
                  pl.BlockSpec((1, tn), lambda j: (0, j))],
        out_specs=pl.BlockSpec((mc, tn), lambda j: (0, j)),
        compiler_params=_cparams(("arbitrary",)),
        name="adaln",
    )(c_all, w_ada, b_ada.reshape(1, n))


def _rms_mod_kernel(x_ref, g_ref, m_ref, o_ref):
    x = x_ref[0]
    y = x * lax.rsqrt(jnp.mean(x * x, axis=-1, keepdims=True) + RMS_EPS) * g_ref[...]
    o_ref[0] = (y * (1.0 + m_ref[0, 1]) + m_ref[0, 0]).astype(o_ref.dtype)


def _rms_mod(x3, g, mod, ts, out_dtype=None):
    bx, t, d = x3.shape
    r = mod.shape[2]
    out_dtype = out_dtype or BF16
    return pl.pallas_call(
        _rms_mod_kernel,
        out_shape=jax.ShapeDtypeStruct((bx, t, d), out_dtype),
        grid=(bx, t // ts),
        in_specs=[pl.BlockSpec((1, ts, d), lambda b, i: (b, i, 0)),
                  pl.BlockSpec((1, d), lambda b, i: (0, 0)),
                  pl.BlockSpec((1, 6, r, d), lambda b, i: (b, 0, 0, 0))],
        out_specs=pl.BlockSpec((1, ts, d), lambda b, i: (b, i, 0)),
        compiler_params=_cparams(("arbitrary", "arbitrary")),
        name="rms_mod",
    )(x3, g.reshape(1, d), mod)


def _mm_kernel(a_ref, w_ref, o_ref, wbf_ref):
    @pl.when(pl.program_id(1) == 0)
    def _():
        wbf_ref[...] = w_ref[...].astype(BF16)

    o_ref[...] = jnp.dot(a_ref[...], wbf_ref[...], preferred_element_type=F32).astype(o_ref.dtype)


def _mm3_kernel(a_ref, w_ref, o_ref, wbf_ref):
    del wbf_ref
    o_ref[...] = _dot3(a_ref[...], w_ref[...]).astype(o_ref.dtype)


def _matmul(a, w, tm, tn, out_dtype=F32):
    m, k = a.shape
    n = w.shape[1]
    return pl.pallas_call(
        _mm_kernel if a.dtype == BF16 else _mm3_kernel,
        out_shape=jax.ShapeDtypeStruct((m, n), out_dtype),
        grid=(n // tn, m // tm),
        in_specs=[pl.BlockSpec((tm, k), lambda j, i: (i, 0)),
                  pl.BlockSpec((k, tn), lambda j, i: (0, j))],
        out_specs=pl.BlockSpec((tm, tn), lambda j, i: (i, j)),
        scratch_shapes=[pltpu.VMEM((k, tn), BF16)],
        compiler_params=_cparams(("arbitrary", "arbitrary")),
        name="in_proj",
    )(a, w)


def _ordered_bits_to_float(u):
    key = u ^ jnp.int32(INT_MIN)
    return pltpu.bitcast(jnp.where(key < 0, key ^ jnp.int32(0x7FFFFFFF), key), F32)


def _kth_largest(count_ge, shape, k):
    def body(bi, res):
        cand = res | lax.shift_left(jnp.int32(1), jnp.int32(31) - bi)
        return jnp.where(count_ge(_ordered_bits_to_float(cand)) >= k, cand, res)

    thr = _ordered_bits_to_float(lax.fori_loop(0, 32, body, jnp.zeros(shape, I32)))
    return jnp.where(thr != thr, -jnp.inf, thr)


def _tie_cutoff(count_tie_lt, shape, need, nbits):
    def body(bi, x):
        cand = x | lax.shift_left(jnp.int32(1), jnp.int32(nbits - 1) - bi)
        return jnp.where(count_tie_lt(cand) < need, cand, x)

    return lax.fori_loop(0, nbits, body, jnp.zeros(shape, I32))


def _fold_lanes(x):
    acc = x[:, 0:LANES]
    for j in range(1, x.shape[1] // LANES):
        acc = acc + x[:, j * LANES:(j + 1) * LANES]
    return acc


def _dot_nt(a, b):
    return lax.dot_general(a, b, (((1,), (1,)), ((), ())), preferred_element_type=F32)


def _dsa_prompt_kernel(qa_ref, qi_ref, kwq_ref, kw_ref, ka_ref, va_ref, o_ref,
                       ka_s, kb_s, kbf_s, vt_s, sc_s, dist_s, xcut_s, qbf_s, m_s, l_s, acc_s, lg_s, p_s, al_s,
                       *, cfg, tq, tk, n_sel):
    hd, idim, ni, ha = cfg.head_dim, cfg.idx_dim, cfg.n_idx_heads, cfg.n_heads_a
    s_total = sc_s.shape[0]
    i = pl.program_id(1)
    t0 = i * tq
    nk = (t0 + tq + tk - 1) // tk

    @pl.when(i == 0)
    def _stage():
        kw = kw_ref[0]
        lane = lax.broadcasted_iota(I32, kw.shape, 1)
        k_lo = jnp.where(lane < idim, kw, 0.0)
        ka_s[...] = k_lo.astype(BF16)
        kb_s[...] = pltpu.roll(k_lo, idim, 1).astype(BF16)
        kbf_s[...] = ka_ref[0].astype(BF16)
        for c in range(s_total // tk):
            vt_s[:, c * tk:(c + 1) * tk] = va_ref[0, c * tk:(c + 1) * tk, :].T.astype(BF16)

    w_t = (kwq_ref[0] * (float(ni * idim) ** -0.5)).T
    qpos = t0 + lax.broadcasted_iota(I32, (tk, tq), 1)
    krow = lax.broadcasted_iota(I32, (tk, tq), 0)

    def score_chunk(c, carry):
        ks = pl.multiple_of(c * tk, tk)
        k_even = ka_s[pl.ds(ks, tk), :]
        k_odd = kb_s[pl.ds(ks, tk), :]
        acc = jnp.zeros((tk, tq), F32)
        for p in range(ni // 2):
            qp = qi_ref[0, :, p * LANES:(p + 1) * LANES].astype(BF16)
            for par, kk in ((0, k_even), (1, k_odd)):
                h = 2 * p + par
                acc = acc + w_t[idim + h:idim + h + 1, :] * jnp.maximum(_dot_nt(kk, qp), 0.0)
        sc_s[pl.ds(ks, tk), :] = jnp.where(ks + krow <= qpos, acc, -jnp.inf)
        return carry

    lax.fori_loop(0, nk, score_chunk, 0)

    def count(pred):
        def body(c, cnt):
            ks = pl.multiple_of(c * tk, tk)
            hit = pred(sc_s[pl.ds(ks, tk), :], ks + krow).astype(I32)
            return cnt + jnp.sum(hit.reshape(tk // 8, 8, tq), axis=0)

        cnt = lax.fori_loop(0, nk, body, jnp.zeros((8, tq), I32))
        return jnp.sum(cnt, axis=0, keepdims=True)

    thr = _kth_largest(lambda t: count(lambda s, k: s >= t), (1, tq), n_sel)

    xcut_s[...] = jnp.full(xcut_s.shape, s_total, I32)

    @pl.when(jnp.max(count(lambda s, k: s >= thr)) > n_sel)
    def _ties():
        need = n_sel - count(lambda s, k: s > thr)
        x = _tie_cutoff(lambda cand: count(lambda s, k: (s == thr) & (k < cand)), (1, tq), need,
                        max(1, (s_total - 1).bit_length()))
        xcut_s[...] = jnp.broadcast_to(x, xcut_s.shape)

    xcut = xcut_s[0:1, :]

    def mask_chunk(c, carry):
        ks = pl.multiple_of(c * tk, tk)
        kpos = ks + krow
        s = sc_s[pl.ds(ks, tk), :]
        sel = ((s > thr) | ((s == thr) & (kpos <= xcut))) & (kpos <= qpos)
        dist_s[pl.ds(ks, tk), :] = jnp.where(sel, (qpos - kpos).astype(F32), -NEG_BIG)
        return carry

    lax.fori_loop(0, nk, mask_chunk, 0)
    scale = float(hd) ** -0.5

    for h in range(ha):
        qbf_s[h] = qa_ref[0, :, h * hd:(h + 1) * hd].astype(BF16)
    m_s[...] = jnp.full(m_s.shape, NEG_BIG, F32)
    l_s[...] = jnp.zeros(l_s.shape, F32)
    acc_s[...] = jnp.zeros(acc_s.shape, F32)

    def att_chunk(c, carry):
        ks = pl.multiple_of(c * tk, tk)
        kc = kbf_s[pl.ds(ks, tk), :]
        vt = vt_s[:, pl.ds(ks, tk)]
        dist = dist_s[pl.ds(ks, tk), :]
        for h in range(ha):
            lg_s[h] = _dot_nt(kc, qbf_s[h])
        for h in range(ha):
            slope = float(2.0 ** (-8.0 * (h + 1) / ha))
            lg = lg_s[h] * scale - slope * dist
            m = m_s[h]
            m_new = jnp.maximum(m, jnp.max(lg, axis=0, keepdims=True))
            p = jnp.exp(lg - m_new)
            alpha = jnp.exp(m - m_new)
            l_s[h] = alpha * l_s[h] + jnp.sum(p, axis=0, keepdims=True)
            p_s[h] = p.astype(BF16)
            al_s[h] = alpha
            m_s[h] = m_new
        for h in range(ha):
            acc_s[h] = al_s[h] * acc_s[h] + jnp.dot(vt, p_s[h], preferred_element_type=F32)
        return carry

    lax.fori_loop(0, nk, att_chunk, 0)
    for h in range(ha):
        o_ref[0, :, h * hd:(h + 1) * hd] = (acc_s[h] / l_s[h]).T.astype(o_ref.dtype)


def _dsa_prompt(u3, cfg):
    b, s, _ = u3.shape
    off, _ = _layout(cfg)
    tq, tk = min(cfg.tq_a, s), min(cfg.tk_a, s)
    n_sel = min(cfg.topk_max, s // 4)
    aw = cfg.n_heads_a * cfg.head_dim
    iw = cfg.n_idx_heads * cfg.idx_dim
    assert 2 * cfg.idx_dim == LANES and s % tk == 0 and s % tq == 0 and tk >= n_sel
    cb = lambda name, w: off[name][0] // w
    kern = functools.partial(_dsa_prompt_kernel, cfg=cfg, tq=tq, tk=tk, n_sel=n_sel)
    return pl.pallas_call(
        kern,
        out_shape=jax.ShapeDtypeStruct((b, s, aw), BF16),
        grid=(b, s // tq),
        in_specs=[pl.BlockSpec((1, tq, aw), lambda bi, i: (bi, i, cb("qa", aw))),
                  pl.BlockSpec((1, tq, iw), lambda bi, i: (bi, i, cb("qi", iw))),
                  pl.BlockSpec((1, tq, LANES), lambda bi, i: (bi, i, cb("kiwi", LANES))),
                  pl.BlockSpec((1, s, LANES), lambda bi, i: (bi, 0, cb("kiwi", LANES))),
                  pl.BlockSpec((1, s, LANES), lambda bi, i: (bi, 0, cb("ka", LANES))),
                  pl.BlockSpec((1, s, LANES), lambda bi, i: (bi, 0, cb("va", LANES)))],
        out_specs=pl.BlockSpec((1, tq, aw), lambda bi, i: (bi, i, 0)),
        scratch_shapes=[pltpu.VMEM((s, LANES), BF16), pltpu.VMEM((s, LANES), BF16),
                        pltpu.VMEM((s, LANES), BF16), pltpu.VMEM((cfg.head_dim, s), BF16),
                        pltpu.VMEM((s, tq), F32), pltpu.VMEM((s, tq), F32), pltpu.VMEM((8, tq), I32),
                        pltpu.VMEM((cfg.n_heads_a, tq, cfg.head_dim), BF16),
                        pltpu.VMEM((cfg.n_heads_a, 1, tq), F32), pltpu.VMEM((cfg.n_heads_a, 1, tq), F32),
                        pltpu.VMEM((cfg.n_heads_a, cfg.head_dim, tq), F32),
                        pltpu.VMEM((cfg.n_heads_a, tk, tq), F32), pltpu.VMEM((cfg.n_heads_a, tk, tq), BF16),
                        pltpu.VMEM((cfg.n_heads_a, 1, tq), F32)],
        compiler_params=_cparams(("arbitrary", "arbitrary")),
        name="dsa_prompt",
    )(u3, u3, u3, u3, u3, u3)


def _sb_terms(q, kc, scale):
    z = _dot_nt(q, kc) * scale
    lk = -(jnp.maximum(z, 0.0) + jnp.log(1.0 + jnp.exp(-jnp.abs(z))))
    return z, lk


def _suffix_in_chunk(lk, upper):
    hi = lk.astype(BF16)
    lo = (lk - hi.astype(F32)).astype(BF16)
    return (jnp.dot(hi, upper, preferred_element_type=F32) + jnp.dot(lo, upper, preferred_element_type=F32))


def _strict_lower_ones(n):
    return (lax.broadcasted_iota(I32, (n, n), 0) > lax.broadcasted_iota(I32, (n, n), 1)).astype(BF16)


def _sb_prompt_kernel(q_ref, k_ref, v_ref, o_ref, kbf_s, vbf_s, *, t, hd, rep):
    i = pl.program_id(2)

    @pl.when(i == 0)
    def _stage():
        kbf_s[...] = k_ref[0].astype(BF16)
        vbf_s[...] = v_ref[0].astype(BF16)

    qs = [q_ref[0, :, r * hd:(r + 1) * hd].astype(BF16) for r in range(rep)]
    scale = float(hd) ** -0.5
    upper = _strict_lower_ones(t)
    before = lax.broadcasted_iota(I32, (t, t), 1) < lax.broadcasted_iota(I32, (t, t), 0)

    ks = pl.multiple_of(i * t, t)
    kc, vc = kbf_s[pl.ds(ks, t), :], vbf_s[pl.ds(ks, t), :]
    state = []
    for q in qs:
        z, lk = _sb_terms(q, kc, scale)
        lkm = jnp.where(before, lk, 0.0)
        tail = _suffix_in_chunk(lkm, upper)
        w = jnp.where(before, jnp.exp(z + lk + tail), 0.0)
        state.append(jnp.sum(lkm, axis=-1, keepdims=True))
        state.append(jnp.dot(w.astype(BF16), vc, preferred_element_type=F32))

    def live(st):
        top = st[0]
        for r in range(1, rep):
            top = jnp.maximum(top, st[2 * r])
        return jnp.max(top) > SB_DEAD_TAIL

    def body(loop):
        j, _, st = loop
        ks = pl.multiple_of((i - 1 - j) * t, t)
        kc, vc = kbf_s[pl.ds(ks, t), :], vbf_s[pl.ds(ks, t), :]
        out = []
        for r, q in enumerate(qs):
            carry, acc = st[2 * r], st[2 * r + 1]
            z, lk = _sb_terms(q, kc, scale)
            tail = carry + _suffix_in_chunk(lk, upper)
            w = jnp.exp(z + lk + tail)
            out.append(carry + jnp.sum(lk, axis=-1, keepdims=True))
            out.append(acc + jnp.dot(w.astype(BF16), vc, preferred_element_type=F32))
        return j + 1, live(out), tuple(out)

    _, _, st = lax.while_loop(lambda loop: (loop[0] < i) & loop[1], body, (jnp.int32(0), live(state), tuple(state)))
    for r in range(rep):
        o_ref[0, :, r * hd:(r + 1) * hd] = st[2 * r + 1].astype(o_ref.dtype)


def _sb_prompt(u3, cfg):
    b, s, _ = u3.shape
    off, _ = _layout(cfg)
    hd, hb, g = cfg.head_dim, cfg.n_heads_b, cfg.n_kv_b
    rep = hb // g
    t = min(cfg.t_b, s)
    assert s % t == 0
    qw = rep * hd
    assert off["qb"][0] % qw == 0
    qb0, kb0, vb0 = off["qb"][0] // qw, off["kb"][0] // hd, off["vb"][0] // hd
    kern = functools.partial(_sb_prompt_kernel, t=t, hd=hd, rep=rep)
    return pl.pallas_call(
        kern,
        out_shape=jax.ShapeDtypeStruct((b, s, hb * hd), BF16),
        grid=(b, g, s // t),
        in_specs=[pl.BlockSpec((1, t, qw), lambda bi, gi, i: (bi, i, qb0 + gi)),
                  pl.BlockSpec((1, s, hd), lambda bi, gi, i: (bi, 0, kb0 + gi)),
                  pl.BlockSpec((1, s, hd), lambda bi, gi, i: (bi, 0, vb0 + gi))],
        out_specs=pl.BlockSpec((1, t, qw), lambda bi, gi, i: (bi, i, gi)),
        scratch_shapes=[pltpu.VMEM((s, hd), BF16), pltpu.VMEM((s, hd), BF16)],
        compiler_params=_cparams(("arbitrary", "arbitrary", "arbitrary")),
        name="sb_prompt",
    )(u3, u3, u3)


def _page_specs(n, block, page_of):
    zeros = (0,) * (len(block) - 1)
    return [pl.BlockSpec((None,) + tuple(block[1:]),
                         functools.partial(lambda b, j, pt, r: (page_of(b, j, r, pt),) + zeros, r=r))
            for r in range(n)]


def _dsa_sample_score_kernel(pt_ref, q_ref, w_ref, *rest, pg):
    k_refs, o_ref = rest[:pg], rest[pg]
    q = q_ref[0]
    w = w_ref[0]
    for r in range(pg):
        d = _dot3(q, k_refs[r][...])
        o_ref[0, 0, r:r + 1, :] = jnp.sum(w * jnp.maximum(d, 0.0), axis=0, keepdims=True)


def _dsa_sample_scores(qi3, wi3, cache_ik_t, page_table, cfg):
    db, ni, idim = qi3.shape
    pool, _, ps = cache_ik_t.shape
    npg = page_table.shape[1]
    pg = min(cfg.pages_per_step, npg)
    assert npg % pg == 0 and ps == LANES
    page_of = lambda b, j, r, pt: pt[b * npg + j * pg + r]
    grid_spec = pltpu.PrefetchScalarGridSpec(
        num_scalar_prefetch=1,
        grid=(db, npg // pg),
        in_specs=[pl.BlockSpec((1, ni, idim), lambda b, j, pt: (b, 0, 0)),
                  pl.BlockSpec((1, ni, 1), lambda b, j, pt: (b, 0, 0))]
                 + _page_specs(pg, (1, idim, ps), page_of),
        out_specs=pl.BlockSpec((1, 1, pg, ps), lambda b, j, pt: (b, j, 0, 0)),
    )
    out = pl.pallas_call(
        functools.partial(_dsa_sample_score_kernel, pg=pg),
        out_shape=jax.ShapeDtypeStruct((db, npg // pg, pg, ps), F32),
        grid_spec=grid_spec,
        compiler_params=_cparams(("arbitrary", "arbitrary")),
        name="dsa_sample_scores",
    )(page_table.reshape(-1), qi3, wi3, *([cache_ik_t] * pg))
    return out


def _dsa_sample_select_kernel(sc_ref, qi_ref, kw_ref, bias_ref, bnew_ref, *, cfg, n_sel, chunk):
    idim, ni = cfg.idx_dim, cfg.n_idx_heads
    db, p = sc_ref.shape
    nchunk = p // chunk
    kw = kw_ref[...]
    ki = kw[:, 0:idim]
    s_new = jnp.zeros((db, 1), F32)
    for h in range(ni):
        d = jnp.sum(qi_ref[:, h * idim:(h + 1) * idim] * ki, axis=-1, keepdims=True)
        s_new = s_new + (kw[:, idim + h:idim + h + 1] * (float(ni * idim) ** -0.5)) * jnp.maximum(d, 0.0)
    kn1 = s_new
    idx0 = lax.broadcasted_iota(I32, (db, chunk), 1)

    def count(pred, new_hit):
        cnt = jnp.zeros((db, LANES), I32)
        for c in range(nchunk):
            cnt = cnt + _fold_lanes(pred(sc_ref[:, c * chunk:(c + 1) * chunk], c * chunk + idx0).astype(I32))
        return jnp.sum(cnt, axis=-1, keepdims=True) + new_hit.astype(I32)

    thr = _kth_largest(lambda t: count(lambda k, i: k >= t, kn1 >= t), (db, 1), n_sel)
    need = n_sel - count(lambda k, i: k > thr, kn1 > thr)
    xcut = _tie_cutoff(lambda x: count(lambda k, i: (k == thr) & (i < x), (kn1 == thr) & (p < x)),
                       (db, 1), need, max(1, p.bit_length()))
    for c in range(nchunk):
        k = sc_ref[:, c * chunk:(c + 1) * chunk]
        sel = (k > thr) | ((k == thr) & (c * chunk + idx0 <= xcut))
        bias_ref[:, c * chunk:(c + 1) * chunk] = jnp.where(sel, 0.0, NEG_BIG)
    sel_new = (kn1 > thr) | ((kn1 == thr) & (p <= xcut))
    bnew_ref[...] = jnp.broadcast_to(jnp.where(sel_new, 0.0, NEG_BIG), (db, LANES))


def _dsa_sample_select(scores, u_s, cfg):
    db, p = scores.shape
    off, _ = _layout(cfg)
    iw = cfg.n_idx_heads * cfg.idx_dim
    n_sel = min(cfg.topk_max, (p + 1) // 4)
    chunk = min(2048, p)
    assert p % chunk == 0 and p + 1 >= n_sel
    kern = functools.partial(_dsa_sample_select_kernel, cfg=cfg, n_sel=n_sel, chunk=chunk)
    return pl.pallas_call(
        kern,
        out_shape=(jax.ShapeDtypeStruct((db, p), F32), jax.ShapeDtypeStruct((db, LANES), F32)),
        grid=(1,),
        in_specs=[pl.BlockSpec((db, p), lambda i: (0, 0)),
                  pl.BlockSpec((db, iw), lambda i: (0, off["qi"][0] // iw)),
                  pl.BlockSpec((db, LANES), lambda i: (0, off["kiwi"][0] // LANES))],
        out_specs=(pl.BlockSpec((db, p), lambda i: (0, 0)), pl.BlockSpec((db, LANES), lambda i: (0, 0))),
        compiler_params=_cparams(("arbitrary",)),
        name="dsa_sample_select",
    )(scores, u_s, u_s)


def _dsa_sample_attend_kernel(pt_ref, q_ref, bias_ref, bnew_ref, knew_ref, vnew_ref, *rest,
                              pg, cfg, past):
    k_refs, v_refs = rest[:pg], rest[pg:2 * pg]
    o_ref, m_s, l_s, acc_s = rest[2 * pg:]
    hd, ha = cfg.head_dim, cfg.n_heads_a
    j = pl.program_id(1)
    scale = float(hd) ** -0.5

    @pl.when(j == 0)
    def _init():
        m_s[...] = jnp.full(m_s.shape, NEG_BIG, F32)
        l_s[...] = jnp.zeros(l_s.shape, F32)
        acc_s[...] = jnp.zeros(acc_s.shape, F32)

    q = q_ref[0]
    hidx = lax.broadcasted_iota(I32, (ha, 1), 0)
    slopes = jnp.zeros((ha, 1), F32)
    for h in range(ha):
        slopes = jnp.where(hidx == h, float(2.0 ** (-8.0 * (h + 1) / ha)), slopes)

    def update(lg, bias, vs):
        sel = jnp.broadcast_to(bias == 0.0, lg.shape)
        lg = lg + bias
        m = m_s[...]
        m_new = jnp.maximum(m, jnp.max(lg, axis=-1, keepdims=True))
        p = jnp.where(sel, jnp.exp(lg - m_new), 0.0)
        alpha = jnp.exp(m - m_new)
        l_s[...] = alpha * l_s[...] + jnp.sum(p, axis=-1, keepdims=True)
        w = vs[0].shape[0]
        pv = _dot3(p[:, 0:w], vs[0])
        for r in range(1, len(vs)):
            pv = pv + _dot3(p[:, r * w:(r + 1) * w], vs[r])
        acc_s[...] = alpha * acc_s[...] + pv
        m_s[...] = m_new

    n = pg * LANES
    pos = j * n + lax.broadcasted_iota(I32, (1, n), 1)
    lg = jnp.concatenate([_dot3(q, k_refs[r][...], _dot_nt) for r in range(pg)], axis=1)
    lg = lg * scale - slopes * (past - pos).astype(F32)
    update(lg, bias_ref[0, 0], [v_refs[r][...] for r in range(pg)])

    @pl.when(j == pl.num_programs(1) - 1)
    def _fin():
        kn = jnp.broadcast_to(knew_ref[0], (16, hd))
        vn = jnp.broadcast_to(vnew_ref[0], (16, hd))
        lg = _dot3(q, kn, _dot_nt) * scale
        first = lax.broadcasted_iota(I32, (1, 16), 1) == 0
        update(lg, jnp.where(first, bnew_ref[0][:, 0:16], NEG_BIG), [vn])
        o_ref[0] = (acc_s[...] / l_s[...]).astype(o_ref.dtype)


def _dsa_sample_attend(qa3, bias, bias_new, u_s3, cache_k, cache_v, page_table, cfg):
    db, ha, hd = qa3.shape
    npg = page_table.shape[1]
    pg = min(cfg.pages_per_step, npg)
    ps = cache_k.shape[1]
    bias4 = bias.reshape(db, npg // pg, 1, pg * ps)
    off, _ = _layout(cfg)
    page_of = lambda b, j, r, pt: pt[b * npg + j * pg + r]
    kern = functools.partial(_dsa_sample_attend_kernel, pg=pg, cfg=cfg, past=npg * ps)
    grid_spec = pltpu.PrefetchScalarGridSpec(
        num_scalar_prefetch=1,
        grid=(db, npg // pg),
        in_specs=[pl.BlockSpec((1, ha, hd), lambda b, j, pt: (b, 0, 0)),
                  pl.BlockSpec((1, 1, 1, pg * ps), lambda b, j, pt: (b, j, 0, 0)),
                  pl.BlockSpec((1, 1, LANES), lambda b, j, pt: (b, 0, 0)),
                  pl.BlockSpec((1, 1, hd), lambda b, j, pt: (b, 0, off["ka"][0] // hd)),
                  pl.BlockSpec((1, 1, hd), lambda b, j, pt: (b, 0, off["va"][0] // hd))]
                 + _page_specs(pg, (1, ps, hd), page_of) + _page_specs(pg, (1, ps, hd), page_of),
        out_specs=pl.BlockSpec((1, ha, hd), lambda b, j, pt: (b, 0, 0)),
        scratch_shapes=[pltpu.VMEM((ha, 1), F32), pltpu.VMEM((ha, 1), F32), pltpu.VMEM((ha, hd), F32)],
    )
    return pl.pallas_call(
        kern,
        out_shape=jax.ShapeDtypeStruct((db, ha, hd), F32),
        grid_spec=grid_spec,
        compiler_params=_cparams(("arbitrary", "arbitrary")),
        name="dsa_sample_attend",
    )(page_table.reshape(-1), qa3, bias4, bias_new.reshape(db, 1, LANES),
      u_s3, u_s3, *([cache_k] * pg), *([cache_v] * pg))


def _sb_sample_kernel(pt_ref, q_ref, k_hbm, v_hbm, o_ref, kbuf, vbuf, sem, *, pg, npg, cfg):
    hd, hb, g = cfg.head_dim, cfg.n_heads_b, cfg.n_kv_b
    rep = hb // g
    b = pl.program_id(0)
    nchunks = npg // pg
    scale = float(hd) ** -0.5

    def page_copies(c, slot):
        out = []
        for r in range(pg):
            page = pt_ref[b * npg + (npg - 1 - (c * pg + r))]
            out.append(pltpu.make_async_copy(k_hbm.at[page], kbuf.at[slot, r], sem.at[0, slot]))
            out.append(pltpu.make_async_copy(v_hbm.at[page], vbuf.at[slot, r], sem.at[1, slot]))
        return out

    q = q_ref[0]
    qg = [q[:, gi * hd:(gi + 1) * hd] for gi in range(g)]
    upper = _strict_lower_ones(LANES)

    for cp in page_copies(0, 0):
        cp.start()

    def body(loop):
        c, _, carry, accs = loop
        slot = c % 2

        @pl.when(c + 1 < nchunks)
        def _prefetch():
            for cp in page_copies(c + 1, 1 - slot):
                cp.start()

        for cp in page_copies(c, slot):
            cp.wait()
        zs = []
        for r in range(pg):
            z = _dot3(qg[0], kbuf[slot, r, :, 0, :], _dot_nt)
            for gi in range(1, g):
                z = z + _dot3(qg[gi], kbuf[slot, r, :, gi, :], _dot_nt)
            zs.append(z * scale)
        z_all = jnp.concatenate(zs, axis=0)
        lk_all = -(jnp.maximum(z_all, 0.0) + jnp.log(1.0 + jnp.exp(-jnp.abs(z_all))))
        tail_in = _suffix_in_chunk(lk_all, upper)
        tot = jnp.sum(lk_all, axis=-1, keepdims=True)
        accs = list(accs)
        for r in range(pg):
            sl = slice(r * hb, (r + 1) * hb)
            w = jnp.exp(z_all[sl] + lk_all[sl] + (carry + tail_in[sl]))
            for gi in range(g):
                accs[gi] = accs[gi] + _dot3(w, vbuf[slot, r, :, gi, :])
            carry = carry + tot[sl]
        return c + 1, jnp.max(carry) > SB_DEAD_TAIL, carry, tuple(accs)

    init = (jnp.int32(0), jnp.bool_(True), jnp.zeros((hb, 1), F32), tuple(jnp.zeros((hb, hd), F32) for _ in range(g)))
    c_end, _, _, accs = lax.while_loop(lambda loop: (loop[0] < nchunks) & loop[1], body, init)

    @pl.when(c_end < nchunks)
    def _drain():
        for cp in page_copies(c_end, c_end % 2):
            cp.wait()

    head_group = lax.broadcasted_iota(I32, (hb, hd), 0) // rep
    out = jnp.zeros((hb, hd), F32)
    for gi in range(g):
        out = out + jnp.where(head_group == gi, accs[gi], 0.0)
    o_ref[0] = out.astype(o_ref.dtype)


def _sb_sample(q_bd, cache_k, cache_v, page_table, cfg):
    db, hb, gw = q_bd.shape
    npg = page_table.shape[1]
    _, ps, g, hd = cache_k.shape
    pg = min(cfg.sb_pages_per_chunk, npg)
    assert npg % pg == 0 and ps == LANES and g * hd == gw
    grid_spec = pltpu.PrefetchScalarGridSpec(
        num_scalar_prefetch=1,
        grid=(db,),
        in_specs=[pl.BlockSpec((1, hb, gw), lambda b, pt: (b, 0, 0)),
                  pl.BlockSpec(memory_space=pl.ANY), pl.BlockSpec(memory_space=pl.ANY)],
        out_specs=pl.BlockSpec((1, hb, cfg.head_dim), lambda b, pt: (b, 0, 0)),
        scratch_shapes=[pltpu.VMEM((2, pg, ps, g, hd), F32), pltpu.VMEM((2, pg, ps, g, hd), F32),
                        pltpu.SemaphoreType.DMA((2, 2))],
    )
    return pl.pallas_call(
        functools.partial(_sb_sample_kernel, pg=pg, npg=npg, cfg=cfg),
        out_shape=jax.ShapeDtypeStruct((db, hb, cfg.head_dim), F32),
        grid_spec=grid_spec,
        compiler_params=_cparams(("arbitrary",)),
        name="sb_sample",
    )(page_table.reshape(-1), q_bd, cache_k, cache_v)


def _merge_kernel(x_ref, oa_ref, ob_ref, ga_ref, gb_ref, m_ref, gf_ref, wua_ref, wub_ref, wo_ref, wr_ref, br_ref,
                  x1_ref, h2_ref, rt_ref, *, cfg):
    ng, epg = cfg.n_groups, cfg.epg
    sig = lambda v: 1.0 / (1.0 + jnp.exp(-v))

    def mm(a, w_ref):
        dot = lambda p, q: jnp.dot(p, q, preferred_element_type=F32)
        if w_ref.shape[0] == 1:
            return dot(a.astype(BF16), w_ref[0])
        a_hi, a_lo = _split(a)
        return dot(a_hi, w_ref[0]) + dot(a_lo, w_ref[0]) + dot(a_hi, w_ref[1])

    merged = sig(ga_ref[...]) * mm(oa_ref[...], wua_ref) + sig(gb_ref[...]) * mm(ob_ref[...], wub_ref)
    x1 = x_ref[0] + m_ref[0, 2] * mm(merged, wo_ref)
    x1_ref[...] = x1
    h2 = x1 * lax.rsqrt(jnp.mean(x1 * x1, axis=-1, keepdims=True) + RMS_EPS) * gf_ref[...]
    h2 = h2 * (1.0 + m_ref[0, 4]) + m_ref[0, 3]
    h2_ref[...] = h2
    logits = mm(h2, wr_ref) + br_ref[...]

    lane = lax.broadcasted_iota(I32, logits.shape, 1).astype(F32)
    first = lambda hit: jnp.min(jnp.where(hit, lane, float(LANES)), axis=-1, keepdims=True)
    in_g = lane < ng
    gl = jnp.where(in_g, logits, -jnp.inf)
    gmax = jnp.max(gl, axis=-1, keepdims=True)
    gidx = first(gl == gmax)
    p_g = 1.0 / jnp.sum(jnp.where(in_g, jnp.exp(logits - gmax), 0.0), axis=-1, keepdims=True)
    lo = ng + gidx * epg
    el = jnp.where((lane >= lo) & (lane < lo + epg), logits, -jnp.inf)
    v1 = jnp.max(el, axis=-1, keepdims=True)
    i1 = first(el == v1)
    el2 = jnp.where(lane == i1, -jnp.inf, el)
    v2 = jnp.max(el2, axis=-1, keepdims=True)
    i2 = first(el2 == v2)
    e21 = jnp.exp(v2 - v1)
    w1 = 1.0 / (1.0 + e21)
    w2 = e21 / (1.0 + e21)
    rt = jnp.where(lane == 0, i1 - ng, jnp.where(lane == 1, i2 - ng,
         jnp.where(lane == 2, p_g * w1, jnp.where(lane == 3, p_g * w2, 0.0))))
    rt_ref[...] = rt


def _merge(x3, oa, ob, u, mod, g_ffn, wua, wub, wo, wr, br, cfg, tm):
    bx, t, d = x3.shape
    m = bx * t
    off, _ = _layout(cfg)
    aw, bw = oa.shape[1], ob.shape[1]
    r = mod.shape[2]
    nt = t // tm
    const = lambda shape: pl.BlockSpec(shape, lambda b, i: (0,) * len(shape), pipeline_mode=pl.Buffered(1))
    return pl.pallas_call(
        functools.partial(_merge_kernel, cfg=cfg),
        out_shape=(jax.ShapeDtypeStruct((m, d), F32), jax.ShapeDtypeStruct((m, d), F32),
                   jax.ShapeDtypeStruct((m, LANES), F32)),
        grid=(bx, nt),
        in_specs=[pl.BlockSpec((1, tm, d), lambda b, i: (b, i, 0)),
                  pl.BlockSpec((tm, aw), lambda b, i: (b * nt + i, 0)),
                  pl.BlockSpec((tm, bw), lambda b, i: (b * nt + i, 0)),
                  pl.BlockSpec((tm, d), lambda b, i: (b * nt + i, off["ga"][0] // d)),
                  pl.BlockSpec((tm, d), lambda b, i: (b * nt + i, off["gb"][0] // d)),
                  pl.BlockSpec((1, 6, r, d), lambda b, i: (b, 0, 0, 0)),
                  const((1, d)), const(wua.shape), const(wub.shape), const(wo.shape), const((2, d, LANES)),
                  const((1, LANES))],
        out_specs=(pl.BlockSpec((tm, d), lambda b, i: (b * nt + i, 0)),
                   pl.BlockSpec((tm, d), lambda b, i: (b * nt + i, 0)),
                   pl.BlockSpec((tm, LANES), lambda b, i: (b * nt + i, 0))),
        compiler_params=_cparams(("arbitrary", "arbitrary")),
        name="merge_out_router",
    )(x3, oa, ob, u, u, mod, g_ffn.reshape(1, d), wua, wub, wo, wr, br)


def _moe_kernel(be_ref, nb_ref, tok_ref, tokn_ref, h_hbm, wg_ref, wu_ref, wd_ref, y_ref,
                xbuf, sem, wg_s, wu_s, wd_s, *, rows):
    i = pl.program_id(0)
    nb = nb_ref[0]

    def row_copy(tok, slot, r):
        return pltpu.make_async_copy(h_hbm.at[pl.ds(tok, 1)], xbuf.at[slot, pl.ds(r, 1)], sem.at[slot])

    def issue(idx_ref, slot):
        def body(r, c):
            row_copy(idx_ref[0, 0, r], slot, r).start()
            return c

        lax.fori_loop(0, rows, body, 0, unroll=8)

    def wait_slot(slot):
        pltpu.make_async_copy(h_hbm.at[pl.ds(0, rows)], xbuf.at[slot], sem.at[slot]).wait()

    @pl.when(i == 0)
    def _prime():
        issue(tok_ref, 0)

    @pl.when(i + 1 < nb)
    def _prefetch():
        issue(tokn_ref, (i + 1) % 2)

    @pl.when(i < nb)
    def _compute():
        slot = i % 2
        wait_slot(slot)

        @pl.when((i == 0) | (be_ref[i] != be_ref[jnp.maximum(i - 1, 0)]))
        def _cast():
            wg_s[...] = wg_ref[...].astype(BF16)
            wu_s[...] = wu_ref[...].astype(BF16)
            wd_s[...] = wd_ref[...].astype(BF16)

        x = xbuf[slot].astype(BF16)
        gate = jnp.dot(x, wg_s[...], preferred_element_type=F32)
        up = jnp.dot(x, wu_s[...], preferred_element_type=F32)
        act = (gate * (1.0 / (1.0 + jnp.exp(-gate))) * up).astype(BF16)
        y_ref[...] = jnp.dot(act, wd_s[...], preferred_element_type=F32)

    @pl.when(i >= nb)
    def _idle():
        y_ref[...] = jnp.zeros(y_ref.shape, F32)


def _moe_experts(h2_all, slot_token, block_expert, n_used, w_gate_e, w_up_e, w_down_e, rows):
    n_blocks = block_expert.shape[0]
    e, d, f = w_gate_e.shape
    tok3 = slot_token.reshape(n_blocks, 1, rows)
    grid_spec = pltpu.PrefetchScalarGridSpec(
        num_scalar_prefetch=2,
        grid=(n_blocks,),
        in_specs=[pl.BlockSpec((1, 1, rows), lambda i, be, nb: (i, 0, 0), memory_space=pltpu.SMEM),
                  pl.BlockSpec((1, 1, rows), lambda i, be, nb: (jnp.minimum(i + 1, n_blocks - 1), 0, 0),
                               memory_space=pltpu.SMEM),
                  pl.BlockSpec(memory_space=pl.ANY),
                  pl.BlockSpec((None, d, f), lambda i, be, nb: (be[i], 0, 0)),
                  pl.BlockSpec((None, d, f), lambda i, be, nb: (be[i], 0, 0)),
                  pl.BlockSpec((None, f, d), lambda i, be, nb: (be[i], 0, 0))],
        out_specs=pl.BlockSpec((rows, d), lambda i, be, nb: (i, 0)),
        scratch_shapes=[pltpu.VMEM((2, rows, d), F32), pltpu.SemaphoreType.DMA((2,)),
                        pltpu.VMEM((d, f), BF16), pltpu.VMEM((d, f), BF16), pltpu.VMEM((f, d), BF16)],
    )
    return pl.pallas_call(
        functools.partial(_moe_kernel, rows=rows),
        out_shape=jax.ShapeDtypeStruct((n_blocks * rows, d), F32),
        grid_spec=grid_spec,
        compiler_params=_cparams(("arbitrary",)),
        name="moe_experts",
    )(block_expert, n_used, tok3, tok3, h2_all, w_gate_e, w_up_e, w_down_e)


def _rank_kernel(rt_ref, base_ref, dest_ref, tot_ref, base_s, *, tm):
    @pl.when(pl.program_id(0) == 0)
    def _init():
        base_s[...] = base_ref[...]

    rt = rt_ref[...]
    lane = lax.broadcasted_iota(I32, rt.shape, 1).astype(F32)
    o1 = (lane == rt[:, 0:1]).astype(F32)
    o2 = (lane == rt[:, 1:2]).astype(F32)
    lower = _strict_lower_ones(tm)
    p1 = jnp.dot(lower, o1.astype(BF16), preferred_element_type=F32)
    p2 = jnp.dot(lower, o2.astype(BF16), preferred_element_type=F32)
    c1 = jnp.sum(o1, axis=0, keepdims=True)
    c2 = jnp.sum(o2, axis=0, keepdims=True)
    base = base_s[...]
    d1 = jnp.sum((base + p1) * o1, axis=-1, keepdims=True)
    d2 = jnp.sum((base + c1 + p2) * o2, axis=-1, keepdims=True)
    dest_ref[...] = jnp.where(lane == 0, d1, jnp.where(lane == 1, d2, 0.0))
    base_s[...] = base + c1 + c2
    tot_ref[...] = base + c1 + c2


def _rank(rt_pad, base, tm):
    n = rt_pad.shape[0]
    return pl.pallas_call(
        functools.partial(_rank_kernel, tm=tm),
        out_shape=(jax.ShapeDtypeStruct((n, LANES), F32), jax.ShapeDtypeStruct((1, LANES), F32)),
        grid=(n // tm,),
        in_specs=[pl.BlockSpec((tm, LANES), lambda i: (i, 0)), pl.BlockSpec((1, LANES), lambda i: (0, 0))],
        out_specs=(pl.BlockSpec((tm, LANES), lambda i: (i, 0)), pl.BlockSpec((1, LANES), lambda i: (0, 0))),
        scratch_shapes=[pltpu.VMEM((1, LANES), F32)],
        compiler_params=_cparams(("arbitrary",)),
        name="moe_rank",
    )(rt_pad, base)


def _moe_plan(rt_all, n_experts, rows, top_k, tm):
    n_tok = rt_all.shape[0]
    a = n_tok * top_k
    assert top_k == 2 and n_experts <= LANES
    n_blocks = -(-(a + n_experts * (rows - 1)) // rows)
    n_pad = _round_up(n_tok, tm)
    rt_pad = jnp.concatenate([rt_all, jnp.full((n_pad - n_tok, LANES), -1.0, F32)], axis=0)
    _, tot = _rank(rt_pad, jnp.zeros((1, LANES), F32), tm)
    counts = tot[0, :n_experts].astype(I32)
    padded = (counts + rows - 1) // rows * rows
    cum = jnp.cumsum(padded)
    pstart = jnp.zeros((1, LANES), F32).at[0, :n_experts].set((cum - padded).astype(F32))
    dest, _ = _rank(rt_pad, pstart, tm)
    pos = (dest[:n_tok, 0].astype(I32), dest[:n_tok, 1].astype(I32))
    tok = jnp.arange(n_tok, dtype=I32)
    slot_token = jnp.zeros((n_blocks * rows,), I32).at[jnp.concatenate(pos)].set(
        jnp.concatenate([tok, tok]), unique_indices=True)
    starts = jnp.arange(n_blocks, dtype=I32) * rows
    block_expert = jnp.minimum(jnp.sum((cum[None, :] <= starts[:, None]).astype(I32), axis=1), n_experts - 1)
    n_used = (cum[-1] // rows).astype(I32).reshape(1)
    return slot_token, pos, block_expert, n_used


def _combine_kernel(pos_ref, posn_ref, x1_ref, rt_ref, m_ref, gfin_ref, ys_hbm, o_ref, ybuf, sem, *, tm):
    t = pl.program_id(0)
    nt = pl.num_programs(0)

    def row_copy(p, slot, r):
        return pltpu.make_async_copy(ys_hbm.at[pl.ds(p, 1)], ybuf.at[slot, pl.ds(r, 1)], sem.at[slot])

    def issue(idx_ref, slot):
        def body(r, c):
            row_copy(idx_ref[0, 0, r], slot, r).start()
            return c

        lax.fori_loop(0, 2 * tm, body, 0, unroll=8)

    @pl.when(t == 0)
    def _prime():
        issue(pos_ref, 0)

    @pl.when(t + 1 < nt)
    def _prefetch():
        issue(posn_ref, (t + 1) % 2)

    slot = t % 2
    pltpu.make_async_copy(ys_hbm.at[pl.ds(0, 2 * tm)], ybuf.at[slot], sem.at[slot]).wait()

    rt = rt_ref[...]
    moe = rt[:, 2:3] * ybuf[slot, 0:tm, :] + rt[:, 3:4] * ybuf[slot, tm:2 * tm, :]
    x2 = x1_ref[...] + m_ref[0, 5] * moe
    y = x2 * lax.rsqrt(jnp.mean(x2 * x2, axis=-1, keepdims=True) + RMS_EPS)
    o_ref[...] = y * gfin_ref[...]


def _combine(x1, rt, mod, g_final, ys, pos, t_per_b, tm):
    m, d = x1.shape
    nt_all = m // tm
    nt_b = t_per_b // tm
    r = mod.shape[2]
    pos3 = jnp.concatenate([p.reshape(nt_all, 1, tm) for p in pos], axis=2)
    return pl.pallas_call(
        functools.partial(_combine_kernel, tm=tm),
        out_shape=jax.ShapeDtypeStruct((m, d), F32),
        grid=(nt_all,),
        in_specs=[pl.BlockSpec((1, 1, 2 * tm), lambda t: (t, 0, 0), memory_space=pltpu.SMEM),
                  pl.BlockSpec((1, 1, 2 * tm), lambda t: (jnp.minimum(t + 1, nt_all - 1), 0, 0),
                               memory_space=pltpu.SMEM),
                  pl.BlockSpec((tm, d), lambda t: (t, 0)),
                  pl.BlockSpec((tm, LANES), lambda t: (t, 0)),
                  pl.BlockSpec((1, 6, r, d), lambda t: (t // nt_b, 0, 0, 0)),
                  pl.BlockSpec((1, d), lambda t: (0, 0)),
                  pl.BlockSpec(memory_space=pl.ANY)],
        out_specs=pl.BlockSpec((tm, d), lambda t: (t, 0)),
        scratch_shapes=[pltpu.VMEM((2, 2 * tm, d), F32), pltpu.SemaphoreType.DMA((2,))],
        compiler_params=_cparams(("arbitrary",)),
        name="moe_combine",
    )(pos3, pos3, x1, rt, mod, g_final.reshape(1, d), ys)


def _forward(cfg, x_prompt, x_sample, cache_a_k, cache_a_v, cache_idx_k, cache_b_k, cache_b_v,
             page_table, c_prompt, c_sample, w_ada, b_ada, g_mix, g_ffn, g_final, w_in,
             w_up_a, w_up_b, w_out, w_group, b_group, w_router, b_router, w_gate_e, w_up_e, w_down_e):
    depth = w_in.shape[0]
    bp, seq, d = x_prompt.shape
    db, dseq, _ = x_sample.shape
    assert dseq == 1, "sample path is written for single-token decode"
    hd = cfg.head_dim
    off, wpad = _layout(cfg)
    ps = cfg.page_size
    ng, epg = cfg.n_groups, cfg.epg
    n_exp = ng * epg
    col = lambda a, name: a[..., off[name][0]:off[name][0] + off[name][1]]

    xp, xs = x_prompt, x_sample.reshape(1, db, d)
    c_all = jnp.concatenate([c_prompt, c_sample, jnp.zeros((-(bp + db) % 8, d), F32)], axis=0)
    rows_p, rows_s = [], []
    for layer in range(depth):
        mod = _adaln(c_all, w_ada[layer], b_ada[layer])
        mod_p = mod[:bp].reshape(bp, 6, 1, d)
        mod_s = mod[bp:bp + db].reshape(db, 6, d).transpose(1, 0, 2).reshape(1, 6, db, d)
        w_al = _permute_w_in(w_in[layer], cfg)
        def parts(w32):
            hi = w32.astype(BF16)
            return jnp.stack([hi, (w32 - hi.astype(F32)).astype(BF16)])

        wua, wub, wo = parts(w_up_a[layer]), parts(w_up_b[layer]), parts(w_out[layer])
        wr = parts(jnp.concatenate([w_group[layer], w_router[layer].transpose(1, 0, 2).reshape(d, n_exp),
                                    jnp.zeros((d, LANES - ng - n_exp), F32)], axis=1))
        br = jnp.concatenate([b_group[layer], b_router[layer].reshape(-1),
                              jnp.zeros((LANES - ng - n_exp,), F32)]).reshape(1, LANES)

        ts_p = min(256, seq)
        h_p = _rms_mod(xp, g_mix[layer], mod_p, ts_p).reshape(bp * seq, d)
        u_p = _matmul(h_p, w_al, min(cfg.tm_in, bp * seq), cfg.tn_in)
        u_p3 = u_p.reshape(bp, seq, wpad)
        oa_p = _dsa_prompt(u_p3, cfg).reshape(bp * seq, -1)
        ob_p = _sb_prompt(u_p3, cfg).reshape(bp * seq, -1)

        h_s = _rms_mod(xs, g_mix[layer], mod_s, db, out_dtype=F32).reshape(db, d)
        u_s = _matmul(h_s, w_al, db, cfg.tn_in)
        qi3 = col(u_s, "qi").reshape(db, cfg.n_idx_heads, cfg.idx_dim)
        kiwi_s = col(u_s, "kiwi")
        wi3 = (kiwi_s[:, cfg.idx_dim:cfg.idx_dim + cfg.n_idx_heads]
               * (float(cfg.n_idx_heads * cfg.idx_dim) ** -0.5)).reshape(db, cfg.n_idx_heads, 1)
        c_ik = jnp.swapaxes(cache_idx_k[layer], 1, 2)
        c_ak = cache_a_k[layer].reshape(cache_a_k.shape[1], ps, hd)
        c_av = cache_a_v[layer].reshape(cache_a_v.shape[1], ps, hd)
        c_bk, c_bv = cache_b_k[layer], cache_b_v[layer]
        sc4 = _dsa_sample_scores(qi3, wi3, c_ik, page_table, cfg)
        bias, bias_new = _dsa_sample_select(sc4.reshape(db, -1), u_s, cfg)
        qa3 = col(u_s, "qa").reshape(db, cfg.n_heads_a, hd)
        oa_s = _dsa_sample_attend(qa3, bias, bias_new, u_s.reshape(db, 1, wpad), c_ak, c_av,
                                  page_table, cfg).reshape(db, -1)
        rep = cfg.n_heads_b // cfg.n_kv_b
        qb3 = col(u_s, "qb").reshape(db, cfg.n_heads_b, 1, hd)
        gsel = (jnp.arange(cfg.n_heads_b)[:, None] // rep == jnp.arange(cfg.n_kv_b)[None, :])
        q_bd = (qb3 * gsel[None, :, :, None].astype(F32)).reshape(db, cfg.n_heads_b, cfg.n_kv_b * hd)
        ob_s = _sb_sample(q_bd, c_bk, c_bv, page_table, cfg).reshape(db, -1)

        x1_p, h2_p, rt_p = _merge(xp, oa_p, ob_p, u_p, mod_p, g_ffn[layer], wua[:1], wub[:1], wo[:1], wr, br, cfg,
                                  min(cfg.tm_merge, seq))
        x1_s, h2_s, rt_s = _merge(xs, oa_s, ob_s, u_s, mod_s, g_ffn[layer], wua, wub, wo, wr, br, cfg, db)

        h2_all = jnp.concatenate([h2_p, h2_s], axis=0)
        rt_all = jnp.concatenate([rt_p, rt_s], axis=0)
        slot_token, pos, block_expert, n_used = _moe_plan(rt_all, n_exp, cfg.moe_rows, cfg.top_k,
                                                          min(256, bp * seq))
        ys = _moe_experts(h2_all, slot_token, block_expert, n_used,
                          w_gate_e[layer], w_up_e[layer], w_down_e[layer], cfg.moe_rows)
        last = layer == depth - 1
        gfin = g_final if last else jnp.ones_like(g_final)
        assert last, "deeper stacks need an un-normalised combine for inner layers"
        n_p = bp * seq
        y_p = _combine(x1_p, rt_p, mod_p, gfin, ys, tuple(p[:n_p] for p in pos), seq, min(cfg.tm_comb, seq))
        y_s = _combine(x1_s, rt_s, mod_s, gfin, ys, tuple(p[n_p:] for p in pos), db, db)

        pages = lambda a: a.reshape((bp, seq // ps, ps) + a.shape[2:])
        rows_p.append((pages(col(u_p3, "ka").reshape(bp, seq, 1, hd)),
                       pages(col(u_p3, "va").reshape(bp, seq, 1, hd)),
                       pages(col(u_p3, "kiwi")[..., :cfg.idx_dim]),
                       pages(col(u_p3, "kb").reshape(bp, seq, cfg.n_kv_b, hd)),
                       pages(col(u_p3, "vb").reshape(bp, seq, cfg.n_kv_b, hd))))
        rows_s.append((col(u_s, "ka").reshape(db, 1, 1, hd), col(u_s, "va").reshape(db, 1, 1, hd),
                       kiwi_s[:, :cfg.idx_dim].reshape(db, 1, cfg.idx_dim),
                       col(u_s, "kb").reshape(db, 1, cfg.n_kv_b, hd),
                       col(u_s, "vb").reshape(db, 1, cfg.n_kv_b, hd)))
        xp, xs = y_p.reshape(bp, seq, d), y_s.reshape(1, db, d)

    stack = lambda rows: tuple(jnp.stack(r) for r in zip(*rows))
    return (xp, xs.reshape(db, 1, d)) + stack(rows_p) + stack(rows_s)


def kernel(x_prompt, x_sample, cache_a_k, cache_a_v, cache_idx_k, cache_b_k, cache_b_v, page_table,
           c_prompt, c_sample, w_ada, b_ada, g_mix, g_ffn, g_final, w_in, w_up_a, w_up_b, w_out,
           w_group, b_group, w_router, b_router, w_gate_e, w_up_e, w_down_e):
    return _forward(Cfg(), x_prompt, x_sample, cache_a_k, cache_a_v, cache_idx_k, cache_b_k, cache_b_v,
                    page_table, c_prompt, c_sample, w_ada, b_ada, g_mix, g_ffn, g_final, w_in,
                    w_up_a, w_up_b, w_out, w_group, b_group, w_router, b_router, w_gate_e, w_up_e, w_down_e)
```

```python
import functools
from typing import NamedTuple

import numpy as np
import jax
import jax.numpy as jnp
from jax import lax
from jax.experimental import pallas as pl
from jax.experimental.pallas import tpu as pltpu

F32 = jnp.float32
BF16 = jnp.bfloat16
I32 = jnp.int32

LANES = 128
INT_MIN = -(2 ** 31)
RMS_EPS = 1e-6
NEG_BIG = -1e30
SB_DEAD_TAIL = -110.0
VMEM_LIMIT = 48 * 1024 * 1024


class Cfg(NamedTuple):
    d_model: int = 2048
    head_dim: int = 128
    n_heads_a: int = 8
    n_idx_heads: int = 16
    idx_dim: int = 64
    n_heads_b: int = 8
    n_kv_b: int = 4
    topk_max: int = 256
    n_groups: int = 8
    epg: int = 8
    top_k: int = 2
    d_ff: int = 512
    page_size: int = 128
    tn_in: int = 512
    tm_in: int = 1024
    tq_a: int = 256
    tk_a: int = 256
    t_b: int = 256
    tm_merge: int = 256
    moe_rows: int = 256
    tm_comb: int = 128
    pages_per_step: int = 16
    sb_pages_per_chunk: int = 4


def _cparams(sem):
    return pltpu.CompilerParams(dimension_semantics=sem, vmem_limit_bytes=VMEM_LIMIT)


def _round_up(a, b):
    return -(-a // b) * b


def _layout(cfg):
    d, hd = cfg.d_model, cfg.head_dim
    segs = [("ga", d), ("gb", d), ("qa", cfg.n_heads_a * hd), ("qi", cfg.n_idx_heads * cfg.idx_dim),
            ("qb", cfg.n_heads_b * hd), ("kb", cfg.n_kv_b * hd), ("vb", cfg.n_kv_b * hd),
            ("ka", hd), ("va", hd), ("kiwi", LANES)]
    off, o = {}, 0
    for name, w in segs:
        off[name] = (o, w)
        o += w
    return off, _round_up(o, cfg.tn_in)


def _permute_w_in(w_in, cfg):
    hd = cfg.head_dim
    sizes = [cfg.n_heads_a * hd, hd, hd, cfg.n_idx_heads * cfg.idx_dim, cfg.idx_dim, cfg.n_idx_heads,
             cfg.n_heads_b * hd, cfg.n_kv_b * hd, cfg.n_kv_b * hd, cfg.d_model, cfg.d_model]
    names = ["qa", "ka", "va", "qi", "ki", "wi", "qb", "kb", "vb", "ga", "gb"]
    cuts = np.concatenate([[0], np.cumsum(sizes)])
    src = {n: w_in[:, int(cuts[i]):int(cuts[i + 1])] for i, n in enumerate(names)}
    off, width = _layout(cfg)
    assert cfg.idx_dim + cfg.n_idx_heads <= LANES
    kiwi = jnp.concatenate(
        [src["ki"], src["wi"], jnp.zeros((w_in.shape[0], LANES - cfg.idx_dim - cfg.n_idx_heads), w_in.dtype)], axis=1)
    parts = [src["ga"], src["gb"], src["qa"], src["qi"], src["qb"], src["kb"], src["vb"], src["ka"], src["va"], kiwi]
    used = sum(p.shape[1] for p in parts)
    if width > used:
        parts.append(jnp.zeros((w_in.shape[0], width - used), w_in.dtype))
    return jnp.concatenate(parts, axis=1)


def _split(x):
    hi = x.astype(BF16)
    return hi, (x - hi.astype(F32)).astype(BF16)


def _dot3(a, b, dot=None):
    dot = dot or (lambda x, y: jnp.dot(x, y, preferred_element_type=F32))
    (ah, al), (bh, bl) = _split(a), _split(b)
    return dot(ah, bh) + dot(al, bh) + dot(ah, bl)


def _adaln_kernel(c_ref, w_ref, b_ref, o_ref):
    c = c_ref[...]
    o_ref[...] = _dot3(c * (1.0 / (1.0 + jnp.exp(-c))), w_ref[...]) + b_ref[...]


def _adaln(c_all, w_ada, b_ada):
    mc, d = c_all.shape
    n = w_ada.shape[1]
    tn = 1024 if n % 1024 == 0 else n
    return pl.pallas_call(
        _adaln_kernel,
        out_shape=jax.ShapeDtypeStruct((mc, n), F32),
        grid=(n // tn,),
        in_specs=[pl.BlockSpec((mc, d), lambda j: (0, 0)),
                  pl.BlockSpec((d, tn), lambda j: (0, j)),
                  pl.BlockSpec((1, tn), lambda j: (0, j))],
        out_specs=pl.BlockSpec((mc, tn), lambda j: (0, j)),
        compiler_params=_cparams(("arbitrary",)),
        name="adaln",
    )(c_all, w_ada, b_ada.reshape(1, n))


def _rms_mod_kernel(x_ref, g_ref, m_ref, o_ref):
    x = x_ref[0]
    y = x * lax.rsqrt(jnp.mean(x * x, axis=-1, keepdims=True) + RMS_EPS) * g_ref[...]
    o_ref[0] = (y * (1.0 + m_ref[0, 1]) + m_ref[0, 0]).astype(o_ref.dtype)


def _rms_mod(x3, g, mod, ts, out_dtype=None):
    bx, t, d = x3.shape
    r = mod.shape[2]
    out_dtype = out_dtype or BF16
    return pl.pallas_call(
        _rms_mod_kernel,
        out_shape=jax.ShapeDtypeStruct((bx, t, d), out_dtype),
        grid=(bx, t // ts),
        in_specs=[pl.BlockSpec((1, ts, d), lambda b, i: (b, i, 0)),
                  pl.BlockSpec((1, d), lambda b, i: (0, 0)),
                  pl.BlockSpec((1, 6, r, d), lambda b, i: (b, 0, 0, 0))],
        out_specs=pl.BlockSpec((1, ts, d), lambda b, i: (b, i, 0)),
        compiler_params=_cparams(("arbitrary", "arbitrary")),
        name="rms_mod",
    )(x3, g.reshape(1, d), mod)


def _mm_kernel(a_ref, w_ref, o_ref, wbf_ref):
    @pl.when(pl.program_id(1) == 0)
    def _():
        wbf_ref[...] = w_ref[...].astype(BF16)

    o_ref[...] = jnp.dot(a_ref[...], wbf_ref[...], preferred_element_type=F32).astype(o_ref.dtype)


def _mm3_kernel(a_ref, w_ref, o_ref, wbf_ref):
    del wbf_ref
    o_ref[...] = _dot3(a_ref[...], w_ref[...]).astype(o_ref.dtype)


def _matmul(a, w, tm, tn, out_dtype=F32):
    m, k = a.shape
    n = w.shape[1]
    return pl.pallas_call(
        _mm_kernel if a.dtype == BF16 else _mm3_kernel,
        out_shape=jax.ShapeDtypeStruct((m, n), out_dtype),
        grid=(n // tn, m // tm),
        in_specs=[pl.BlockSpec((tm, k), lambda j, i: (i, 0)),
                  pl.BlockSpec((k, tn), lambda j, i: (0, j))],
        out_specs=pl.BlockSpec((tm, tn), lambda j, i: (i, j)),
        scratch_shapes=[pltpu.VMEM((k, tn), BF16)],
        compiler_params=_cparams(("arbitrary", "arbitrary")),
        name="in_proj",
    )(a, w)


def _ordered_bits_to_float(u):
    key = u ^ jnp.int32(INT_MIN)
    return pltpu.bitcast(jnp.where(key < 0, key ^ jnp.int32(0x7FFFFFFF), key), F32)


def _kth_largest(count_ge, shape, k):
    def body(bi, res):
        cand = res | lax.shift_left(jnp.int32(1), jnp.int32(31) - bi)
        return jnp.where(count_ge(_ordered_bits_to_float(cand)) >= k, cand, res)

    thr = _ordered_bits_to_float(lax.fori_loop(0, 32, body, jnp.zeros(shape, I32)))
    return jnp.where(thr != thr, -jnp.inf, thr)


def _tie_cutoff(count_tie_lt, shape, need, nbits):
    def body(bi, x):
        cand = x | lax.shift_left(jnp.int32(1), jnp.int32(nbits - 1) - bi)
        return jnp.where(count_tie_lt(cand) < need, cand, x)

    return lax.fori_loop(0, nbits, body, jnp.zeros(shape, I32))


def _fold_lanes(x):
    acc = x[:, 0:LANES]
    for j in range(1, x.shape[1] // LANES):
        acc = acc + x[:, j * LANES:(j + 1) * LANES]
    return acc


def _dot_nt(a, b):
    return lax.dot_general(a, b, (((1,), (1,)), ((), ())), preferred_element_type=F32)


def _dsa_prompt_kernel(qa_ref, qi_ref, kwq_ref, kw_ref, ka_ref, va_ref, o_ref,
                       ka_s, kb_s, kbf_s, vt_s, sc_s, dist_s, xcut_s, qbf_s, m_s, l_s, acc_s, lg_s, p_s, al_s,
                       *, cfg, tq, tk, n_sel):
    hd, idim, ni, ha = cfg.head_dim, cfg.idx_dim, cfg.n_idx_heads, cfg.n_heads_a
    s_total = sc_s.shape[0]
    i = pl.program_id(1)
    t0 = i * tq
    nk = (t0 + tq + tk - 1) // tk

    @pl.when(i == 0)
    def _stage():
        kw = kw_ref[0]
        lane = lax.broadcasted_iota(I32, kw.shape, 1)
        k_lo = jnp.where(lane < idim, kw, 0.0)
        ka_s[...] = k_lo.astype(BF16)
        kb_s[...] = pltpu.roll(k_lo, idim, 1).astype(BF16)
        kbf_s[...] = ka_ref[0].astype(BF16)
        for c in range(s_total // tk):
            vt_s[:, c * tk:(c + 1) * tk] = va_ref[0, c * tk:(c + 1) * tk, :].T.astype(BF16)

    w_t = (kwq_ref[0] * (float(ni * idim) ** -0.5)).T
    qpos = t0 + lax.broadcasted_iota(I32, (tk, tq), 1)
    krow = lax.broadcasted_iota(I32, (tk, tq), 0)

    def score_chunk(c, carry):
        ks = pl.multiple_of(c * tk, tk)
        k_even = ka_s[pl.ds(ks, tk), :]
        k_odd = kb_s[pl.ds(ks, tk), :]
        acc = jnp.zeros((tk, tq), F32)
        for p in range(ni // 2):
            qp = qi_ref[0, :, p * LANES:(p + 1) * LANES].astype(BF16)
            for par, kk in ((0, k_even), (1, k_odd)):
                h = 2 * p + par
                acc = acc + w_t[idim + h:idim + h + 1, :] * jnp.maximum(_dot_nt(kk, qp), 0.0)
        sc_s[pl.ds(ks, tk), :] = jnp.where(ks + krow <= qpos, acc, -jnp.inf)
        return carry

    lax.fori_loop(0, nk, score_chunk, 0)

    def count(pred):
        def body(c, cnt):
            ks = pl.multiple_of(c * tk, tk)
            hit = pred(sc_s[pl.ds(ks, tk), :], ks + krow).astype(I32)
            return cnt + jnp.sum(hit.reshape(tk // 8, 8, tq), axis=0)

        cnt = lax.fori_loop(0, nk, body, jnp.zeros((8, tq), I32))
        return jnp.sum(cnt, axis=0, keepdims=True)

    thr = _kth_largest(lambda t: count(lambda s, k: s >= t), (1, tq), n_sel)

    xcut_s[...] = jnp.full(xcut_s.shape, s_total, I32)

    @pl.when(jnp.max(count(lambda s, k: s >= thr)) > n_sel)
    def _ties():
        need = n_sel - count(lambda s, k: s > thr)
        x = _tie_cutoff(lambda cand: count(lambda s, k: (s == thr) & (k < cand)), (1, tq), need,
                        max(1, (s_total - 1).bit_length()))
        xcut_s[...] = jnp.broadcast_to(x, xcut_s.shape)

    xcut = xcut_s[0:1, :]

    def mask_chunk(c, carry):
        ks = pl.multiple_of(c * tk, tk)
        kpos = ks + krow
        s = sc_s[pl.ds(ks, tk), :]
        sel = ((s > thr) | ((s == thr) & (kpos <= xcut))) & (kpos <= qpos)
        dist_s[pl.ds(ks, tk), :] = jnp.where(sel, (qpos - kpos).astype(F32), -NEG_BIG)
        return carry

    lax.fori_loop(0, nk, mask_chunk, 0)
    scale = float(hd) ** -0.5

    for h in range(ha):
        qbf_s[h] = qa_ref[0, :, h * hd:(h + 1) * hd].astype(BF16)
    m_s[...] = jnp.full(m_s.shape, NEG_BIG, F32)
    l_s[...] = jnp.zeros(l_s.shape, F32)
    acc_s[...] = jnp.zeros(acc_s.shape, F32)

    def att_chunk(c, carry):
        ks = pl.multiple_of(c * tk, tk)
        kc = kbf_s[pl.ds(ks, tk), :]
        vt = vt_s[:, pl.ds(ks, tk)]
        dist = dist_s[pl.ds(ks, tk), :]
        for h in range(ha):
            lg_s[h] = _dot_nt(kc, qbf_s[h])
        for h in range(ha):
            slope = float(2.0 ** (-8.0 * (h + 1) / ha))
            lg = lg_s[h] * scale - slope * dist
            m = m_s[h]
            m_new = jnp.maximum(m, jnp.max(lg, axis=0, keepdims=True))
            p = jnp.exp(lg - m_new)
            alpha = jnp.exp(m - m_new)
            l_s[h] = alpha * l_s[h] + jnp.sum(p, axis=0, keepdims=True)
            p_s[h] = p.astype(BF16)
            al_s[h] = alpha
            m_s[h] = m_new
        for h in range(ha):
            acc_s[h] = al_s[h] * acc_s[h] + jnp.dot(vt, p_s[h], preferred_element_type=F32)
        return carry

    lax.fori_loop(0, nk, att_chunk, 0)
    for h in range(ha):
        o_ref[0, :, h * hd:(h + 1) * hd] = (acc_s[h] / l_s[h]).T.astype(o_ref.dtype)


def _dsa_prompt(u3, cfg):
    b, s, _ = u3.shape
    off, _ = _layout(cfg)
    tq, tk = min(cfg.tq_a, s), min(cfg.tk_a, s)
    n_sel = min(cfg.topk_max, s // 4)
    aw = cfg.n_heads_a * cfg.head_dim
    iw = cfg.n_idx_heads * cfg.idx_dim
    assert 2 * cfg.idx_dim == LANES and s % tk == 0 and s % tq == 0 and tk >= n_sel
    cb = lambda name, w: off[name][0] // w
    kern = functools.partial(_dsa_prompt_kernel, cfg=cfg, tq=tq, tk=tk, n_sel=n_sel)
    return pl.pallas_call(
        kern,
        out_shape=jax.ShapeDtypeStruct((b, s, aw), BF16),
        grid=(b, s // tq),
        in_specs=[pl.BlockSpec((1, tq, aw), lambda bi, i: (bi, i, cb("qa", aw))),
                  pl.BlockSpec((1, tq, iw), lambda bi, i: (bi, i, cb("qi", iw))),
                  pl.BlockSpec((1, tq, LANES), lambda bi, i: (bi, i, cb("kiwi", LANES))),
                  pl.BlockSpec((1, s, LANES), lambda bi, i: (bi, 0, cb("kiwi", LANES))),
                  pl.BlockSpec((1, s, LANES), lambda bi, i: (bi, 0, cb("ka", LANES))),
                  pl.BlockSpec((1, s, LANES), lambda bi, i: (bi, 0, cb("va", LANES)))],
        out_specs=pl.BlockSpec((1, tq, aw), lambda bi, i: (bi, i, 0)),
        scratch_shapes=[pltpu.VMEM((s, LANES), BF16), pltpu.VMEM((s, LANES), BF16),
                        pltpu.VMEM((s, LANES), BF16), pltpu.VMEM((cfg.head_dim, s), BF16),
                        pltpu.VMEM((s, tq), F32), pltpu.VMEM((s, tq), F32), pltpu.VMEM((8, tq), I32),
                        pltpu.VMEM((cfg.n_heads_a, tq, cfg.head_dim), BF16),
                        pltpu.VMEM((cfg.n_heads_a, 1, tq), F32), pltpu.VMEM((cfg.n_heads_a, 1, tq), F32),
                        pltpu.VMEM((cfg.n_heads_a, cfg.head_dim, tq), F32),
                        pltpu.VMEM((cfg.n_heads_a, tk, tq), F32), pltpu.VMEM((cfg.n_heads_a, tk, tq), BF16),
                        pltpu.VMEM((cfg.n_heads_a, 1, tq), F32)],
        compiler_params=_cparams(("arbitrary", "arbitrary")),
        name="dsa_prompt",
    )(u3, u3, u3, u3, u3, u3)


def _sb_terms(q, kc, scale):
    z = _dot_nt(q, kc) * scale
    lk = -(jnp.maximum(z, 0.0) + jnp.log(1.0 + jnp.exp(-jnp.abs(z))))
    return z, lk


def _suffix_in_chunk(lk, upper):
    hi = lk.astype(BF16)
    lo = (lk - hi.astype(F32)).astype(BF16)
    return (jnp.dot(hi, upper, preferred_element_type=F32) + jnp.dot(lo, upper, preferred_element_type=F32))


def _strict_lower_ones(n):
    return (lax.broadcasted_iota(I32, (n, n), 0) > lax.broadcasted_iota(I32, (n, n), 1)).astype(BF16)


def _sb_prompt_kernel(q_ref, k_ref, v_ref, o_ref, kbf_s, vbf_s, *, t, hd, rep):
    i = pl.program_id(2)

    @pl.when(i == 0)
    def _stage():
        kbf_s[...] = k_ref[0].astype(BF16)
        vbf_s[...] = v_ref[0].astype(BF16)

    qs = [q_ref[0, :, r * hd:(r + 1) * hd].astype(BF16) for r in range(rep)]
    scale = float(hd) ** -0.5
    upper = _strict_lower_ones(t)
    before = lax.broadcasted_iota(I32, (t, t), 1) < lax.broadcasted_iota(I32, (t, t), 0)

    ks = pl.multiple_of(i * t, t)
    kc, vc = kbf_s[pl.ds(ks, t), :], vbf_s[pl.ds(ks, t), :]
    state = []
    for q in qs:
        z, lk = _sb_terms(q, kc, scale)
        lkm = jnp.where(before, lk, 0.0)
        tail = _suffix_in_chunk(lkm, upper)
        w = jnp.where(before, jnp.exp(z + lk + tail), 0.0)
        state.append(jnp.sum(lkm, axis=-1, keepdims=True))
        state.append(jnp.dot(w.astype(BF16), vc, preferred_element_type=F32))

    def live(st):
        top = st[0]
        for r in range(1, rep):
            top = jnp.maximum(top, st[2 * r])
        return jnp.max(top) > SB_DEAD_TAIL

    def body(loop):
        j, _, st = loop
        ks = pl.multiple_of((i - 1 - j) * t, t)
        kc, vc = kbf_s[pl.ds(ks, t), :], vbf_s[pl.ds(ks, t), :]
        out = []
        for r, q in enumerate(qs):
            carry, acc = st[2 * r], st[2 * r + 1]
            z, lk = _sb_terms(q, kc, scale)
            tail = carry + _suffix_in_chunk(lk, upper)
            w = jnp.exp(z + lk + tail)
            out.append(carry + jnp.sum(lk, axis=-1, keepdims=True))
            out.append(acc + jnp.dot(w.astype(BF16), vc, preferred_element_type=F32))
        return j + 1, live(out), tuple(out)

    _, _, st = lax.while_loop(lambda loop: (loop[0] < i) & loop[1], body, (jnp.int32(0), live(state), tuple(state)))
    for r in range(rep):
        o_ref[0, :, r * hd:(r + 1) * hd] = st[2 * r + 1].astype(o_ref.dtype)


def _sb_prompt(u3, cfg):
    b, s, _ = u3.shape
    off, _ = _layout(cfg)
    hd, hb, g = cfg.head_dim, cfg.n_heads_b, cfg.n_kv_b
    rep = hb // g
    t = min(cfg.t_b, s)
    assert s % t == 0
    qw = rep * hd
    assert off["qb"][0] % qw == 0
    qb0, kb0, vb0 = off["qb"][0] // qw, off["kb"][0] // hd, off["vb"][0] // hd
    kern = functools.partial(_sb_prompt_kernel, t=t, hd=hd, rep=rep)
    return pl.pallas_call(
        kern,
        out_shape=jax.ShapeDtypeStruct((b, s, hb * hd), BF16),
        grid=(b, g, s // t),
        in_specs=[pl.BlockSpec((1, t, qw), lambda bi, gi, i: (bi, i, qb0 + gi)),
                  pl.BlockSpec((1, s, hd), lambda bi, gi, i: (bi, 0, kb0 + gi)),
                  pl.BlockSpec((1, s, hd), lambda bi, gi, i: (bi, 0, vb0 + gi))],
        out_specs=pl.BlockSpec((1, t, qw), lambda bi, gi, i: (bi, i, gi)),
        scratch_shapes=[pltpu.VMEM((s, hd), BF16), pltpu.VMEM((s, hd), BF16)],
        compiler_params=_cparams(("arbitrary", "arbitrary", "arbitrary")),
        name="sb_prompt",
    )(u3, u3, u3)


def _page_specs(n, block, page_of):
    zeros = (0,) * (len(block) - 1)
    return [pl.BlockSpec((None,) + tuple(block[1:]),
                         functools.partial(lambda b, j, pt, r: (page_of(b, j, r, pt),) + zeros, r=r))
            for r in range(n)]


def _dsa_sample_score_kernel(pt_ref, q_ref, w_ref, *rest, pg):
    k_refs, o_ref = rest[:pg], rest[pg]
    q = q_ref[0]
    w = w_ref[0]
    for r in range(pg):
        d = _dot3(q, k_refs[r][...])
        o_ref[0, 0, r:r + 1, :] = jnp.sum(w * jnp.maximum(d, 0.0), axis=0, keepdims=True)


def _dsa_sample_scores(qi3, wi3, cache_ik_t, page_table, cfg):
    db, ni, idim = qi3.shape
    pool, _, ps = cache_ik_t.shape
    npg = page_table.shape[1]
    pg = min(cfg.pages_per_step, npg)
    assert npg % pg == 0 and ps == LANES
    page_of = lambda b, j, r, pt: pt[b * npg + j * pg + r]
    grid_spec = pltpu.PrefetchScalarGridSpec(
        num_scalar_prefetch=1,
        grid=(db, npg // pg),
        in_specs=[pl.BlockSpec((1, ni, idim), lambda b, j, pt: (b, 0, 0)),
                  pl.BlockSpec((1, ni, 1), lambda b, j, pt: (b, 0, 0))]
                 + _page_specs(pg, (1, idim, ps), page_of),
        out_specs=pl.BlockSpec((1, 1, pg, ps), lambda b, j, pt: (b, j, 0, 0)),
    )
    out = pl.pallas_call(
        functools.partial(_dsa_sample_score_kernel, pg=pg),
        out_shape=jax.ShapeDtypeStruct((db, npg // pg, pg, ps), F32),
        grid_spec=grid_spec,
        compiler_params=_cparams(("arbitrary", "arbitrary")),
        name="dsa_sample_scores",
    )(page_table.reshape(-1), qi3, wi3, *([cache_ik_t] * pg))
    return out


def _dsa_sample_select_kernel(sc_ref, qi_ref, kw_ref, bias_ref, bnew_ref, *, cfg, n_sel, chunk):
    idim, ni = cfg.idx_dim, cfg.n_idx_heads
    db, p = sc_ref.shape
    nchunk = p // chunk
    kw = kw_ref[...]
    ki = kw[:, 0:idim]
    s_new = jnp.zeros((db, 1), F32)
    for h in range(ni):
        d = jnp.sum(qi_ref[:, h * idim:(h + 1) * idim] * ki, axis=-1, keepdims=True)
        s_new = s_new + (kw[:, idim + h:idim + h + 1] * (float(ni * idim) ** -0.5)) * jnp.maximum(d, 0.0)
    kn1 = s_new
    idx0 = lax.broadcasted_iota(I32, (db, chunk), 1)

    def count(pred, new_hit):
        cnt = jnp.zeros((db, LANES), I32)
        for c in range(nchunk):
            cnt = cnt + _fold_lanes(pred(sc_ref[:, c * chunk:(c + 1) * chunk], c * chunk + idx0).astype(I32))
        return jnp.sum(cnt, axis=-1, keepdims=True) + new_hit.astype(I32)

    thr = _kth_largest(lambda t: count(lambda k, i: k >= t, kn1 >= t), (db, 1), n_sel)
    need = n_sel - count(lambda k, i: k > thr, kn1 > thr)
    xcut = _tie_cutoff(lambda x: count(lambda k, i: (k == thr) & (i < x), (kn1 == thr) & (p < x)),
                       (db, 1), need, max(1, p.bit_length()))
    for c in range(nchunk):
        k = sc_ref[:, c * chunk:(c + 1) * chunk]
        sel = (k > thr) | ((k == thr) & (c * chunk + idx0 <= xcut))
        bias_ref[:, c * chunk:(c + 1) * chunk] = jnp.where(sel, 0.0, NEG_BIG)
    sel_new = (kn1 > thr) | ((kn1 == thr) & (p <= xcut))
    bnew_ref[...] = jnp.broadcast_to(jnp.where(sel_new, 0.0, NEG_BIG), (db, LANES))


def _dsa_sample_select(scores, u_s, cfg):
    db, p = scores.shape
    off, _ = _layout(cfg)
    iw = cfg.n_idx_heads * cfg.idx_dim
    n_sel = min(cfg.topk_max, (p + 1) // 4)
    chunk = min(2048, p)
    assert p % chunk == 0 and p + 1 >= n_sel
    kern = functools.partial(_dsa_sample_select_kernel, cfg=cfg, n_sel=n_sel, chunk=chunk)
    return pl.pallas_call(
        kern,
        out_shape=(jax.ShapeDtypeStruct((db, p), F32), jax.ShapeDtypeStruct((db, LANES), F32)),
        grid=(1,),
        in_specs=[pl.BlockSpec((db, p), lambda i: (0, 0)),
                  pl.BlockSpec((db, iw), lambda i: (0, off["qi"][0] // iw)),
                  pl.BlockSpec((db, LANES), lambda i: (0, off["kiwi"][0] // LANES))],
        out_specs=(pl.BlockSpec((db, p), lambda i: (0, 0)), pl.BlockSpec((db, LANES), lambda i: (0, 0))),
        compiler_params=_cparams(("arbitrary",)),
        name="dsa_sample_select",
    )(scores, u_s, u_s)


def _dsa_sample_attend_kernel(pt_ref, q_ref, bias_ref, bnew_ref, knew_ref, vnew_ref, *rest,
                              pg, cfg, past):
    k_refs, v_refs = rest[:pg], rest[pg:2 * pg]
    o_ref, m_s, l_s, acc_s = rest[2 * pg:]
    hd, ha = cfg.head_dim, cfg.n_heads_a
    j = pl.program_id(1)
    scale = float(hd) ** -0.5

    @pl.when(j == 0)
    def _init():
        m_s[...] = jnp.full(m_s.shape, NEG_BIG, F32)
        l_s[...] = jnp.zeros(l_s.shape, F32)
        acc_s[...] = jnp.zeros(acc_s.shape, F32)

    q = q_ref[0]
    hidx = lax.broadcasted_iota(I32, (ha, 1), 0)
    slopes = jnp.zeros((ha, 1), F32)
    for h in range(ha):
        slopes = jnp.where(hidx == h, float(2.0 ** (-8.0 * (h + 1) / ha)), slopes)

    def update(lg, bias, vs):
        sel = jnp.broadcast_to(bias == 0.0, lg.shape)
        lg = lg + bias
        m = m_s[...]
        m_new = jnp.maximum(m, jnp.max(lg, axis=-1, keepdims=True))
        p = jnp.where(sel, jnp.exp(lg - m_new), 0.0)
        alpha = jnp.exp(m - m_new)
        l_s[...] = alpha * l_s[...] + jnp.sum(p, axis=-1, keepdims=True)
        w = vs[0].shape[0]
        pv = _dot3(p[:, 0:w], vs[0])
        for r in range(1, len(vs)):
            pv = pv + _dot3(p[:, r * w:(r + 1) * w], vs[r])
        acc_s[...] = alpha * acc_s[...] + pv
        m_s[...] = m_new

    n = pg * LANES
    pos = j * n + lax.broadcasted_iota(I32, (1, n), 1)
    lg = jnp.concatenate([_dot3(q, k_refs[r][...], _dot_nt) for r in range(pg)], axis=1)
    lg = lg * scale - slopes * (past - pos).astype(F32)
    update(lg, bias_ref[0, 0], [v_refs[r][...] for r in range(pg)])

    @pl.when(j == pl.num_programs(1) - 1)
    def _fin():
        kn = jnp.broadcast_to(knew_ref[0], (16, hd))
        vn = jnp.broadcast_to(vnew_ref[0], (16, hd))
        lg = _dot3(q, kn, _dot_nt) * scale
        first = lax.broadcasted_iota(I32, (1, 16), 1) == 0
        update(lg, jnp.where(first, bnew_ref[0][:, 0:16], NEG_BIG), [vn])
        o_ref[0] = (acc_s[...] / l_s[...]).astype(o_ref.dtype)


def _dsa_sample_attend(qa3, bias, bias_new, u_s3, cache_k, cache_v, page_table, cfg):
    db, ha, hd = qa3.shape
    npg = page_table.shape[1]
    pg = min(cfg.pages_per_step, npg)
    ps = cache_k.shape[1]
    bias4 = bias.reshape(db, npg // pg, 1, pg * ps)
    off, _ = _layout(cfg)
    page_of = lambda b, j, r, pt: pt[b * npg + j * pg + r]
    kern = functools.partial(_dsa_sample_attend_kernel, pg=pg, cfg=cfg, past=npg * ps)
    grid_spec = pltpu.PrefetchScalarGridSpec(
        num_scalar_prefetch=1,
        grid=(db, npg // pg),
        in_specs=[pl.BlockSpec((1, ha, hd), lambda b, j, pt: (b, 0, 0)),
                  pl.BlockSpec((1, 1, 1, pg * ps), lambda b, j, pt: (b, j, 0, 0)),
                  pl.BlockSpec((1, 1, LANES), lambda b, j, pt: (b, 0, 0)),
                  pl.BlockSpec((1, 1, hd), lambda b, j, pt: (b, 0, off["ka"][0] // hd)),
                  pl.BlockSpec((1, 1, hd), lambda b, j, pt: (b, 0, off["va"][0] // hd))]
                 + _page_specs(pg, (1, ps, hd), page_of) + _page_specs(pg, (1, ps, hd), page_of),
        out_specs=pl.BlockSpec((1, ha, hd), lambda b, j, pt: (b, 0, 0)),
        scratch_shapes=[pltpu.VMEM((ha, 1), F32), pltpu.VMEM((ha, 1), F32), pltpu.VMEM((ha, hd), F32)],
    )
    return pl.pallas_call(
        kern,
        out_shape=jax.ShapeDtypeStruct((db, ha, hd), F32),
        grid_spec=grid_spec,
        compiler_params=_cparams(("arbitrary", "arbitrary")),
        name="dsa_sample_attend",
    )(page_table.reshape(-1), qa3, bias4, bias_new.reshape(db, 1, LANES),
      u_s3, u_s3, *([cache_k] * pg), *([cache_v] * pg))


def _sb_sample_kernel(pt_ref, q_ref, k_hbm, v_hbm, o_ref, kbuf, vbuf, sem, *, pg, npg, cfg):
    hd, hb, g = cfg.head_dim, cfg.n_heads_b, cfg.n_kv_b
    rep = hb // g
    b = pl.program_id(0)
    nchunks = npg // pg
    scale = float(hd) ** -0.5

    def page_copies(c, slot):
        out = []
        for r in range(pg):
            page = pt_ref[b * npg + (npg - 1 - (c * pg + r))]
            out.append(pltpu.make_async_copy(k_hbm.at[page], kbuf.at[slot, r], sem.at[0, slot]))
            out.append(pltpu.make_async_copy(v_hbm.at[page], vbuf.at[slot, r], sem.at[1, slot]))
        return out

    q = q_ref[0]
    qg = [q[:, gi * hd:(gi + 1) * hd] for gi in range(g)]
    upper = _strict_lower_ones(LANES)

    for cp in page_copies(0, 0):
        cp.start()

    def body(loop):
        c, _, carry, accs = loop
        slot = c % 2

        @pl.when(c + 1 < nchunks)
        def _prefetch():
            for cp in page_copies(c + 1, 1 - slot):
                cp.start()

        for cp in page_copies(c, slot):
            cp.wait()
        zs = []
        for r in range(pg):
            z = _dot3(qg[0], kbuf[slot, r, :, 0, :], _dot_nt)
            for gi in range(1, g):
                z = z + _dot3(qg[gi], kbuf[slot, r, :, gi, :], _dot_nt)
            zs.append(z * scale)
        z_all = jnp.concatenate(zs, axis=0)
        lk_all = -(jnp.maximum(z_all, 0.0) + jnp.log(1.0 + jnp.exp(-jnp.abs(z_all))))
        tail_in = _suffix_in_chunk(lk_all, upper)
        tot = jnp.sum(lk_all, axis=-1, keepdims=True)
        accs = list(accs)
        for r in range(pg):
            sl = slice(r * hb, (r + 1) * hb)
            w = jnp.exp(z_all[sl] + lk_all[sl] + (carry + tail_in[sl]))
            for gi in range(g):
                accs[gi] = accs[gi] + _dot3(w, vbuf[slot, r, :, gi, :])
            carry = carry + tot[sl]
        return c + 1, jnp.max(carry) > SB_DEAD_TAIL, carry, tuple(accs)

    init = (jnp.int32(0), jnp.bool_(True), jnp.zeros((hb, 1), F32), tuple(jnp.zeros((hb, hd), F32) for _ in range(g)))
    c_end, _, _, accs = lax.while_loop(lambda loop: (loop[0] < nchunks) & loop[1], body, init)

    @pl.when(c_end < nchunks)
    def _drain():
        for cp in page_copies(c_end, c_end % 2):
            cp.wait()

    head_group = lax.broadcasted_iota(I32, (hb, hd), 0) // rep
    out = jnp.zeros((hb, hd), F32)
    for gi in range(g):
        out = out + jnp.where(head_group == gi, accs[gi], 0.0)
    o_ref[0] = out.astype(o_ref.dtype)


def _sb_sample(q_bd, cache_k, cache_v, page_table, cfg):
    db, hb, gw = q_bd.shape
    npg = page_table.shape[1]
    _, ps, g, hd = cache_k.shape
    pg = min(cfg.sb_pages_per_chunk, npg)
    assert npg % pg == 0 and ps == LANES and g * hd == gw
    grid_spec = pltpu.PrefetchScalarGridSpec(
        num_scalar_prefetch=1,
        grid=(db,),
        in_specs=[pl.BlockSpec((1, hb, gw), lambda b, pt: (b, 0, 0)),
                  pl.BlockSpec(memory_space=pl.ANY), pl.BlockSpec(memory_space=pl.ANY)],
        out_specs=pl.BlockSpec((1, hb, cfg.head_dim), lambda b, pt: (b, 0, 0)),
        scratch_shapes=[pltpu.VMEM((2, pg, ps, g, hd), F32), pltpu.VMEM((2, pg, ps, g, hd), F32),
                        pltpu.SemaphoreType.DMA((2, 2))],
    )
    return pl.pallas_call(
        functools.partial(_sb_sample_kernel, pg=pg, npg=npg, cfg=cfg),
        out_shape=jax.ShapeDtypeStruct((db, hb, cfg.head_dim), F32),
        grid_spec=grid_spec,
        compiler_params=_cparams(("arbitrary",)),
        name="sb_sample",
    )(page_table.reshape(-1), q_bd, cache_k, cache_v)


def _merge_kernel(x_ref, oa_ref, ob_ref, ga_ref, gb_ref, m_ref, gf_ref, wua_ref, wub_ref, wo_ref, wr_ref, br_ref,
                  x1_ref, h2_ref, rt_ref, *, cfg):
    ng, epg = cfg.n_groups, cfg.epg
    sig = lambda v: 1.0 / (1.0 + jnp.exp(-v))

    def mm(a, w_ref):
        dot = lambda p, q: jnp.dot(p, q, preferred_element_type=F32)
        if w_ref.shape[0] == 1:
            return dot(a.astype(BF16), w_ref[0])
        a_hi, a_lo = _split(a)
        return dot(a_hi, w_ref[0]) + dot(a_lo, w_ref[0]) + dot(a_hi, w_ref[1])

    merged = sig(ga_ref[...]) * mm(oa_ref[...], wua_ref) + sig(gb_ref[...]) * mm(ob_ref[...], wub_ref)
    x1 = x_ref[0] + m_ref[0, 2] * mm(merged, wo_ref)
    x1_ref[...] = x1
    h2 = x1 * lax.rsqrt(jnp.mean(x1 * x1, axis=-1, keepdims=True) + RMS_EPS) * gf_ref[...]
    h2 = h2 * (1.0 + m_ref[0, 4]) + m_ref[0, 3]
    h2_ref[...] = h2
    logits = mm(h2, wr_ref) + br_ref[...]

    lane = lax.broadcasted_iota(I32, logits.shape, 1).astype(F32)
    first = lambda hit: jnp.min(jnp.where(hit, lane, float(LANES)), axis=-1, keepdims=True)
    in_g = lane < ng
    gl = jnp.where(in_g, logits, -jnp.inf)
    gmax = jnp.max(gl, axis=-1, keepdims=True)
    gidx = first(gl == gmax)
    p_g = 1.0 / jnp.sum(jnp.where(in_g, jnp.exp(logits - gmax), 0.0), axis=-1, keepdims=True)
    lo = ng + gidx * epg
    el = jnp.where((lane >= lo) & (lane < lo + epg), logits, -jnp.inf)
    v1 = jnp.max(el, axis=-1, keepdims=True)
    i1 = first(el == v1)
    el2 = jnp.where(lane == i1, -jnp.inf, el)
    v2 = jnp.max(el2, axis=-1, keepdims=True)
    i2 = first(el2 == v2)
    e21 = jnp.exp(v2 - v1)
    w1 = 1.0 / (1.0 + e21)
    w2 = e21 / (1.0 + e21)
    rt = jnp.where(lane == 0, i1 - ng, jnp.where(lane == 1, i2 - ng,
         jnp.where(lane == 2, p_g * w1, jnp.where(lane == 3, p_g * w2, 0.0))))
    rt_ref[...] = rt


def _merge(x3, oa, ob, u, mod, g_ffn, wua, wub, wo, wr, br, cfg, tm):
    bx, t, d = x3.shape
    m = bx * t
    off, _ = _layout(cfg)
    aw, bw = oa.shape[1], ob.shape[1]
    r = mod.shape[2]
    nt = t // tm
    const = lambda shape: pl.BlockSpec(shape, lambda b, i: (0,) * len(shape), pipeline_mode=pl.Buffered(1))
    return pl.pallas_call(
        functools.partial(_merge_kernel, cfg=cfg),
        out_shape=(jax.ShapeDtypeStruct((m, d), F32), jax.ShapeDtypeStruct((m, d), F32),
                   jax.ShapeDtypeStruct((m, LANES), F32)),
        grid=(bx, nt),
        in_specs=[pl.BlockSpec((1, tm, d), lambda b, i: (b, i, 0)),
                  pl.BlockSpec((tm, aw), lambda b, i: (b * nt + i, 0)),
                  pl.BlockSpec((tm, bw), lambda b, i: (b * nt + i, 0)),
                  pl.BlockSpec((tm, d), lambda b, i: (b * nt + i, off["ga"][0] // d)),
                  pl.BlockSpec((tm, d), lambda b, i: (b * nt + i, off["gb"][0] // d)),
                  pl.BlockSpec((1, 6, r, d), lambda b, i: (b, 0, 0, 0)),
                  const((1, d)), const(wua.shape), const(wub.shape), const(wo.shape), const((2, d, LANES)),
                  const((1, LANES))],
        out_specs=(pl.BlockSpec((tm, d), lambda b, i: (b * nt + i, 0)),
                   pl.BlockSpec((tm, d), lambda b, i: (b * nt + i, 0)),
                   pl.BlockSpec((tm, LANES), lambda b, i: (b * nt + i, 0))),
        compiler_params=_cparams(("arbitrary", "arbitrary")),
        name="merge_out_router",
    )(x3, oa, ob, u, u, mod, g_ffn.reshape(1, d), wua, wub, wo, wr, br)


def _moe_kernel(be_ref, nb_ref, tok_ref, tokn_ref, h_hbm, wg_ref, wu_ref, wd_ref, y_ref,
                xbuf, sem, wg_s, wu_s, wd_s, *, rows):
    i = pl.program_id(0)
    nb = nb_ref[0]

    def row_copy(tok, slot, r):
        return pltpu.make_async_copy(h_hbm.at[pl.ds(tok, 1)], xbuf.at[slot, pl.ds(r, 1)], sem.at[slot])

    def issue(idx_ref, slot):
        def body(r, c):
            row_copy(idx_ref[0, 0, r], slot, r).start()
            return c

        lax.fori_loop(0, rows, body, 0, unroll=8)

    def wait_slot(slot):
        pltpu.make_async_copy(h_hbm.at[pl.ds(0, rows)], xbuf.at[slot], sem.at[slot]).wait()

    @pl.when(i == 0)
    def _prime():
        issue(tok_ref, 0)

    @pl.when(i + 1 < nb)
    def _prefetch():
        issue(tokn_ref, (i + 1) % 2)

    @pl.when(i < nb)
    def _compute():
        slot = i % 2
        wait_slot(slot)

        @pl.when((i == 0) | (be_ref[i] != be_ref[jnp.maximum(i - 1, 0)]))
        def _cast():
            wg_s[...] = wg_ref[...].astype(BF16)
            wu_s[...] = wu_ref[...].astype(BF16)
            wd_s[...] = wd_ref[...].astype(BF16)

        x = xbuf[slot].astype(BF16)
        gate = jnp.dot(x, wg_s[...], preferred_element_type=F32)
        up = jnp.dot(x, wu_s[...], preferred_element_type=F32)
        act = (gate * (1.0 / (1.0 + jnp.exp(-gate))) * up).astype(BF16)
        y_ref[...] = jnp.dot(act, wd_s[...], preferred_element_type=F32)

    @pl.when(i >= nb)
    def _idle():
        y_ref[...] = jnp.zeros(y_ref.shape, F32)


def _moe_experts(h2_all, slot_token, block_expert, n_used, w_gate_e, w_up_e, w_down_e, rows):
    n_blocks = block_expert.shape[0]
    e, d, f = w_gate_e.shape
    tok3 = slot_token.reshape(n_blocks, 1, rows)
    grid_spec = pltpu.PrefetchScalarGridSpec(
        num_scalar_prefetch=2,
        grid=(n_blocks,),
        in_specs=[pl.BlockSpec((1, 1, rows), lambda i, be, nb: (i, 0, 0), memory_space=pltpu.SMEM),
                  pl.BlockSpec((1, 1, rows), lambda i, be, nb: (jnp.minimum(i + 1, n_blocks - 1), 0, 0),
                               memory_space=pltpu.SMEM),
                  pl.BlockSpec(memory_space=pl.ANY),
                  pl.BlockSpec((None, d, f), lambda i, be, nb: (be[i], 0, 0)),
                  pl.BlockSpec((None, d, f), lambda i, be, nb: (be[i], 0, 0)),
                  pl.BlockSpec((None, f, d), lambda i, be, nb: (be[i], 0, 0))],
        out_specs=pl.BlockSpec((rows, d), lambda i, be, nb: (i, 0)),
        scratch_shapes=[pltpu.VMEM((2, rows, d), F32), pltpu.SemaphoreType.DMA((2,)),
                        pltpu.VMEM((d, f), BF16), pltpu.VMEM((d, f), BF16), pltpu.VMEM((f, d), BF16)],
    )
    return pl.pallas_call(
        functools.partial(_moe_kernel, rows=rows),
        out_shape=jax.ShapeDtypeStruct((n_blocks * rows, d), F32),
        grid_spec=grid_spec,
        compiler_params=_cparams(("arbitrary",)),
        name="moe_experts",
    )(block_expert, n_used, tok3, tok3, h2_all, w_gate_e, w_up_e, w_down_e)


def _rank_kernel(rt_ref, base_ref, dest_ref, tot_ref, base_s, *, tm):
    @pl.when(pl.program_id(0) == 0)
    def _init():
        base_s[...] = base_ref[...]

    rt = rt_ref[...]
    lane = lax.broadcasted_iota(I32, rt.shape, 1).astype(F32)
    o1 = (lane == rt[:, 0:1]).astype(F32)
    o2 = (lane == rt[:, 1:2]).astype(F32)
    lower = _strict_lower_ones(tm)
    p1 = jnp.dot(lower, o1.astype(BF16), preferred_element_type=F32)
    p2 = jnp.dot(lower, o2.astype(BF16), preferred_element_type=F32)
    c1 = jnp.sum(o1, axis=0, keepdims=True)
    c2 = jnp.sum(o2, axis=0, keepdims=True)
    base = base_s[...]
    d1 = jnp.sum((base + p1) * o1, axis=-1, keepdims=True)
    d2 = jnp.sum((base + c1 + p2) * o2, axis=-1, keepdims=True)
    dest_ref[...] = jnp.where(lane == 0, d1, jnp.where(lane == 1, d2, 0.0))
    base_s[...] = base + c1 + c2
    tot_ref[...] = base + c1 + c2


def _rank(rt_pad, base, tm):
    n = rt_pad.shape[0]
    return pl.pallas_call(
        functools.partial(_rank_kernel, tm=tm),
        out_shape=(jax.ShapeDtypeStruct((n, LANES), F32), jax.ShapeDtypeStruct((1, LANES), F32)),
        grid=(n // tm,),
        in_specs=[pl.BlockSpec((tm, LANES), lambda i: (i, 0)), pl.BlockSpec((1, LANES), lambda i: (0, 0))],
        out_specs=(pl.BlockSpec((tm, LANES), lambda i: (i, 0)), pl.BlockSpec((1, LANES), lambda i: (0, 0))),
        scratch_shapes=[pltpu.VMEM((1, LANES), F32)],
        compiler_params=_cparams(("arbitrary",)),
        name="moe_rank",
    )(rt_pad, base)


def _moe_plan(rt_all, n_experts, rows, top_k, tm):
    n_tok = rt_all.shape[0]
    a = n_tok * top_k
    assert top_k == 2 and n_experts <= LANES
    n_blocks = -(-(a + n_experts * (rows - 1)) // rows)
    n_pad = _round_up(n_tok, tm)
    rt_pad = jnp.concatenate([rt_all, jnp.full((n_pad - n_tok, LANES), -1.0, F32)], axis=0)
    _, tot = _rank(rt_pad, jnp.zeros((1, LANES), F32), tm)
    counts = tot[0, :n_experts].astype(I32)
    padded = (counts + rows - 1) // rows * rows
    cum = jnp.cumsum(padded)
    pstart = jnp.zeros((1, LANES), F32).at[0, :n_experts].set((cum - padded).astype(F32))
    dest, _ = _rank(rt_pad, pstart, tm)
    pos = (dest[:n_tok, 0].astype(I32), dest[:n_tok, 1].astype(I32))
    tok = jnp.arange(n_tok, dtype=I32)
    slot_token = jnp.zeros((n_blocks * rows,), I32).at[jnp.concatenate(pos)].set(
        jnp.concatenate([tok, tok]), unique_indices=True)
    starts = jnp.arange(n_blocks, dtype=I32) * rows
    block_expert = jnp.minimum(jnp.sum((cum[None, :] <= starts[:, None]).astype(I32), axis=1), n_experts - 1)
    n_used = (cum[-1] // rows).astype(I32).reshape(1)
    return slot_token, pos, block_expert, n_used


def _combine_kernel(pos_ref, posn_ref, x1_ref, rt_ref, m_ref, gfin_ref, ys_hbm, o_ref, ybuf, sem, *, tm):
    t = pl.program_id(0)
    nt = pl.num_programs(0)

    def row_copy(p, slot, r):
        return pltpu.make_async_copy(ys_hbm.at[pl.ds(p, 1)], ybuf.at[slot, pl.ds(r, 1)], sem.at[slot])

    def issue(idx_ref, slot):
        def body(r, c):
            row_copy(idx_ref[0, 0, r], slot, r).start()
            return c

        lax.fori_loop(0, 2 * tm, body, 0, unroll=8)

    @pl.when(t == 0)
    def _prime():
        issue(pos_ref, 0)

    @pl.when(t + 1 < nt)
    def _prefetch():
        issue(posn_ref, (t + 1) % 2)

    slot = t % 2
    pltpu.make_async_copy(ys_hbm.at[pl.ds(0, 2 * tm)], ybuf.at[slot], sem.at[slot]).wait()

    rt = rt_ref[...]
    moe = rt[:, 2:3] * ybuf[slot, 0:tm, :] + rt[:, 3:4] * ybuf[slot, tm:2 * tm, :]
    x2 = x1_ref[...] + m_ref[0, 5] * moe
    y = x2 * lax.rsqrt(jnp.mean(x2 * x2, axis=-1, keepdims=True) + RMS_EPS)
    o_ref[...] = y * gfin_ref[...]


def _combine(x1, rt, mod, g_final, ys, pos, t_per_b, tm):
    m, d = x1.shape
    nt_all = m // tm
    nt_b = t_per_b // tm
    r = mod.shape[2]
    pos3 = jnp.concatenate([p.reshape(nt_all, 1, tm) for p in pos], axis=2)
    return pl.pallas_call(
        functools.partial(_combine_kernel, tm=tm),
        out_shape=jax.ShapeDtypeStruct((m, d), F32),
        grid=(nt_all,),
        in_specs=[pl.BlockSpec((1, 1, 2 * tm), lambda t: (t, 0, 0), memory_space=pltpu.SMEM),
                  pl.BlockSpec((1, 1, 2 * tm), lambda t: (jnp.minimum(t + 1, nt_all - 1), 0, 0),
                               memory_space=pltpu.SMEM),
                  pl.BlockSpec((tm, d), lambda t: (t, 0)),
                  pl.BlockSpec((tm, LANES), lambda t: (t, 0)),
                  pl.BlockSpec((1, 6, r, d), lambda t: (t // nt_b, 0, 0, 0)),
                  pl.BlockSpec((1, d), lambda t: (0, 0)),
                  pl.BlockSpec(memory_space=pl.ANY)],
        out_specs=pl.BlockSpec((tm, d), lambda t: (t, 0)),
        scratch_shapes=[pltpu.VMEM((2, 2 * tm, d), F32), pltpu.SemaphoreType.DMA((2,))],
        compiler_params=_cparams(("arbitrary",)),
        name="moe_combine",
    )(pos3, pos3, x1, rt, mod, g_final.reshape(1, d), ys)


def _forward(cfg, x_prompt, x_sample, cache_a_k, cache_a_v, cache_idx_k, cache_b_k, cache_b_v,
             page_table, c_prompt, c_sample, w_ada, b_ada, g_mix, g_ffn, g_final, w_in,
             w_up_a, w_up_b, w_out, w_group, b_group, w_router, b_router, w_gate_e, w_up_e, w_down_e):
    depth = w_in.shape[0]
    bp, seq, d = x_prompt.shape
    db, dseq, _ = x_sample.shape
    assert dseq == 1, "sample path is written for single-token decode"
    hd = cfg.head_dim
    off, wpad = _layout(cfg)
    ps = cfg.page_size
    ng, epg = cfg.n_groups, cfg.epg
    n_exp = ng * epg
    col = lambda a, name: a[..., off[name][0]:off[name][0] + off[name][1]]

    xp, xs = x_prompt, x_sample.reshape(1, db, d)
    c_all = jnp.concatenate([c_prompt, c_sample, jnp.zeros((-(bp + db) % 8, d), F32)], axis=0)
    rows_p, rows_s = [], []
    for layer in range(depth):
        mod = _adaln(c_all, w_ada[layer], b_ada[layer])
        mod_p = mod[:bp].reshape(bp, 6, 1, d)
        mod_s = mod[bp:bp + db].reshape(db, 6, d).transpose(1, 0, 2).reshape(1, 6, db, d)
        w_al = _permute_w_in(w_in[layer], cfg)
        def parts(w32):
            hi = w32.astype(BF16)
            return jnp.stack([hi, (w32 - hi.astype(F32)).astype(BF16)])

        wua, wub, wo = parts(w_up_a[layer]), parts(w_up_b[layer]), parts(w_out[layer])
        wr = parts(jnp.concatenate([w_group[layer], w_router[layer].transpose(1, 0, 2).reshape(d, n_exp),
                                    jnp.zeros((d, LANES - ng - n_exp), F32)], axis=1))
        br = jnp.concatenate([b_group[layer], b_router[layer].reshape(-1),
                              jnp.zeros((LANES - ng - n_exp,), F32)]).reshape(1, LANES)

        ts_p = min(256, seq)
        h_p = _rms_mod(xp, g_mix[layer], mod_p, ts_p).reshape(bp * seq, d)
        u_p = _matmul(h_p, w_al, min(cfg.tm_in, bp * seq), cfg.tn_in)
        u_p3 = u_p.reshape(bp, seq, wpad)
        oa_p = _dsa_prompt(u_p3, cfg).reshape(bp * seq, -1)
        ob_p = _sb_prompt(u_p3, cfg).reshape(bp * seq, -1)

        h_s = _rms_mod(xs, g_mix[layer], mod_s, db, out_dtype=F32).reshape(db, d)
        u_s = _matmul(h_s, w_al, db, cfg.tn_in)
        qi3 = col(u_s, "qi").reshape(db, cfg.n_idx_heads, cfg.idx_dim)
        kiwi_s = col(u_s, "kiwi")
        wi3 = (kiwi_s[:, cfg.idx_dim:cfg.idx_dim + cfg.n_idx_heads]
               * (float(cfg.n_idx_heads * cfg.idx_dim) ** -0.5)).reshape(db, cfg.n_idx_heads, 1)
        c_ik = jnp.swapaxes(cache_idx_k[layer], 1, 2)
        c_ak = cache_a_k[layer].reshape(cache_a_k.shape[1], ps, hd)
        c_av = cache_a_v[layer].reshape(cache_a_v.shape[1], ps, hd)
        c_bk, c_bv = cache_b_k[layer], cache_b_v[layer]
        sc4 = _dsa_sample_scores(qi3, wi3, c_ik, page_table, cfg)
        bias, bias_new = _dsa_sample_select(sc4.reshape(db, -1), u_s, cfg)
        qa3 = col(u_s, "qa").reshape(db, cfg.n_heads_a, hd)
        oa_s = _dsa_sample_attend(qa3, bias, bias_new, u_s.reshape(db, 1, wpad), c_ak, c_av,
                                  page_table, cfg).reshape(db, -1)
        rep = cfg.n_heads_b // cfg.n_kv_b
        qb3 = col(u_s, "qb").reshape(db, cfg.n_heads_b, 1, hd)
        gsel = (jnp.arange(cfg.n_heads_b)[:, None] // rep == jnp.arange(cfg.n_kv_b)[None, :])
        q_bd = (qb3 * gsel[None, :, :, None].astype(F32)).reshape(db, cfg.n_heads_b, cfg.n_kv_b * hd)
        ob_s = _sb_sample(q_bd, c_bk, c_bv, page_table, cfg).reshape(db, -1)

        x1_p, h2_p, rt_p = _merge(xp, oa_p, ob_p, u_p, mod_p, g_ffn[layer], wua[:1], wub[:1], wo[:1], wr, br, cfg,
                                  min(cfg.tm_merge, seq))
        x1_s, h2_s, rt_s = _merge(xs, oa_s, ob_s, u_s, mod_s, g_ffn[layer], wua, wub, wo, wr, br, cfg, db)

        h2_all = jnp.concatenate([h2_p, h2_s], axis=0)
        rt_all = jnp.concatenate([rt_p, rt_s], axis=0)
        slot_token, pos, block_expert, n_used = _moe_plan(rt_all, n_exp, cfg.moe_rows, cfg.top_k,
                                                          min(256, bp * seq))
        ys = _moe_experts(h2_all, slot_token, block_expert, n_used,
                          w_gate_e[layer], w_up_e[layer], w_down_e[layer], cfg.moe_rows)
        last = layer == depth - 1
        gfin = g_final if last else jnp.ones_like(g_final)
        assert last, "deeper stacks need an un-normalised combine for inner layers"
        n_p = bp * seq
        y_p = _combine(x1_p, rt_p, mod_p, gfin, ys, tuple(p[:n_p] for p in pos), seq, min(cfg.tm_comb, seq))
        y_s = _combine(x1_s, rt_s, mod_s, gfin, ys, tuple(p[n_p:] for p in pos), db, db)

        pages = lambda a: a.reshape((bp, seq // ps, ps) + a.shape[2:])
        rows_p.append((pages(col(u_p3, "ka").reshape(bp, seq, 1, hd)),
                       pages(col(u_p3, "va").reshape(bp, seq, 1, hd)),
                       pages(col(u_p3, "kiwi")[..., :cfg.idx_dim]),
                       pages(col(u_p3, "kb").reshape(bp, seq, cfg.n_kv_b, hd)),
                       pages(col(u_p3, "vb").reshape(bp, seq, cfg.n_kv_b, hd))))
        rows_s.append((col(u_s, "ka").reshape(db, 1, 1, hd), col(u_s, "va").reshape(db, 1, 1, hd),
                       kiwi_s[:, :cfg.idx_dim].reshape(db, 1, cfg.idx_dim),
                       col(u_s, "kb").reshape(db, 1, cfg.n_kv_b, hd),
                       col(u_s, "vb").reshape(db, 1, cfg.n_kv_b, hd)))
        xp, xs = y_p.reshape(bp, seq, d), y_s.reshape(1, db, d)

    stack = lambda rows: tuple(jnp.stack(r) for r in zip(*rows))
    return (xp, xs.reshape(db, 1, d)) + stack(rows_p) + stack(rows_s)


def kernel(x_prompt, x_sample, cache_a_k, cache_a_v, cache_idx_k, cache_b_k, cache_b_v, page_table,
           c_prompt, c_sample, w_ada, b_ada, g_mix, g_ffn, g_final, w_in, w_up_a, w_up_b, w_out,
           w_group, b_group, w_router, b_router, w_gate_e, w_up_e, w_down_e):
    return _forward(Cfg(), x_prompt, x_sample, cache_a_k, cache_a_v, cache_idx_k, cache_b_k, cache_b_v,
                    page_table, c_prompt, c_sample, w_ada, b_ada, g_mix, g_ffn, g_final, w_in,
                    w_up_a, w_up_b, w_out, w_group, b_group, w_router, b_router, w_gate_e, w_up_e, w_down_e)
```

```python
import functools
from typing import NamedTuple

import numpy as np
import jax
import jax.numpy as jnp
from jax import lax
from jax.experimental import pallas as pl
from jax.experimental.pallas import tpu as pltpu

F32 = jnp.float32
BF16 = jnp.bfloat16
I32 = jnp.int32

LANES = 128
INT_MIN = -(2 ** 31)
RMS_EPS = 1e-6
NEG_BIG = -1e30
SB_DEAD_TAIL = -110.0
VMEM_LIMIT = 48 * 1024 * 1024


class Cfg(NamedTuple):
    d_model: int = 2048
    head_dim: int = 128
    n_heads_a: int = 8
    n_idx_heads: int = 16
    idx_dim: int = 64
    n_heads_b: int = 8
    n_kv_b: int = 4
    topk_max: int = 256
    n_groups: int = 8
    epg: int = 8
    top_k: int = 2
    d_ff: int = 512
    page_size: int = 128
    tn_in: int = 512
    tm_in: int = 1024
    tq_a: int = 256
    tk_a: int = 256
    t_b: int = 256
    tm_merge: int = 256
    moe_rows: int = 128
    tm_comb: int = 128
    pages_per_step: int = 16
    sb_pages_per_chunk: int = 4


def _cparams(sem):
    return pltpu.CompilerParams(dimension_semantics=sem, vmem_limit_bytes=VMEM_LIMIT)


def _round_up(a, b):
    return -(-a // b) * b


def _layout(cfg):
    d, hd = cfg.d_model, cfg.head_dim
    segs = [("ga", d), ("gb", d), ("qa", cfg.n_heads_a * hd), ("qi", cfg.n_idx_heads * cfg.idx_dim),
            ("qb", cfg.n_heads_b * hd), ("kb", cfg.n_kv_b * hd), ("vb", cfg.n_kv_b * hd),
            ("ka", hd), ("va", hd), ("kiwi", LANES)]
    off, o = {}, 0
    for name, w in segs:
        off[name] = (o, w)
        o += w
    return off, _round_up(o, cfg.tn_in)


def _permute_w_in(w_in, cfg):
    hd = cfg.head_dim
    sizes = [cfg.n_heads_a * hd, hd, hd, cfg.n_idx_heads * cfg.idx_dim, cfg.idx_dim, cfg.n_idx_heads,
             cfg.n_heads_b * hd, cfg.n_kv_b * hd, cfg.n_kv_b * hd, cfg.d_model, cfg.d_model]
    names = ["qa", "ka", "va", "qi", "ki", "wi", "qb", "kb", "vb", "ga", "gb"]
    cuts = np.concatenate([[0], np.cumsum(sizes)])
    src = {n: w_in[:, int(cuts[i]):int(cuts[i + 1])] for i, n in enumerate(names)}
    off, width = _layout(cfg)
    assert cfg.idx_dim + cfg.n_idx_heads <= LANES
    kiwi = jnp.concatenate(
        [src["ki"], src["wi"], jnp.zeros((w_in.shape[0], LANES - cfg.idx_dim - cfg.n_idx_heads), w_in.dtype)], axis=1)
    parts = [src["ga"], src["gb"], src["qa"], src["qi"], src["qb"], src["kb"], src["vb"], src["ka"], src["va"], kiwi]
    used = sum(p.shape[1] for p in parts)
    if width > used:
        parts.append(jnp.zeros((w_in.shape[0], width - used), w_in.dtype))
    return jnp.concatenate(parts, axis=1)


def _split(x):
    hi = x.astype(BF16)
    return hi, (x - hi.astype(F32)).astype(BF16)


def _dot3(a, b, dot=None):
    dot = dot or (lambda x, y: jnp.dot(x, y, preferred_element_type=F32))
    (ah, al), (bh, bl) = _split(a), _split(b)
    return dot(ah, bh) + dot(al, bh) + dot(ah, bl)


def _adaln_kernel(c_ref, w_ref, b_ref, o_ref):
    c = c_ref[...]
    o_ref[...] = _dot3(c * (1.0 / (1.0 + jnp.exp(-c))), w_ref[...]) + b_ref[...]


def _adaln(c_all, w_ada, b_ada):
    mc, d = c_all.shape
    n = w_ada.shape[1]
    tn = 1024 if n % 1024 == 0 else n
    return pl.pallas_call(
        _adaln_kernel,
        out_shape=jax.ShapeDtypeStruct((mc, n), F32),
        grid=(n // tn,),
        in_specs=[pl.BlockSpec((mc, d), lambda j: (0, 0)),
                  pl.BlockSpec((d, tn), lambda j: (0, j)),
                  pl.BlockSpec((1, tn), lambda j: (0, j))],
        out_specs=pl.BlockSpec((mc, tn), lambda j: (0, j)),
        compiler_params=_cparams(("arbitrary",)),
        name="adaln",
    )(c_all, w_ada, b_ada.reshape(1, n))


def _rms_mod_kernel(x_ref, g_ref, m_ref, o_ref):
    x = x_ref[0]
    y = x * lax.rsqrt(jnp.mean(x * x, axis=-1, keepdims=True) + RMS_EPS) * g_ref[...]
    o_ref[0] = (y * (1.0 + m_ref[0, 1]) + m_ref[0, 0]).astype(o_ref.dtype)


def _rms_mod(x3, g, mod, ts, out_dtype=None):
    bx, t, d = x3.shape
    r = mod.shape[2]
    out_dtype = out_dtype or BF16
    return pl.pallas_call(
        _rms_mod_kernel,
        out_shape=jax.ShapeDtypeStruct((bx, t, d), out_dtype),
        grid=(bx, t // ts),
        in_specs=[pl.BlockSpec((1, ts, d), lambda b, i: (b, i, 0)),
                  pl.BlockSpec((1, d), lambda b, i: (0, 0)),
                  pl.BlockSpec((1, 6, r, d), lambda b, i: (b, 0, 0, 0))],
        out_specs=pl.BlockSpec((1, ts, d), lambda b, i: (b, i, 0)),
        compiler_params=_cparams(("arbitrary", "arbitrary")),
        name="rms_mod",
    )(x3, g.reshape(1, d), mod)


def _mm_kernel(a_ref, w_ref, o_ref, wbf_ref):
    @pl.when(pl.program_id(1) == 0)
    def _():
        wbf_ref[...] = w_ref[...].astype(BF16)

    o_ref[...] = jnp.dot(a_ref[...], wbf_ref[...], preferred_element_type=F32).astype(o_ref.dtype)


def _mm3_kernel(a_ref, w_ref, o_ref, wbf_ref):
    del wbf_ref
    o_ref[...] = _dot3(a_ref[...], w_ref[...]).astype(o_ref.dtype)


def _matmul(a, w, tm, tn, out_dtype=F32):
    m, k = a.shape
    n = w.shape[1]
    return pl.pallas_call(
        _mm_kernel if a.dtype == BF16 else _mm3_kernel,
        out_shape=jax.ShapeDtypeStruct((m, n), out_dtype),
        grid=(n // tn, m // tm),
        in_specs=[pl.BlockSpec((tm, k), lambda j, i: (i, 0)),
                  pl.BlockSpec((k, tn), lambda j, i: (0, j))],
        out_specs=pl.BlockSpec((tm, tn), lambda j, i: (i, j)),
        scratch_shapes=[pltpu.VMEM((k, tn), BF16)],
        compiler_params=_cparams(("arbitrary", "arbitrary")),
        name="in_proj",
    )(a, w)


def _ordered_bits_to_float(u):
    key = u ^ jnp.int32(INT_MIN)
    return pltpu.bitcast(jnp.where(key < 0, key ^ jnp.int32(0x7FFFFFFF), key), F32)


def _kth_largest(count_ge, shape, k):
    def body(bi, res):
        cand = res | lax.shift_left(jnp.int32(1), jnp.int32(31) - bi)
        return jnp.where(count_ge(_ordered_bits_to_float(cand)) >= k, cand, res)

    thr = _ordered_bits_to_float(lax.fori_loop(0, 32, body, jnp.zeros(shape, I32)))
    return jnp.where(thr != thr, -jnp.inf, thr)


def _tie_cutoff(count_tie_lt, shape, need, nbits):
    def body(bi, x):
        cand = x | lax.shift_left(jnp.int32(1), jnp.int32(nbits - 1) - bi)
        return jnp.where(count_tie_lt(cand) < need, cand, x)

    return lax.fori_loop(0, nbits, body, jnp.zeros(shape, I32))


def _fold_lanes(x):
    acc = x[:, 0:LANES]
    for j in range(1, x.shape[1] // LANES):
        acc = acc + x[:, j * LANES:(j + 1) * LANES]
    return acc


def _dot_nt(a, b):
    return lax.dot_general(a, b, (((1,), (1,)), ((), ())), preferred_element_type=F32)


def _dsa_prompt_kernel(qa_ref, qi_ref, kwq_ref, kw_ref, ka_ref, va_ref, o_ref,
                       ka_s, kb_s, kbf_s, vt_s, sc_s, dist_s, xcut_s, qbf_s, m_s, l_s, acc_s, lg_s, p_s, al_s,
                       *, cfg, tq, tk, n_sel):
    hd, idim, ni, ha = cfg.head_dim, cfg.idx_dim, cfg.n_idx_heads, cfg.n_heads_a
    s_total = sc_s.shape[0]
    i = pl.program_id(1)
    t0 = i * tq
    nk = (t0 + tq + tk - 1) // tk

    @pl.when(i == 0)
    def _stage():
        kw = kw_ref[0]
        lane = lax.broadcasted_iota(I32, kw.shape, 1)
        k_lo = jnp.where(lane < idim, kw, 0.0)
        ka_s[...] = k_lo.astype(BF16)
        kb_s[...] = pltpu.roll(k_lo, idim, 1).astype(BF16)
        kbf_s[...] = ka_ref[0].astype(BF16)
        for c in range(s_total // tk):
            vt_s[:, c * tk:(c + 1) * tk] = va_ref[0, c * tk:(c + 1) * tk, :].T.astype(BF16)

    w_t = (kwq_ref[0] * (float(ni * idim) ** -0.5)).T
    qpos = t0 + lax.broadcasted_iota(I32, (tk, tq), 1)
    krow = lax.broadcasted_iota(I32, (tk, tq), 0)

    def score_chunk(c, carry):
        ks = pl.multiple_of(c * tk, tk)
        k_even = ka_s[pl.ds(ks, tk), :]
        k_odd = kb_s[pl.ds(ks, tk), :]
        acc = jnp.zeros((tk, tq), F32)
        for p in range(ni // 2):
            qp = qi_ref[0, :, p * LANES:(p + 1) * LANES].astype(BF16)
            for par, kk in ((0, k_even), (1, k_odd)):
                h = 2 * p + par
                acc = acc + w_t[idim + h:idim + h + 1, :] * jnp.maximum(_dot_nt(kk, qp), 0.0)
        sc_s[pl.ds(ks, tk), :] = jnp.where(ks + krow <= qpos, acc, -jnp.inf)
        return carry

    lax.fori_loop(0, nk, score_chunk, 0)

    def count(pred):
        def body(c, cnt):
            ks = pl.multiple_of(c * tk, tk)
            hit = pred(sc_s[pl.ds(ks, tk), :], ks + krow).astype(I32)
            return cnt + jnp.sum(hit.reshape(tk // 8, 8, tq), axis=0)

        cnt = lax.fori_loop(0, nk, body, jnp.zeros((8, tq), I32))
        return jnp.sum(cnt, axis=0, keepdims=True)

    thr = _kth_largest(lambda t: count(lambda s, k: s >= t), (1, tq), n_sel)

    xcut_s[...] = jnp.full(xcut_s.shape, s_total, I32)

    @pl.when(jnp.max(count(lambda s, k: s >= thr)) > n_sel)
    def _ties():
        need = n_sel - count(lambda s, k: s > thr)
        x = _tie_cutoff(lambda cand: count(lambda s, k: (s == thr) & (k < cand)), (1, tq), need,
                        max(1, (s_total - 1).bit_length()))
        xcut_s[...] = jnp.broadcast_to(x, xcut_s.shape)

    xcut = xcut_s[0:1, :]

    def mask_chunk(c, carry):
        ks = pl.multiple_of(c * tk, tk)
        kpos = ks + krow
        s = sc_s[pl.ds(ks, tk), :]
        sel = ((s > thr) | ((s == thr) & (kpos <= xcut))) & (kpos <= qpos)
        dist_s[pl.ds(ks, tk), :] = jnp.where(sel, (qpos - kpos).astype(F32), -NEG_BIG)
        return carry

    lax.fori_loop(0, nk, mask_chunk, 0)
    scale = float(hd) ** -0.5

    for h in range(ha):
        qbf_s[h] = qa_ref[0, :, h * hd:(h + 1) * hd].astype(BF16)
    m_s[...] = jnp.full(m_s.shape, NEG_BIG, F32)
    l_s[...] = jnp.zeros(l_s.shape, F32)
    acc_s[...] = jnp.zeros(acc_s.shape, F32)

    def att_chunk(c, carry):
        ks = pl.multiple_of(c * tk, tk)
        kc = kbf_s[pl.ds(ks, tk), :]
        vt = vt_s[:, pl.ds(ks, tk)]
        dist = dist_s[pl.ds(ks, tk), :]
        for h in range(ha):
            lg_s[h] = _dot_nt(kc, qbf_s[h])
        for h in range(ha):
            slope = float(2.0 ** (-8.0 * (h + 1) / ha))
            lg = lg_s[h] * scale - slope * dist
            m = m_s[h]
            m_new = jnp.maximum(m, jnp.max(lg, axis=0, keepdims=True))
            p = jnp.exp(lg - m_new)
            alpha = jnp.exp(m - m_new)
            l_s[h] = alpha * l_s[h] + jnp.sum(p, axis=0, keepdims=True)
            p_s[h] = p.astype(BF16)
            al_s[h] = alpha
            m_s[h] = m_new
        for h in range(ha):
            acc_s[h] = al_s[h] * acc_s[h] + jnp.dot(vt, p_s[h], preferred_element_type=F32)
        return carry

    lax.fori_loop(0, nk, att_chunk, 0)
    for h in range(ha):
        o_ref[0, :, h * hd:(h + 1) * hd] = (acc_s[h] / l_s[h]).T.astype(o_ref.dtype)


def _dsa_prompt(u3, cfg):
    b, s, _ = u3.shape
    off, _ = _layout(cfg)
    tq, tk = min(cfg.tq_a, s), min(cfg.tk_a, s)
    n_sel = min(cfg.topk_max, s // 4)
    aw = cfg.n_heads_a * cfg.head_dim
    iw = cfg.n_idx_heads * cfg.idx_dim
    assert 2 * cfg.idx_dim == LANES and s % tk == 0 and s % tq == 0 and tk >= n_sel
    cb = lambda name, w: off[name][0] // w
    kern = functools.partial(_dsa_prompt_kernel, cfg=cfg, tq=tq, tk=tk, n_sel=n_sel)
    return pl.pallas_call(
        kern,
        out_shape=jax.ShapeDtypeStruct((b, s, aw), BF16),
        grid=(b, s // tq),
        in_specs=[pl.BlockSpec((1, tq, aw), lambda bi, i: (bi, i, cb("qa", aw))),
                  pl.BlockSpec((1, tq, iw), lambda bi, i: (bi, i, cb("qi", iw))),
                  pl.BlockSpec((1, tq, LANES), lambda bi, i: (bi, i, cb("kiwi", LANES))),
                  pl.BlockSpec((1, s, LANES), lambda bi, i: (bi, 0, cb("kiwi", LANES))),
                  pl.BlockSpec((1, s, LANES), lambda bi, i: (bi, 0, cb("ka", LANES))),
                  pl.BlockSpec((1, s, LANES), lambda bi, i: (bi, 0, cb("va", LANES)))],
        out_specs=pl.BlockSpec((1, tq, aw), lambda bi, i: (bi, i, 0)),
        scratch_shapes=[pltpu.VMEM((s, LANES), BF16), pltpu.VMEM((s, LANES), BF16),
                        pltpu.VMEM((s, LANES), BF16), pltpu.VMEM((cfg.head_dim, s), BF16),
                        pltpu.VMEM((s, tq), F32), pltpu.VMEM((s, tq), F32), pltpu.VMEM((8, tq), I32),
                        pltpu.VMEM((cfg.n_heads_a, tq, cfg.head_dim), BF16),
                        pltpu.VMEM((cfg.n_heads_a, 1, tq), F32), pltpu.VMEM((cfg.n_heads_a, 1, tq), F32),
                        pltpu.VMEM((cfg.n_heads_a, cfg.head_dim, tq), F32),
                        pltpu.VMEM((cfg.n_heads_a, tk, tq), F32), pltpu.VMEM((cfg.n_heads_a, tk, tq), BF16),
                        pltpu.VMEM((cfg.n_heads_a, 1, tq), F32)],
        compiler_params=_cparams(("arbitrary", "arbitrary")),
        name="dsa_prompt",
    )(u3, u3, u3, u3, u3, u3)


def _sb_terms(q, kc, scale):
    z = _dot_nt(q, kc) * scale
    lk = -(jnp.maximum(z, 0.0) + jnp.log(1.0 + jnp.exp(-jnp.abs(z))))
    return z, lk


def _suffix_in_chunk(lk, upper):
    hi = lk.astype(BF16)
    lo = (lk - hi.astype(F32)).astype(BF16)
    return (jnp.dot(hi, upper, preferred_element_type=F32) + jnp.dot(lo, upper, preferred_element_type=F32))


def _strict_lower_ones(n):
    return (lax.broadcasted_iota(I32, (n, n), 0) > lax.broadcasted_iota(I32, (n, n), 1)).astype(BF16)


def _sb_prompt_kernel(q_ref, k_ref, v_ref, o_ref, kbf_s, vbf_s, *, t, hd, rep):
    i = pl.program_id(2)

    @pl.when(i == 0)
    def _stage():
        kbf_s[...] = k_ref[0].astype(BF16)
        vbf_s[...] = v_ref[0].astype(BF16)

    qs = [q_ref[0, :, r * hd:(r + 1) * hd].astype(BF16) for r in range(rep)]
    scale = float(hd) ** -0.5
    upper = _strict_lower_ones(t)
    before = lax.broadcasted_iota(I32, (t, t), 1) < lax.broadcasted_iota(I32, (t, t), 0)

    ks = pl.multiple_of(i * t, t)
    kc, vc = kbf_s[pl.ds(ks, t), :], vbf_s[pl.ds(ks, t), :]
    state = []
    for q in qs:
        z, lk = _sb_terms(q, kc, scale)
        lkm = jnp.where(before, lk, 0.0)
        tail = _suffix_in_chunk(lkm, upper)
        w = jnp.where(before, jnp.exp(z + lk + tail), 0.0)
        state.append(jnp.sum(lkm, axis=-1, keepdims=True))
        state.append(jnp.dot(w.astype(BF16), vc, preferred_element_type=F32))

    def live(st):
        top = st[0]
        for r in range(1, rep):
            top = jnp.maximum(top, st[2 * r])
        return jnp.max(top) > SB_DEAD_TAIL

    def body(loop):
        j, _, st = loop
        ks = pl.multiple_of((i - 1 - j) * t, t)
        kc, vc = kbf_s[pl.ds(ks, t), :], vbf_s[pl.ds(ks, t), :]
        out = []
        for r, q in enumerate(qs):
            carry, acc = st[2 * r], st[2 * r + 1]
            z, lk = _sb_terms(q, kc, scale)
            tail = carry + _suffix_in_chunk(lk, upper)
            w = jnp.exp(z + lk + tail)
            out.append(carry + jnp.sum(lk, axis=-1, keepdims=True))
            out.append(acc + jnp.dot(w.astype(BF16), vc, preferred_element_type=F32))
        return j + 1, live(out), tuple(out)

    _, _, st = lax.while_loop(lambda loop: (loop[0] < i) & loop[1], body, (jnp.int32(0), live(state), tuple(state)))
    for r in range(rep):
        o_ref[0, :, r * hd:(r + 1) * hd] = st[2 * r + 1].astype(o_ref.dtype)


def _sb_prompt(u3, cfg):
    b, s, _ = u3.shape
    off, _ = _layout(cfg)
    hd, hb, g = cfg.head_dim, cfg.n_heads_b, cfg.n_kv_b
    rep = hb // g
    t = min(cfg.t_b, s)
    assert s % t == 0
    qw = rep * hd
    assert off["qb"][0] % qw == 0
    qb0, kb0, vb0 = off["qb"][0] // qw, off["kb"][0] // hd, off["vb"][0] // hd
    kern = functools.partial(_sb_prompt_kernel, t=t, hd=hd, rep=rep)
    return pl.pallas_call(
        kern,
        out_shape=jax.ShapeDtypeStruct((b, s, hb * hd), BF16),
        grid=(b, g, s // t),
        in_specs=[pl.BlockSpec((1, t, qw), lambda bi, gi, i: (bi, i, qb0 + gi)),
                  pl.BlockSpec((1, s, hd), lambda bi, gi, i: (bi, 0, kb0 + gi)),
                  pl.BlockSpec((1, s, hd), lambda bi, gi, i: (bi, 0, vb0 + gi))],
        out_specs=pl.BlockSpec((1, t, qw), lambda bi, gi, i: (bi, i, gi)),
        scratch_shapes=[pltpu.VMEM((s, hd), BF16), pltpu.VMEM((s, hd), BF16)],
        compiler_params=_cparams(("arbitrary", "arbitrary", "arbitrary")),
        name="sb_prompt",
    )(u3, u3, u3)


def _page_specs(n, block, page_of):
    zeros = (0,) * (len(block) - 1)
    return [pl.BlockSpec((None,) + tuple(block[1:]),
                         functools.partial(lambda b, j, pt, r: (page_of(b, j, r, pt),) + zeros, r=r))
            for r in range(n)]


def _dsa_sample_score_kernel(pt_ref, q_ref, w_ref, *rest, pg):
    k_refs, o_ref = rest[:pg], rest[pg]
    q = q_ref[0]
    w = w_ref[0]
    for r in range(pg):
        d = _dot3(q, k_refs[r][...])
        o_ref[0, 0, r:r + 1, :] = jnp.sum(w * jnp.maximum(d, 0.0), axis=0, keepdims=True)


def _dsa_sample_scores(qi3, wi3, cache_ik_t, page_table, cfg):
    db, ni, idim = qi3.shape
    pool, _, ps = cache_ik_t.shape
    npg = page_table.shape[1]
    pg = min(cfg.pages_per_step, npg)
    assert npg % pg == 0 and ps == LANES
    page_of = lambda b, j, r, pt: pt[b * npg + j * pg + r]
    grid_spec = pltpu.PrefetchScalarGridSpec(
        num_scalar_prefetch=1,
        grid=(db, npg // pg),
        in_specs=[pl.BlockSpec((1, ni, idim), lambda b, j, pt: (b, 0, 0)),
                  pl.BlockSpec((1, ni, 1), lambda b, j, pt: (b, 0, 0))]
                 + _page_specs(pg, (1, idim, ps), page_of),
        out_specs=pl.BlockSpec((1, 1, pg, ps), lambda b, j, pt: (b, j, 0, 0)),
    )
    out = pl.pallas_call(
        functools.partial(_dsa_sample_score_kernel, pg=pg),
        out_shape=jax.ShapeDtypeStruct((db, npg // pg, pg, ps), F32),
        grid_spec=grid_spec,
        compiler_params=_cparams(("arbitrary", "arbitrary")),
        name="dsa_sample_scores",
    )(page_table.reshape(-1), qi3, wi3, *([cache_ik_t] * pg))
    return out


def _dsa_sample_select_kernel(sc_ref, qi_ref, kw_ref, bias_ref, bnew_ref, *, cfg, n_sel, chunk):
    idim, ni = cfg.idx_dim, cfg.n_idx_heads
    db, p = sc_ref.shape
    nchunk = p // chunk
    kw = kw_ref[...]
    ki = kw[:, 0:idim]
    s_new = jnp.zeros((db, 1), F32)
    for h in range(ni):
        d = jnp.sum(qi_ref[:, h * idim:(h + 1) * idim] * ki, axis=-1, keepdims=True)
        s_new = s_new + (kw[:, idim + h:idim + h + 1] * (float(ni * idim) ** -0.5)) * jnp.maximum(d, 0.0)
    kn1 = s_new
    idx0 = lax.broadcasted_iota(I32, (db, chunk), 1)

    def count(pred, new_hit):
        cnt = jnp.zeros((db, LANES), I32)
        for c in range(nchunk):
            cnt = cnt + _fold_lanes(pred(sc_ref[:, c * chunk:(c + 1) * chunk], c * chunk + idx0).astype(I32))
        return jnp.sum(cnt, axis=-1, keepdims=True) + new_hit.astype(I32)

    thr = _kth_largest(lambda t: count(lambda k, i: k >= t, kn1 >= t), (db, 1), n_sel)
    need = n_sel - count(lambda k, i: k > thr, kn1 > thr)
    xcut = _tie_cutoff(lambda x: count(lambda k, i: (k == thr) & (i < x), (kn1 == thr) & (p < x)),
                       (db, 1), need, max(1, p.bit_length()))
    for c in range(nchunk):
        k = sc_ref[:, c * chunk:(c + 1) * chunk]
        sel = (k > thr) | ((k == thr) & (c * chunk + idx0 <= xcut))
        bias_ref[:, c * chunk:(c + 1) * chunk] = jnp.where(sel, 0.0, NEG_BIG)
    sel_new = (kn1 > thr) | ((kn1 == thr) & (p <= xcut))
    bnew_ref[...] = jnp.broadcast_to(jnp.where(sel_new, 0.0, NEG_BIG), (db, LANES))


def _dsa_sample_select(scores, u_s, cfg):
    db, p = scores.shape
    off, _ = _layout(cfg)
    iw = cfg.n_idx_heads * cfg.idx_dim
    n_sel = min(cfg.topk_max, (p + 1) // 4)
    chunk = min(2048, p)
    assert p % chunk == 0 and p + 1 >= n_sel
    kern = functools.partial(_dsa_sample_select_kernel, cfg=cfg, n_sel=n_sel, chunk=chunk)
    return pl.pallas_call(
        kern,
        out_shape=(jax.ShapeDtypeStruct((db, p), F32), jax.ShapeDtypeStruct((db, LANES), F32)),
        grid=(1,),
        in_specs=[pl.BlockSpec((db, p), lambda i: (0, 0)),
                  pl.BlockSpec((db, iw), lambda i: (0, off["qi"][0] // iw)),
                  pl.BlockSpec((db, LANES), lambda i: (0, off["kiwi"][0] // LANES))],
        out_specs=(pl.BlockSpec((db, p), lambda i: (0, 0)), pl.BlockSpec((db, LANES), lambda i: (0, 0))),
        compiler_params=_cparams(("arbitrary",)),
        name="dsa_sample_select",
    )(scores, u_s, u_s)


def _dsa_sample_attend_kernel(pt_ref, q_ref, bias_ref, bnew_ref, knew_ref, vnew_ref, *rest,
                              pg, cfg, past):
    k_refs, v_refs = rest[:pg], rest[pg:2 * pg]
    o_ref, m_s, l_s, acc_s = rest[2 * pg:]
    hd, ha = cfg.head_dim, cfg.n_heads_a
    j = pl.program_id(1)
    scale = float(hd) ** -0.5

    @pl.when(j == 0)
    def _init():
        m_s[...] = jnp.full(m_s.shape, NEG_BIG, F32)
        l_s[...] = jnp.zeros(l_s.shape, F32)
        acc_s[...] = jnp.zeros(acc_s.shape, F32)

    q = q_ref[0]
    hidx = lax.broadcasted_iota(I32, (ha, 1), 0)
    slopes = jnp.zeros((ha, 1), F32)
    for h in range(ha):
        slopes = jnp.where(hidx == h, float(2.0 ** (-8.0 * (h + 1) / ha)), slopes)

    def update(lg, bias, vs):
        sel = jnp.broadcast_to(bias == 0.0, lg.shape)
        lg = lg + bias
        m = m_s[...]
        m_new = jnp.maximum(m, jnp.max(lg, axis=-1, keepdims=True))
        p = jnp.where(sel, jnp.exp(lg - m_new), 0.0)
        alpha = jnp.exp(m - m_new)
        l_s[...] = alpha * l_s[...] + jnp.sum(p, axis=-1, keepdims=True)
        w = vs[0].shape[0]
        pv = _dot3(p[:, 0:w], vs[0])
        for r in range(1, len(vs)):
            pv = pv + _dot3(p[:, r * w:(r + 1) * w], vs[r])
        acc_s[...] = alpha * acc_s[...] + pv
        m_s[...] = m_new

    n = pg * LANES
    pos = j * n + lax.broadcasted_iota(I32, (1, n), 1)
    lg = jnp.concatenate([_dot3(q, k_refs[r][...], _dot_nt) for r in range(pg)], axis=1)
    lg = lg * scale - slopes * (past - pos).astype(F32)
    update(lg, bias_ref[0, 0], [v_refs[r][...] for r in range(pg)])

    @pl.when(j == pl.num_programs(1) - 1)
    def _fin():
        kn = jnp.broadcast_to(knew_ref[0], (16, hd))
        vn = jnp.broadcast_to(vnew_ref[0], (16, hd))
        lg = _dot3(q, kn, _dot_nt) * scale
        first = lax.broadcasted_iota(I32, (1, 16), 1) == 0
        update(lg, jnp.where(first, bnew_ref[0][:, 0:16], NEG_BIG), [vn])
        o_ref[0] = (acc_s[...] / l_s[...]).astype(o_ref.dtype)


def _dsa_sample_attend(qa3, bias, bias_new, u_s3, cache_k, cache_v, page_table, cfg):
    db, ha, hd = qa3.shape
    npg = page_table.shape[1]
    pg = min(cfg.pages_per_step, npg)
    ps = cache_k.shape[1]
    bias4 = bias.reshape(db, npg // pg, 1, pg * ps)
    off, _ = _layout(cfg)
    page_of = lambda b, j, r, pt: pt[b * npg + j * pg + r]
    kern = functools.partial(_dsa_sample_attend_kernel, pg=pg, cfg=cfg, past=npg * ps)
    grid_spec = pltpu.PrefetchScalarGridSpec(
        num_scalar_prefetch=1,
        grid=(db, npg // pg),
        in_specs=[pl.BlockSpec((1, ha, hd), lambda b, j, pt: (b, 0, 0)),
                  pl.BlockSpec((1, 1, 1, pg * ps), lambda b, j, pt: (b, j, 0, 0)),
                  pl.BlockSpec((1, 1, LANES), lambda b, j, pt: (b, 0, 0)),
                  pl.BlockSpec((1, 1, hd), lambda b, j, pt: (b, 0, off["ka"][0] // hd)),
                  pl.BlockSpec((1, 1, hd), lambda b, j, pt: (b, 0, off["va"][0] // hd))]
                 + _page_specs(pg, (1, ps, hd), page_of) + _page_specs(pg, (1, ps, hd), page_of),
        out_specs=pl.BlockSpec((1, ha, hd), lambda b, j, pt: (b, 0, 0)),
        scratch_shapes=[pltpu.VMEM((ha, 1), F32), pltpu.VMEM((ha, 1), F32), pltpu.VMEM((ha, hd), F32)],
    )
    return pl.pallas_call(
        kern,
        out_shape=jax.ShapeDtypeStruct((db, ha, hd), F32),
        grid_spec=grid_spec,
        compiler_params=_cparams(("arbitrary", "arbitrary")),
        name="dsa_sample_attend",
    )(page_table.reshape(-1), qa3, bias4, bias_new.reshape(db, 1, LANES),
      u_s3, u_s3, *([cache_k] * pg), *([cache_v] * pg))


def _sb_sample_kernel(pt_ref, q_ref, k_hbm, v_hbm, o_ref, kbuf, vbuf, sem, *, pg, npg, cfg):
    hd, hb, g = cfg.head_dim, cfg.n_heads_b, cfg.n_kv_b
    rep = hb // g
    b = pl.program_id(0)
    nchunks = npg // pg
    scale = float(hd) ** -0.5

    def page_copies(c, slot):
        out = []
        for r in range(pg):
            page = pt_ref[b * npg + (npg - 1 - (c * pg + r))]
            out.append(pltpu.make_async_copy(k_hbm.at[page], kbuf.at[slot, r], sem.at[0, slot]))
            out.append(pltpu.make_async_copy(v_hbm.at[page], vbuf.at[slot, r], sem.at[1, slot]))
        return out

    q = q_ref[0]
    qg = [q[:, gi * hd:(gi + 1) * hd] for gi in range(g)]
    upper = _strict_lower_ones(LANES)

    for cp in page_copies(0, 0):
        cp.start()

    def body(loop):
        c, _, carry, accs = loop
        slot = c % 2

        @pl.when(c + 1 < nchunks)
        def _prefetch():
            for cp in page_copies(c + 1, 1 - slot):
                cp.start()

        for cp in page_copies(c, slot):
            cp.wait()
        zs = []
        for r in range(pg):
            z = _dot3(qg[0], kbuf[slot, r, :, 0, :], _dot_nt)
            for gi in range(1, g):
                z = z + _dot3(qg[gi], kbuf[slot, r, :, gi, :], _dot_nt)
            zs.append(z * scale)
        z_all = jnp.concatenate(zs, axis=0)
        lk_all = -(jnp.maximum(z_all, 0.0) + jnp.log(1.0 + jnp.exp(-jnp.abs(z_all))))
        tail_in = _suffix_in_chunk(lk_all, upper)
        tot = jnp.sum(lk_all, axis=-1, keepdims=True)
        accs = list(accs)
        for r in range(pg):
            sl = slice(r * hb, (r + 1) * hb)
            w = jnp.exp(z_all[sl] + lk_all[sl] + (carry + tail_in[sl]))
            for gi in range(g):
                accs[gi] = accs[gi] + _dot3(w, vbuf[slot, r, :, gi, :])
            carry = carry + tot[sl]
        return c + 1, jnp.max(carry) > SB_DEAD_TAIL, carry, tuple(accs)

    init = (jnp.int32(0), jnp.bool_(True), jnp.zeros((hb, 1), F32), tuple(jnp.zeros((hb, hd), F32) for _ in range(g)))
    c_end, _, _, accs = lax.while_loop(lambda loop: (loop[0] < nchunks) & loop[1], body, init)

    @pl.when(c_end < nchunks)
    def _drain():
        for cp in page_copies(c_end, c_end % 2):
            cp.wait()

    head_group = lax.broadcasted_iota(I32, (hb, hd), 0) // rep
    out = jnp.zeros((hb, hd), F32)
    for gi in range(g):
        out = out + jnp.where(head_group == gi, accs[gi], 0.0)
    o_ref[0] = out.astype(o_ref.dtype)


def _sb_sample(q_bd, cache_k, cache_v, page_table, cfg):
    db, hb, gw = q_bd.shape
    npg = page_table.shape[1]
    _, ps, g, hd = cache_k.shape
    pg = min(cfg.sb_pages_per_chunk, npg)
    assert npg % pg == 0 and ps == LANES and g * hd == gw
    grid_spec = pltpu.PrefetchScalarGridSpec(
        num_scalar_prefetch=1,
        grid=(db,),
        in_specs=[pl.BlockSpec((1, hb, gw), lambda b, pt: (b, 0, 0)),
                  pl.BlockSpec(memory_space=pl.ANY), pl.BlockSpec(memory_space=pl.ANY)],
        out_specs=pl.BlockSpec((1, hb, cfg.head_dim), lambda b, pt: (b, 0, 0)),
        scratch_shapes=[pltpu.VMEM((2, pg, ps, g, hd), F32), pltpu.VMEM((2, pg, ps, g, hd), F32),
                        pltpu.SemaphoreType.DMA((2, 2))],
    )
    return pl.pallas_call(
        functools.partial(_sb_sample_kernel, pg=pg, npg=npg, cfg=cfg),
        out_shape=jax.ShapeDtypeStruct((db, hb, cfg.head_dim), F32),
        grid_spec=grid_spec,
        compiler_params=_cparams(("arbitrary",)),
        name="sb_sample",
    )(page_table.reshape(-1), q_bd, cache_k, cache_v)


def _merge_kernel(x_ref, oa_ref, ob_ref, ga_ref, gb_ref, m_ref, gf_ref, wua_ref, wub_ref, wo_ref, wr_ref, br_ref,
                  x1_ref, h2_ref, rt_ref, *, cfg):
    ng, epg = cfg.n_groups, cfg.epg
    sig = lambda v: 1.0 / (1.0 + jnp.exp(-v))

    def mm(a, w_ref):
        dot = lambda p, q: jnp.dot(p, q, preferred_element_type=F32)
        if w_ref.shape[0] == 1:
            return dot(a.astype(BF16), w_ref[0])
        a_hi, a_lo = _split(a)
        return dot(a_hi, w_ref[0]) + dot(a_lo, w_ref[0]) + dot(a_hi, w_ref[1])

    merged = sig(ga_ref[...]) * mm(oa_ref[...], wua_ref) + sig(gb_ref[...]) * mm(ob_ref[...], wub_ref)
    x1 = x_ref[0] + m_ref[0, 2] * mm(merged, wo_ref)
    x1_ref[...] = x1
    h2 = x1 * lax.rsqrt(jnp.mean(x1 * x1, axis=-1, keepdims=True) + RMS_EPS) * gf_ref[...]
    h2 = h2 * (1.0 + m_ref[0, 4]) + m_ref[0, 3]
    h2_ref[...] = h2
    logits = mm(h2, wr_ref) + br_ref[...]

    lane = lax.broadcasted_iota(I32, logits.shape, 1).astype(F32)
    first = lambda hit: jnp.min(jnp.where(hit, lane, float(LANES)), axis=-1, keepdims=True)
    in_g = lane < ng
    gl = jnp.where(in_g, logits, -jnp.inf)
    gmax = jnp.max(gl, axis=-1, keepdims=True)
    gidx = first(gl == gmax)
    p_g = 1.0 / jnp.sum(jnp.where(in_g, jnp.exp(logits - gmax), 0.0), axis=-1, keepdims=True)
    lo = ng + gidx * epg
    el = jnp.where((lane >= lo) & (lane < lo + epg), logits, -jnp.inf)
    v1 = jnp.max(el, axis=-1, keepdims=True)
    i1 = first(el == v1)
    el2 = jnp.where(lane == i1, -jnp.inf, el)
    v2 = jnp.max(el2, axis=-1, keepdims=True)
    i2 = first(el2 == v2)
    e21 = jnp.exp(v2 - v1)
    w1 = 1.0 / (1.0 + e21)
    w2 = e21 / (1.0 + e21)
    rt = jnp.where(lane == 0, i1 - ng, jnp.where(lane == 1, i2 - ng,
         jnp.where(lane == 2, p_g * w1, jnp.where(lane == 3, p_g * w2, 0.0))))
    rt_ref[...] = rt


def _merge(x3, oa, ob, u, mod, g_ffn, wua, wub, wo, wr, br, cfg, tm):
    bx, t, d = x3.shape
    m = bx * t
    off, _ = _layout(cfg)
    aw, bw = oa.shape[1], ob.shape[1]
    r = mod.shape[2]
    nt = t // tm
    const = lambda shape: pl.BlockSpec(shape, lambda b, i: (0,) * len(shape), pipeline_mode=pl.Buffered(1))
    return pl.pallas_call(
        functools.partial(_merge_kernel, cfg=cfg),
        out_shape=(jax.ShapeDtypeStruct((m, d), F32), jax.ShapeDtypeStruct((m, d), F32),
                   jax.ShapeDtypeStruct((m, LANES), F32)),
        grid=(bx, nt),
        in_specs=[pl.BlockSpec((1, tm, d), lambda b, i: (b, i, 0)),
                  pl.BlockSpec((tm, aw), lambda b, i: (b * nt + i, 0)),
                  pl.BlockSpec((tm, bw), lambda b, i: (b * nt + i, 0)),
                  pl.BlockSpec((tm, d), lambda b, i: (b * nt + i, off["ga"][0] // d)),
                  pl.BlockSpec((tm, d), lambda b, i: (b * nt + i, off["gb"][0] // d)),
                  pl.BlockSpec((1, 6, r, d), lambda b, i: (b, 0, 0, 0)),
                  const((1, d)), const(wua.shape), const(wub.shape), const(wo.shape), const((2, d, LANES)),
                  const((1, LANES))],
        out_specs=(pl.BlockSpec((tm, d), lambda b, i: (b * nt + i, 0)),
                   pl.BlockSpec((tm, d), lambda b, i: (b * nt + i, 0)),
                   pl.BlockSpec((tm, LANES), lambda b, i: (b * nt + i, 0))),
        compiler_params=_cparams(("arbitrary", "arbitrary")),
        name="merge_out_router",
    )(x3, oa, ob, u, u, mod, g_ffn.reshape(1, d), wua, wub, wo, wr, br)


def _moe_kernel(be_ref, nb_ref, tok_ref, tokn_ref, h_hbm, wg_ref, wu_ref, wd_ref, y_ref,
                xbuf, sem, wg_s, wu_s, wd_s, *, rows):
    i = pl.program_id(0)
    nb = nb_ref[0]

    def row_copy(tok, slot, r):
        return pltpu.make_async_copy(h_hbm.at[pl.ds(tok, 1)], xbuf.at[slot, pl.ds(r, 1)], sem.at[slot])

    def issue(idx_ref, slot):
        def body(r, c):
            row_copy(idx_ref[0, 0, r], slot, r).start()
            return c

        lax.fori_loop(0, rows, body, 0, unroll=8)

    def wait_slot(slot):
        pltpu.make_async_copy(h_hbm.at[pl.ds(0, rows)], xbuf.at[slot], sem.at[slot]).wait()

    @pl.when(i == 0)
    def _prime():
        issue(tok_ref, 0)

    @pl.when(i + 1 < nb)
    def _prefetch():
        issue(tokn_ref, (i + 1) % 2)

    @pl.when(i < nb)
    def _compute():
        slot = i % 2
        wait_slot(slot)

        @pl.when((i == 0) | (be_ref[i] != be_ref[jnp.maximum(i - 1, 0)]))
        def _cast():
            wg_s[...] = wg_ref[...].astype(BF16)
            wu_s[...] = wu_ref[...].astype(BF16)
            wd_s[...] = wd_ref[...].astype(BF16)

        x = xbuf[slot].astype(BF16)
        gate = jnp.dot(x, wg_s[...], preferred_element_type=F32)
        up = jnp.dot(x, wu_s[...], preferred_element_type=F32)
        act = (gate * (1.0 / (1.0 + jnp.exp(-gate))) * up).astype(BF16)
        y_ref[...] = jnp.dot(act, wd_s[...], preferred_element_type=F32)

    @pl.when(i >= nb)
    def _idle():
        y_ref[...] = jnp.zeros(y_ref.shape, F32)


def _moe_experts(h2_all, slot_token, block_expert, n_used, w_gate_e, w_up_e, w_down_e, rows):
    n_blocks = block_expert.shape[0]
    e, d, f = w_gate_e.shape
    tok3 = slot_token.reshape(n_blocks, 1, rows)
    grid_spec = pltpu.PrefetchScalarGridSpec(
        num_scalar_prefetch=2,
        grid=(n_blocks,),
        in_specs=[pl.BlockSpec((1, 1, rows), lambda i, be, nb: (i, 0, 0), memory_space=pltpu.SMEM),
                  pl.BlockSpec((1, 1, rows), lambda i, be, nb: (jnp.minimum(i + 1, n_blocks - 1), 0, 0),
                               memory_space=pltpu.SMEM),
                  pl.BlockSpec(memory_space=pl.ANY),
                  pl.BlockSpec((None, d, f), lambda i, be, nb: (be[i], 0, 0)),
                  pl.BlockSpec((None, d, f), lambda i, be, nb: (be[i], 0, 0)),
                  pl.BlockSpec((None, f, d), lambda i, be, nb: (be[i], 0, 0))],
        out_specs=pl.BlockSpec((rows, d), lambda i, be, nb: (i, 0)),
        scratch_shapes=[pltpu.VMEM((2, rows, d), F32), pltpu.SemaphoreType.DMA((2,)),
                        pltpu.VMEM((d, f), BF16), pltpu.VMEM((d, f), BF16), pltpu.VMEM((f, d), BF16)],
    )
    return pl.pallas_call(
        functools.partial(_moe_kernel, rows=rows),
        out_shape=jax.ShapeDtypeStruct((n_blocks * rows, d), F32),
        grid_spec=grid_spec,
        compiler_params=_cparams(("arbitrary",)),
        name="moe_experts",
    )(block_expert, n_used, tok3, tok3, h2_all, w_gate_e, w_up_e, w_down_e)


def _rank_kernel(rt_ref, base_ref, dest_ref, tot_ref, base_s, *, tm):
    @pl.when(pl.program_id(0) == 0)
    def _init():
        base_s[...] = base_ref[...]

    rt = rt_ref[...]
    lane = lax.broadcasted_iota(I32, rt.shape, 1).astype(F32)
    o1 = (lane == rt[:, 0:1]).astype(F32)
    o2 = (lane == rt[:, 1:2]).astype(F32)
    lower = _strict_lower_ones(tm)
    p1 = jnp.dot(lower, o1.astype(BF16), preferred_element_type=F32)
    p2 = jnp.dot(lower, o2.astype(BF16), preferred_element_type=F32)
    c1 = jnp.sum(o1, axis=0, keepdims=True)
    c2 = jnp.sum(o2, axis=0, keepdims=True)
    base = base_s[...]
    d1 = jnp.sum((base + p1) * o1, axis=-1, keepdims=True)
    d2 = jnp.sum((base + c1 + p2) * o2, axis=-1, keepdims=True)
    dest_ref[...] = jnp.where(lane == 0, d1, jnp.where(lane == 1, d2, 0.0))
    base_s[...] = base + c1 + c2
    tot_ref[...] = base + c1 + c2


def _rank(rt_pad, base, tm):
    n = rt_pad.shape[0]
    return pl.pallas_call(
        functools.partial(_rank_kernel, tm=tm),
        out_shape=(jax.ShapeDtypeStruct((n, LANES), F32), jax.ShapeDtypeStruct((1, LANES), F32)),
        grid=(n // tm,),
        in_specs=[pl.BlockSpec((tm, LANES), lambda i: (i, 0)), pl.BlockSpec((1, LANES), lambda i: (0, 0))],
        out_specs=(pl.BlockSpec((tm, LANES), lambda i: (i, 0)), pl.BlockSpec((1, LANES), lambda i: (0, 0))),
        scratch_shapes=[pltpu.VMEM((1, LANES), F32)],
        compiler_params=_cparams(("arbitrary",)),
        name="moe_rank",
    )(rt_pad, base)


def _moe_plan(rt_all, n_experts, rows, top_k, tm):
    n_tok = rt_all.shape[0]
    a = n_tok * top_k
    assert top_k == 2 and n_experts <= LANES
    n_blocks = -(-(a + n_experts * (rows - 1)) // rows)
    n_pad = _round_up(n_tok, tm)
    rt_pad = jnp.concatenate([rt_all, jnp.full((n_pad - n_tok, LANES), -1.0, F32)], axis=0)
    _, tot = _rank(rt_pad, jnp.zeros((1, LANES), F32), tm)
    counts = tot[0, :n_experts].astype(I32)
    padded = (counts + rows - 1) // rows * rows
    cum = jnp.cumsum(padded)
    pstart = jnp.zeros((1, LANES), F32).at[0, :n_experts].set((cum - padded).astype(F32))
    dest, _ = _rank(rt_pad, pstart, tm)
    pos = (dest[:n_tok, 0].astype(I32), dest[:n_tok, 1].astype(I32))
    tok = jnp.arange(n_tok, dtype=I32)
    slot_token = jnp.zeros((n_blocks * rows,), I32).at[jnp.concatenate(pos)].set(
        jnp.concatenate([tok, tok]), unique_indices=True)
    starts = jnp.arange(n_blocks, dtype=I32) * rows
    block_expert = jnp.minimum(jnp.sum((cum[None, :] <= starts[:, None]).astype(I32), axis=1), n_experts - 1)
    n_used = (cum[-1] // rows).astype(I32).reshape(1)
    return slot_token, pos, block_expert, n_used


def _combine_kernel(pos_ref, posn_ref, x1_ref, rt_ref, m_ref, gfin_ref, ys_hbm, o_ref, ybuf, sem, *, tm):
    t = pl.program_id(0)
    nt = pl.num_programs(0)

    def row_copy(p, slot, r):
        return pltpu.make_async_copy(ys_hbm.at[pl.ds(p, 1)], ybuf.at[slot, pl.ds(r, 1)], sem.at[slot])

    def issue(idx_ref, slot):
        for r in range(2 * tm):
            row_copy(idx_ref[0, 0, r], slot, r).start(priority=r % 2)

    @pl.when(t == 0)
    def _prime():
        issue(pos_ref, 0)

    @pl.when(t + 1 < nt)
    def _prefetch():
        issue(posn_ref, (t + 1) % 2)

    slot = t % 2
    pltpu.make_async_copy(ys_hbm.at[pl.ds(0, 2 * tm)], ybuf.at[slot], sem.at[slot]).wait()

    rt = rt_ref[...]
    moe = rt[:, 2:3] * ybuf[slot, 0:tm, :] + rt[:, 3:4] * ybuf[slot, tm:2 * tm, :]
    x2 = x1_ref[...] + m_ref[0, 5] * moe
    y = x2 * lax.rsqrt(jnp.mean(x2 * x2, axis=-1, keepdims=True) + RMS_EPS)
    o_ref[...] = y * gfin_ref[...]


def _combine(x1, rt, mod, g_final, ys, pos, t_per_b, tm):
    m, d = x1.shape
    nt_all = m // tm
    nt_b = t_per_b // tm
    r = mod.shape[2]
    pos3 = jnp.concatenate([p.reshape(nt_all, 1, tm) for p in pos], axis=2)
    return pl.pallas_call(
        functools.partial(_combine_kernel, tm=tm),
        out_shape=jax.ShapeDtypeStruct((m, d), F32),
        grid=(nt_all,),
        in_specs=[pl.BlockSpec((1, 1, 2 * tm), lambda t: (t, 0, 0), memory_space=pltpu.SMEM),
                  pl.BlockSpec((1, 1, 2 * tm), lambda t: (jnp.minimum(t + 1, nt_all - 1), 0, 0),
                               memory_space=pltpu.SMEM),
                  pl.BlockSpec((tm, d), lambda t: (t, 0)),
                  pl.BlockSpec((tm, LANES), lambda t: (t, 0)),
                  pl.BlockSpec((1, 6, r, d), lambda t: (t // nt_b, 0, 0, 0)),
                  pl.BlockSpec((1, d), lambda t: (0, 0)),
                  pl.BlockSpec(memory_space=pl.ANY)],
        out_specs=pl.BlockSpec((tm, d), lambda t: (t, 0)),
        scratch_shapes=[pltpu.VMEM((2, 2 * tm, d), F32), pltpu.SemaphoreType.DMA((2,))],
        compiler_params=_cparams(("arbitrary",)),
        name="moe_combine",
    )(pos3, pos3, x1, rt, mod, g_final.reshape(1, d), ys)


def _forward(cfg, x_prompt, x_sample, cache_a_k, cache_a_v, cache_idx_k, cache_b_k, cache_b_v,
             page_table, c_prompt, c_sample, w_ada, b_ada, g_mix, g_ffn, g_final, w_in,
             w_up_a, w_up_b, w_out, w_group, b_group, w_router, b_router, w_gate_e, w_up_e, w_down_e):
    depth = w_in.shape[0]
    bp, seq, d = x_prompt.shape
    db, dseq, _ = x_sample.shape
    assert dseq == 1, "sample path is written for single-token decode"
    hd = cfg.head_dim
    off, wpad = _layout(cfg)
    ps = cfg.page_size
    ng, epg = cfg.n_groups, cfg.epg
    n_exp = ng * epg
    col = lambda a, name: a[..., off[name][0]:off[name][0] + off[name][1]]

    xp, xs = x_prompt, x_sample.reshape(1, db, d)
    c_all = jnp.concatenate([c_prompt, c_sample, jnp.zeros((-(bp + db) % 8, d), F32)], axis=0)
    rows_p, rows_s = [], []
    for layer in range(depth):
        mod = _adaln(c_all, w_ada[layer], b_ada[layer])
        mod_p = mod[:bp].reshape(bp, 6, 1, d)
        mod_s = mod[bp:bp + db].reshape(db, 6, d).transpose(1, 0, 2).reshape(1, 6, db, d)
        w_al = _permute_w_in(w_in[layer], cfg)
        def parts(w32):
            hi = w32.astype(BF16)
            return jnp.stack([hi, (w32 - hi.astype(F32)).astype(BF16)])

        wua, wub, wo = parts(w_up_a[layer]), parts(w_up_b[layer]), parts(w_out[layer])
        wr = parts(jnp.concatenate([w_group[layer], w_router[layer].transpose(1, 0, 2).reshape(d, n_exp),
                                    jnp.zeros((d, LANES - ng - n_exp), F32)], axis=1))
        br = jnp.concatenate([b_group[layer], b_router[layer].reshape(-1),
                              jnp.zeros((LANES - ng - n_exp,), F32)]).reshape(1, LANES)

        ts_p = min(256, seq)
        h_p = _rms_mod(xp, g_mix[layer], mod_p, ts_p).reshape(bp * seq, d)
        u_p = _matmul(h_p, w_al, min(cfg.tm_in, bp * seq), cfg.tn_in)
        u_p3 = u_p.reshape(bp, seq, wpad)
        oa_p = _dsa_prompt(u_p3, cfg).reshape(bp * seq, -1)
        ob_p = _sb_prompt(u_p3, cfg).reshape(bp * seq, -1)

        h_s = _rms_mod(xs, g_mix[layer], mod_s, db, out_dtype=F32).reshape(db, d)
        u_s = _matmul(h_s, w_al, db, cfg.tn_in)
        qi3 = col(u_s, "qi").reshape(db, cfg.n_idx_heads, cfg.idx_dim)
        kiwi_s = col(u_s, "kiwi")
        wi3 = (kiwi_s[:, cfg.idx_dim:cfg.idx_dim + cfg.n_idx_heads]
               * (float(cfg.n_idx_heads * cfg.idx_dim) ** -0.5)).reshape(db, cfg.n_idx_heads, 1)
        c_ik = jnp.swapaxes(cache_idx_k[layer], 1, 2)
        c_ak = cache_a_k[layer].reshape(cache_a_k.shape[1], ps, hd)
        c_av = cache_a_v[layer].reshape(cache_a_v.shape[1], ps, hd)
        c_bk, c_bv = cache_b_k[layer], cache_b_v[layer]
        sc4 = _dsa_sample_scores(qi3, wi3, c_ik, page_table, cfg)
        bias, bias_new = _dsa_sample_select(sc4.reshape(db, -1), u_s, cfg)
        qa3 = col(u_s, "qa").reshape(db, cfg.n_heads_a, hd)
        oa_s = _dsa_sample_attend(qa3, bias, bias_new, u_s.reshape(db, 1, wpad), c_ak, c_av,
                                  page_table, cfg).reshape(db, -1)
        rep = cfg.n_heads_b // cfg.n_kv_b
        qb3 = col(u_s, "qb").reshape(db, cfg.n_heads_b, 1, hd)
        gsel = (jnp.arange(cfg.n_heads_b)[:, None] // rep == jnp.arange(cfg.n_kv_b)[None, :])
        q_bd = (qb3 * gsel[None, :, :, None].astype(F32)).reshape(db, cfg.n_heads_b, cfg.n_kv_b * hd)
        ob_s = _sb_sample(q_bd, c_bk, c_bv, page_table, cfg).reshape(db, -1)

        x1_p, h2_p, rt_p = _merge(xp, oa_p, ob_p, u_p, mod_p, g_ffn[layer], wua[:1], wub[:1], wo[:1], wr, br, cfg,
                                  min(cfg.tm_merge, seq))
        x1_s, h2_s, rt_s = _merge(xs, oa_s, ob_s, u_s, mod_s, g_ffn[layer], wua, wub, wo, wr, br, cfg, db)

        h2_all = jnp.concatenate([h2_p, h2_s], axis=0)
        rt_all = jnp.concatenate([rt_p, rt_s], axis=0)
        slot_token, pos, block_expert, n_used = _moe_plan(rt_all, n_exp, cfg.moe_rows, cfg.top_k,
                                                          min(256, bp * seq))
        ys = _moe_experts(h2_all, slot_token, block_expert, n_used,
                          w_gate_e[layer], w_up_e[layer], w_down_e[layer], cfg.moe_rows)
        last = layer == depth - 1
        gfin = g_final if last else jnp.ones_like(g_final)
        assert last, "deeper stacks need an un-normalised combine for inner layers"
        n_p = bp * seq
        y_p = _combine(x1_p, rt_p, mod_p, gfin, ys, tuple(p[:n_p] for p in pos), seq, min(cfg.tm_comb, seq))
        y_s = _combine(x1_s, rt_s, mod_s, gfin, ys, tuple(p[n_p:] for p in pos), db, db)

        pages = lambda a: a.reshape((bp, seq // ps, ps) + a.shape[2:])
        rows_p.append((pages(col(u_p3, "ka").reshape(bp, seq, 1, hd)),
                       pages(col(u_p3, "va").reshape(bp, seq, 1, hd)),
                       pages(col(u_p3, "kiwi")[..., :cfg.idx_dim]),
                       pages(col(u_p3, "kb").reshape(bp, seq, cfg.n_kv_b, hd)),
                       pages(col(u_p3, "vb").reshape(bp, seq, cfg.n_kv_b, hd))))
        rows_s.append((col(u_s, "ka").reshape(db, 1, 1, hd), col(u_s, "va").reshape(db, 1, 1, hd),
                       kiwi_s[:, :cfg.idx_dim].reshape(db, 1, cfg.idx_dim),
                       col(u_s, "kb").reshape(db, 1, cfg.n_kv_b, hd),
                       col(u_s, "vb").reshape(db, 1, cfg.n_kv_b, hd)))
        xp, xs = y_p.reshape(bp, seq, d), y_s.reshape(1, db, d)

    stack = lambda rows: tuple(jnp.stack(r) for r in zip(*rows))
    return (xp, xs.reshape(db, 1, d)) + stack(rows_p) + stack(rows_s)


def kernel(x_prompt, x_sample, cache_a_k, cache_a_v, cache_idx_k, cache_b_k, cache_b_v, page_table,
           c_prompt, c_sample, w_ada, b_ada, g_mix, g_ffn, g_final, w_in, w_up_a, w_up_b, w_out,
           w_group, b_group, w_router, b_router, w_gate_e, w_up_e, w_down_e):
    return _forward(Cfg(), x_prompt, x_sample, cache_a_k, cache_a_v, cache_idx_k, cache_b_k, cache_b_v,
                    page_table, c_prompt, c_sample, w_ada, b_ada, g_mix, g_ffn, g_final, w_in,
                    w_up_a, w_up_b, w_out, w_group, b_group, w_router, b_router, w_gate_e, w_up_e, w_down_e)
```
